```python
import jax, jax.numpy as jnp
from jax import lax
import numpy as np

D_MODEL = 1024
BATCH = 8
SEQ = 2048
DEPTH = 1
DEC_BATCH = 128
DEC_SEQ = 8
PAST_LEN = 16384
PAGE_SIZE = 128

MIX_WIDTH = D_MODEL
CONV_WIDTH = MIX_WIDTH // 2
RWKV_WIDTH = MIX_WIDTH - CONV_WIDTH
HEAD_SIZE = 64
N_HEADS = RWKV_WIDTH // HEAD_SIZE
CONV_K = 3
D_DECAY_LORA = 64
D_AAA_LORA = 64
D_GATE_LORA = 128
D_FF = 4 * D_MODEL
RWKV_COLS = 3 * RWKV_WIDTH + D_DECAY_LORA + D_AAA_LORA + D_GATE_LORA
IN_COLS = 3 * CONV_WIDTH + RWKV_COLS
RMS_EPS = 1e-6
GN_EPS = 64e-5
NORM_EPS = 1e-12

kernel_name = "hymba_shortconv_rwkv7_step"


def _rmsnorm(x, g):
    xf = x.astype(jnp.float32)
    y = xf * lax.rsqrt(jnp.mean(xf * xf, axis=-1, keepdims=True) + RMS_EPS) * g.astype(jnp.float32)
    return y.astype(x.dtype)


def _short_conv(u, buf, conv_w):
    t_len = u.shape[1]
    full = jnp.concatenate([buf.astype(u.dtype), u], axis=1)
    out = sum(full[:, j:j + t_len] * conv_w[j] for j in range(CONV_K))
    return out, full[:, -(CONV_K - 1):]


def _wkv_scan(s0, r, decay, k, v, kk, a):
    def step(s, inp):
        r_t, w_t, k_t, v_t, kk_t, a_t = inp
        s_kk = jnp.einsum('bhvk,bhk->bhv', s, kk_t)
        s = (s * w_t[:, :, None, :]
             - s_kk[..., None] * (kk_t * a_t)[:, :, None, :]
             + v_t[..., None] * k_t[:, :, None, :])
        y_t = jnp.einsum('bhvk,bhk->bhv', s, r_t)
        return s, y_t
    xs = tuple(jnp.swapaxes(t, 0, 1) for t in (r, decay, k, v, kk, a))
    s_final, ys = lax.scan(step, s0, xs)
    return jnp.swapaxes(ys, 0, 1), s_final


def _layer(x, conv_buf, shift_buf, wkv_state, norm1_g, w_in, conv_w, mu, w0, w2, a0, a2, g2,
           k_k, k_a, r_k, lnx_g, lnx_b, w_out, norm2_g, w_ff1, w_ff2):
    bsz, t_len, _ = x.shape
    f32 = jnp.float32
    h = _rmsnorm(x, norm1_g)
    proj = h @ w_in
    c_b, c_c, c_x, p_rw = jnp.split(proj, [CONV_WIDTH, 2 * CONV_WIDTH, 3 * CONV_WIDTH], axis=-1)

    conv_out, new_conv = _short_conv(c_c * c_x, conv_buf, conv_w)
    y_conv = c_b * conv_out

    prev = jnp.concatenate([shift_buf[:, None, :].astype(p_rw.dtype), p_rw[:, :-1]], axis=1)
    xm = (p_rw + (prev - p_rw) * mu).astype(f32)
    new_shift = p_rw[:, -1]
    r, k, v, pw, pa, pg = jnp.split(
        xm, [RWKV_WIDTH, 2 * RWKV_WIDTH, 3 * RWKV_WIDTH, 3 * RWKV_WIDTH + D_DECAY_LORA,
             3 * RWKV_WIDTH + D_DECAY_LORA + D_AAA_LORA], axis=-1)
    w = -jax.nn.softplus(-(w0.astype(f32) + jnp.tanh(pw) @ w2.astype(f32))) - 0.5
    decay = jnp.exp(-jnp.exp(w))
    a = jax.nn.sigmoid(a0.astype(f32) + pa @ a2.astype(f32))
    g = jax.nn.sigmoid(pg) @ g2.astype(f32)

    def heads(t):
        return t.reshape(bsz, t_len, N_HEADS, HEAD_SIZE)
    r, k, v, decay, a = heads(r), heads(k), heads(v), heads(decay), heads(a)
    kk = k * k_k.astype(f32).reshape(N_HEADS, HEAD_SIZE)
    kk = kk / jnp.maximum(jnp.sqrt(jnp.sum(kk * kk, axis=-1, keepdims=True)), NORM_EPS)
    k = k * (1.0 + (a - 1.0) * k_a.astype(f32).reshape(N_HEADS, HEAD_SIZE))
    y, new_wkv = _wkv_scan(wkv_state.astype(f32), r, decay, k, v, kk, a)
    mean = jnp.mean(y, axis=-1, keepdims=True)
    var = jnp.mean(jnp.square(y - mean), axis=-1, keepdims=True)
    y = ((y - mean) * lax.rsqrt(var + GN_EPS)).reshape(bsz, t_len, RWKV_WIDTH)
    y = y * lnx_g.astype(f32) + lnx_b.astype(f32)
    bonus = jnp.sum(r * k * r_k.astype(f32), axis=-1, keepdims=True) * v
    y_rwkv = ((y + bonus.reshape(bsz, t_len, RWKV_WIDTH)) * g).astype(x.dtype)

    x = x + jnp.concatenate([y_conv, y_rwkv], axis=-1) @ w_out
    h2 = _rmsnorm(x, norm2_g)
    x = x + jnp.square(jax.nn.relu(h2 @ w_ff1)) @ w_ff2
    return x, new_conv, new_shift, new_wkv.astype(x.dtype)


def setup_inputs(seed: int = 0) -> dict:
    key = jax.random.key(seed)
    ks = jax.random.split(key, 32)
    f32 = jnp.float32
    nrm = lambda k, shape, s: jax.random.normal(k, shape, f32) * s
    return {
        "x_prompt": nrm(ks[0], (BATCH, SEQ, D_MODEL), 1.0),
        "x_sample": nrm(ks[1], (DEC_BATCH, DEC_SEQ, D_MODEL), 1.0),
        "state_conv": nrm(ks[2], (DEPTH, DEC_BATCH, CONV_K - 1, CONV_WIDTH), 1.0),
        "state_shift": nrm(ks[3], (DEPTH, DEC_BATCH, RWKV_COLS), 1.0),
        "state_wkv": nrm(ks[4], (DEPTH, DEC_BATCH, N_HEADS, HEAD_SIZE, HEAD_SIZE), 0.1),
        "norm1_g": 1.0 + nrm(ks[5], (DEPTH, D_MODEL), 0.02),
        "w_in": nrm(ks[6], (DEPTH, D_MODEL, IN_COLS), D_MODEL ** -0.5),
        "conv_w": nrm(ks[7], (DEPTH, CONV_K, CONV_WIDTH), CONV_K ** -0.5),
        "mu": jax.random.uniform(ks[8], (DEPTH, RWKV_COLS), f32, 0.0, 1.0),
        "w0": jax.random.uniform(ks[9], (DEPTH, RWKV_WIDTH), f32, -4.0, 0.0),
        "w2": nrm(ks[10], (DEPTH, D_DECAY_LORA, RWKV_WIDTH), 0.1 * D_DECAY_LORA ** -0.5),
        "a0": nrm(ks[11], (DEPTH, RWKV_WIDTH), 0.1),
        "a2": nrm(ks[12], (DEPTH, D_AAA_LORA, RWKV_WIDTH), 0.1 * D_AAA_LORA ** -0.5),
        "g2": nrm(ks[13], (DEPTH, D_GATE_LORA, RWKV_WIDTH), D_GATE_LORA ** -0.5),
        "k_k": 0.85 + nrm(ks[14], (DEPTH, RWKV_WIDTH), 0.02),
        "k_a": 1.0 + nrm(ks[15], (DEPTH, RWKV_WIDTH), 0.02),
        "r_k": nrm(ks[16], (DEPTH, N_HEADS, HEAD_SIZE), 0.1),
        "lnx_g": 1.0 + nrm(ks[17], (DEPTH, RWKV_WIDTH), 0.02),
        "lnx_b": nrm(ks[18], (DEPTH, RWKV_WIDTH), 0.02),
        "w_out": nrm(ks[19], (DEPTH, MIX_WIDTH, D_MODEL), MIX_WIDTH ** -0.5),
        "norm2_g": 1.0 + nrm(ks[20], (DEPTH, D_MODEL), 0.02),
        "w_ff1": nrm(ks[21], (DEPTH, D_MODEL, D_FF), D_MODEL ** -0.5),
        "w_ff2": nrm(ks[22], (DEPTH, D_FF, D_MODEL), D_FF ** -0.5),
        "normf_g": 1.0 + nrm(ks[23], (D_MODEL,), 0.02),
    }


def reference(x_prompt, x_sample, state_conv, state_shift, state_wkv, norm1_g, w_in, conv_w, mu,
              w0, w2, a0, a2, g2, k_k, k_a, r_k, lnx_g, lnx_b, w_out, norm2_g, w_ff1, w_ff2,
              normf_g):
    dt = x_prompt.dtype
    bp = x_prompt.shape[0]
    hp, hs = x_prompt, x_sample
    conv_p, shift_p, wkv_p, conv_s, shift_s, wkv_s = [], [], [], [], [], []
    for layer in range(DEPTH):
        weights = (norm1_g[layer], w_in[layer], conv_w[layer], mu[layer], w0[layer], w2[layer],
                   a0[layer], a2[layer], g2[layer], k_k[layer], k_a[layer], r_k[layer],
                   lnx_g[layer], lnx_b[layer], w_out[layer], norm2_g[layer], w_ff1[layer],
                   w_ff2[layer])
        hp, c_p, s_p, m_p = _layer(
            hp, jnp.zeros((bp, CONV_K - 1, CONV_WIDTH), dt), jnp.zeros((bp, RWKV_COLS), dt),
            jnp.zeros((bp, N_HEADS, HEAD_SIZE, HEAD_SIZE), dt), *weights)
        hs, c_s, s_s, m_s = _layer(hs, state_conv[layer], state_shift[layer], state_wkv[layer],
                                   *weights)
        conv_p.append(c_p); shift_p.append(s_p); wkv_p.append(m_p)
        conv_s.append(c_s); shift_s.append(s_s); wkv_s.append(m_s)
    y_prompt = _rmsnorm(hp, normf_g)
    y_sample = _rmsnorm(hs, normf_g)
    return (y_prompt, y_sample, jnp.stack(conv_p), jnp.stack(shift_p), jnp.stack(wkv_p),
            jnp.stack(conv_s), jnp.stack(shift_s), jnp.stack(wkv_s))
```

```python
import functools

import jax
import jax.numpy as jnp
from jax import lax
from jax.experimental import pallas as pl
from jax.experimental.pallas import tpu as pltpu

F32 = jnp.float32
BF16 = jnp.bfloat16

HEAD = 64
RMS_EPS = 1e-6
GN_EPS = 64e-5
NORM_EPS = 1e-12
VMEM_LIMIT = 56 * 1024 * 1024
CHUNK = 64
GROUP = 4
HI = lax.Precision.HIGHEST


def _dot(a, b):
    return jnp.dot(a, b, preferred_element_type=F32)


def _rmsnorm(x, g):
    return x * lax.rsqrt(jnp.mean(x * x, axis=-1, keepdims=True) + RMS_EPS) * g


def _split3(x):
    x1 = x.astype(BF16)
    r1 = x - x1.astype(F32)
    x2 = r1.astype(BF16)
    x3 = (r1 - x2.astype(F32)).astype(BF16)
    return x1, x2, x3


def _seg_sum(x, bd):
    x1, x2, x3 = _split3(x)
    return _dot(x1, bd) + _dot(x2, bd) + _dot(x3, bd)


def _shift_rows(x, k, head):
    rows = x.shape[0]
    if k == rows:
        return head
    if k % 8 == 0:
        return jnp.concatenate([head, x[:rows - k]], axis=0)
    rolled = pltpu.roll(x, k, 0)
    row = lax.broadcasted_iota(jnp.int32, x.shape, 0)
    out = rolled
    for j in range(k):
        out = jnp.where(row == j, head[j:j + 1, :], out)
    return out


def _front_kernel(s, cw_dim, rw_dim,
                  x_ref, conv0_ref, shift0_ref, n1_ref, win_ref, cw_ref, mu_ref, w0_ref, w2_ref,
                  a0_ref, a2_ref, g2_ref, kk_w_ref, ka_w_ref, bd_ref,
                  yconv_ref, r_ref, k_ref, v_ref, kk_ref, a_ref, lw_ref, g_ref,
                  nconv_ref, nshift_ref,
                  cu_ref, cp_ref):
    @pl.when(pl.program_id(1) == 0)
    def _():
        cu_ref[...] = conv0_ref[...]
        cp_ref[...] = shift0_ref[...]

    x = x_ref[...]
    rows = x.shape[0]
    h = _rmsnorm(x, n1_ref[...]).astype(BF16)
    proj = _dot(h, win_ref[...])
    c_b = proj[:, :cw_dim]
    c_c = proj[:, cw_dim:2 * cw_dim]
    c_x = proj[:, 2 * cw_dim:3 * cw_dim]
    p = proj[:, 3 * cw_dim:]

    u = c_c * c_x
    cu = cu_ref[...]
    u1 = _shift_rows(u, s, cu[s:2 * s])
    u2 = _shift_rows(u, 2 * s, cu)
    cw = cw_ref[...]
    conv = cw[0:1] * u2 + cw[1:2] * u1 + cw[2:3] * u
    yconv_ref[...] = (c_b * conv).astype(yconv_ref.dtype)
    new_cu = u[rows - 2 * s:]
    cu_ref[...] = new_cu
    nconv_ref[...] = new_cu

    prev = _shift_rows(p, s, cp_ref[...])
    new_cp = p[rows - s:]
    cp_ref[...] = new_cp
    nshift_ref[...] = new_cp
    xm = p + (prev - p) * mu_ref[...]
    r = xm[:, :rw_dim]
    k = xm[:, rw_dim:2 * rw_dim]
    v = xm[:, 2 * rw_dim:3 * rw_dim]
    o = 3 * rw_dim
    d_w = w2_ref.shape[0]
    d_a = a2_ref.shape[0]
    pw = xm[:, o:o + d_w]
    pa = xm[:, o + d_w:o + d_w + d_a]
    pg = xm[:, o + d_w + d_a:]

    wl = w0_ref[...] + _dot(jnp.tanh(pw).astype(BF16), w2_ref[...])
    z = -wl
    softplus = jnp.maximum(z, 0.0) + jnp.log(1.0 + jnp.exp(-jnp.abs(z)))
    lw_ref[...] = -jnp.exp(-softplus - 0.5)
    a = 1.0 / (1.0 + jnp.exp(-(a0_ref[...] + _dot(pa.astype(BF16), a2_ref[...]))))
    g_ref[...] = _dot((1.0 / (1.0 + jnp.exp(-pg))).astype(BF16), g2_ref[...])

    kk = k * kk_w_ref[...]
    nrm = jnp.sqrt(_seg_sum(kk * kk, bd_ref[...]))
    kk_ref[...] = kk / jnp.maximum(nrm, NORM_EPS)
    k_ref[...] = k * (1.0 + (a - 1.0) * ka_w_ref[...])
    r_ref[...] = r
    v_ref[...] = v
    a_ref[...] = a


def _const_spec(shape):
    nd = len(shape)
    return pl.BlockSpec(shape, lambda *_: (0,) * nd, pipeline_mode=pl.Buffered(1))


def _front(x, conv0, shift0, s, tb, wts):
    ns, tt, d = x.shape
    cw_dim = conv0.shape[-1]
    rw_dim = wts["w0"].shape[-1]
    rcols = shift0.shape[-1]
    nblk = tt // tb
    row_spec = lambda w: pl.BlockSpec((None, tb, w), lambda b, i: (b, i, 0))
    st_spec = lambda r, w: pl.BlockSpec((None, r, w), lambda b, i: (b, 0, 0))
    names = ["norm1_g", "w_in", "conv_w", "mu", "w0", "w2", "a0", "a2", "g2", "k_k", "k_a", "bd"]
    w_in = [wts[n] for n in names]
    out_shapes = ([jax.ShapeDtypeStruct((ns, tt, cw_dim), BF16)]
                  + [jax.ShapeDtypeStruct((ns, tt, rw_dim), F32)] * 7
                  + [jax.ShapeDtypeStruct((ns, 2 * s, cw_dim), F32),
                     jax.ShapeDtypeStruct((ns, s, rcols), F32)])
    out_specs = ([row_spec(cw_dim)] + [row_spec(rw_dim)] * 7
                 + [st_spec(2 * s, cw_dim), st_spec(s, rcols)])
    return pl.pallas_call(
        functools.partial(_front_kernel, s, cw_dim, rw_dim),
        grid=(ns, nblk),
        in_specs=[row_spec(d), st_spec(2 * s, cw_dim), st_spec(s, rcols)]
        + [_const_spec(w.shape) for w in w_in],
        out_specs=out_specs,
        out_shape=out_shapes,
        scratch_shapes=[pltpu.VMEM((2 * s, cw_dim), F32), pltpu.VMEM((s, rcols), F32)],
        compiler_params=pltpu.CompilerParams(
            dimension_semantics=("arbitrary", "arbitrary"), vmem_limit_bytes=VMEM_LIMIT),
        name="front",
    )(x, conv0, shift0, *w_in)


def _back_kernel(x_ref, yconv_ref, y_ref, r_ref, k_ref, v_ref, g_ref,
                 rk_ref, lg_ref, lb_ref, wo_ref, n2_ref, f1_ref, f2_ref, nf_ref, bd_ref,
                 out_ref):
    bd = bd_ref[...]
    y = y_ref[...]
    inv_n = 1.0 / HEAD
    mean = _seg_sum(y, bd) * inv_n
    dlt = y - mean
    var = _seg_sum(dlt * dlt, bd) * inv_n
    yn = dlt * lax.rsqrt(var + GN_EPS) * lg_ref[...] + lb_ref[...]
    v = v_ref[...]
    bonus = _seg_sum(r_ref[...] * k_ref[...] * rk_ref[...], bd) * v
    y_rwkv = ((yn + bonus) * g_ref[...]).astype(BF16)
    cw_dim = yconv_ref.shape[-1]
    x = x_ref[...]
    x1 = x + _dot(yconv_ref[...], wo_ref[:cw_dim, :]) + _dot(y_rwkv, wo_ref[cw_dim:, :])
    h2 = _rmsnorm(x1, n2_ref[...]).astype(BF16)
    f = jnp.maximum(_dot(h2, f1_ref[...]), 0.0)
    x2 = x1 + _dot((f * f).astype(BF16), f2_ref[...])
    out_ref[...] = _rmsnorm(x2, nf_ref[...])


def _back(x, yconv, y, r, k, v, g, tb, wts):
    ns, tt, d = x.shape
    nblk = tt // tb
    row_spec = lambda w: pl.BlockSpec((None, tb, w), lambda b, i: (b, i, 0))
    names = ["r_k", "lnx_g", "lnx_b", "w_out", "norm2_g", "w_ff1", "w_ff2", "normf_g", "bd"]
    w_in = [wts[n] for n in names]
    acts = [x, yconv, y, r, k, v, g]
    return pl.pallas_call(
        _back_kernel,
        grid=(ns, nblk),
        in_specs=[row_spec(a.shape[-1]) for a in acts] + [_const_spec(w.shape) for w in w_in],
        out_specs=row_spec(d),
        out_shape=jax.ShapeDtypeStruct((ns, tt, d), F32),
        compiler_params=pltpu.CompilerParams(
            dimension_semantics=("arbitrary", "arbitrary"), vmem_limit_bytes=VMEM_LIMIT),
        name="back",
    )(*acts, *w_in)


def _mm(a, b):
    return jnp.dot(a, b, preferred_element_type=F32, precision=HI)


def _mm_nt(a, b):
    return lax.dot_general(a, b, (((1,), (1,)), ((), ())), preferred_element_type=F32,
                           precision=HI)


def _mm_tn(a, b):
    return lax.dot_general(a, b, (((0,), (0,)), ((), ())), preferred_element_type=F32,
                           precision=HI)


def _chunk_chain(lw, kk, a, r, k, v, z0, consts):
    tril, lane_head_mask, strict_h, incl_h, eye_h, bd_cc, bd_ll, eye_ll = consts
    c = lw.shape[0]

    def bd_rows(m):
        return jnp.where(lane_head_mask, jnp.concatenate([m] * GROUP, axis=0), 0.0)

    def bd_mat(m):
        return jnp.where(bd_cc, jnp.concatenate([m] * GROUP, axis=0), 0.0)

    cum = _mm(tril, lw)
    cum_prev = cum - lw
    cum_last = cum[c - 1:c, :]
    beta = kk * a
    alpha_t = -kk * jnp.exp(cum_prev)
    r_t = r * jnp.exp(cum)
    inv = jnp.exp(-cum)
    rem = jnp.exp(cum_last - cum)
    ar = jnp.concatenate([alpha_t, r_t], axis=0)
    g_b = _mm_nt(ar, bd_rows(beta * inv))
    g_k = _mm_nt(ar, bd_rows(k * inv))
    a_ab = jnp.where(strict_h, g_b[:c], 0.0)
    a_rb = jnp.where(incl_h, g_b[c:], 0.0)
    a_ak = jnp.where(strict_h, g_k[:c], 0.0)
    a_rk = jnp.where(incl_h, g_k[c:], 0.0)

    xp = a_ab
    t_h = eye_h + a_ab
    n_sq = max(c.bit_length() - 2, 0)
    for _ in range(n_sq):
        xp = _mm(xp, bd_mat(xp))
        t_h = t_h + _mm(t_h, bd_mat(xp))

    v_bd = bd_rows(v)
    akv = _mm(a_ak, v_bd)
    wa = _mm(t_h, bd_rows(alpha_t))
    uv = _mm(t_h, bd_rows(akv))
    rq = r_t + _mm(a_rb, bd_rows(wa))
    yv = _mm(a_rk, v_bd) + _mm(a_rb, bd_rows(uv))
    b_h = beta * rem
    k_h = k * rem
    m_bd = jnp.where(bd_ll, _mm_tn(b_h, wa), 0.0) + jnp.where(eye_ll, jnp.exp(cum_last), 0.0)
    n_bd = jnp.where(bd_ll, _mm_tn(b_h, uv) + _mm_tn(k_h, v), 0.0)
    y = _mm(rq, z0) + yv
    z1 = _mm(m_bd, z0) + n_bd
    return y, z1


def _pscan_kernel(kk_ref, a_ref, lw_ref, k_ref, r_ref, v_ref, y_ref, zout_ref, z_ref):
    @pl.when(pl.program_id(1) == 0)
    def _():
        z_ref[...] = jnp.zeros_like(z_ref)

    c = kk_ref.shape[0]
    lanes = GROUP * HEAD
    gc = GROUP * c
    i0 = lambda shape: lax.broadcasted_iota(jnp.int32, shape, 0)
    i1 = lambda shape: lax.broadcasted_iota(jnp.int32, shape, 1)
    tril = (i1((c, c)) <= i0((c, c))).astype(F32)
    lane_head_mask = (i0((gc, lanes)) // c) == (i1((gc, lanes)) // HEAD)
    col_t = i1((c, gc)) % c
    strict_h = col_t < i0((c, gc))
    incl_h = col_t <= i0((c, gc))
    eye_h = (col_t == i0((c, gc))).astype(F32)
    bd_cc = (i0((gc, gc)) // c) == (i1((gc, gc)) // c)
    bd_ll = (i0((lanes, lanes)) // HEAD) == (i1((lanes, lanes)) // HEAD)
    eye_ll = i0((lanes, lanes)) == i1((lanes, lanes))
    consts = (tril, lane_head_mask, strict_h, incl_h, eye_h, bd_cc, bd_ll, eye_ll)

    n_groups = kk_ref.shape[1] // lanes
    for g in range(n_groups):
        sl = slice(g * lanes, (g + 1) * lanes)
        y, z1 = _chunk_chain(lw_ref[:, sl], kk_ref[:, sl], a_ref[:, sl], r_ref[:, sl],
                             k_ref[:, sl], v_ref[:, sl], z_ref[g], consts)
        y_ref[:, sl] = y
        z_ref[g] = z1
        zout_ref[g] = z1


def _prompt_scan(kk, a, lw, k, r, v):
    nb, t, w = kk.shape
    lanes = GROUP * HEAD
    ng = w // lanes
    spec = pl.BlockSpec((None, CHUNK, w), lambda b, i: (b, i, 0))
    return pl.pallas_call(
        _pscan_kernel,
        grid=(nb, t // CHUNK),
        in_specs=[spec] * 6,
        out_specs=[spec, pl.BlockSpec((None, ng, lanes, lanes), lambda b, i: (b, 0, 0, 0))],
        out_shape=[jax.ShapeDtypeStruct((nb, t, w), F32),
                   jax.ShapeDtypeStruct((nb, ng, lanes, lanes), F32)],
        scratch_shapes=[pltpu.VMEM((ng, lanes, lanes), F32)],
        compiler_params=pltpu.CompilerParams(
            dimension_semantics=("arbitrary", "arbitrary"), vmem_limit_bytes=VMEM_LIMIT),
        name="prompt_scan",
    )(kk, a, lw, k, r, v)


def _sscan_kernel(kk_ref, a_ref, lw_ref, k_ref, r_ref, v_ref, s_in_ref, y_ref, s_out_ref):
    n_t = kk_ref.shape[0]
    n_k = s_in_ref.shape[0]
    shape = s_in_ref.shape[1:]
    for t in range(n_t):
        src = s_in_ref if t == 0 else s_out_ref

        def p1(k8, acc, src=src, t=t):
            base = pl.multiple_of(k8 * 8, 8)
            kk8 = kk_ref[t, pl.ds(base, 8), :]
            for j in range(8):
                acc = acc + src[base + j] * kk8[j:j + 1, :]
            return acc

        sk = lax.fori_loop(0, n_k // 8, p1, jnp.zeros(shape, F32))
        vt = v_ref[t]

        def p2(k8, acc, src=src, t=t, sk=sk, vt=vt):
            base = pl.multiple_of(k8 * 8, 8)
            kk8 = kk_ref[t, pl.ds(base, 8), :]
            kka8 = kk8 * a_ref[t, pl.ds(base, 8), :]
            w8 = jnp.exp(lw_ref[t, pl.ds(base, 8), :])
            k8v = k_ref[t, pl.ds(base, 8), :]
            r8 = r_ref[t, pl.ds(base, 8), :]
            for j in range(8):
                snew = (src[base + j] * w8[j:j + 1, :] - sk * kka8[j:j + 1, :]
                        + vt * k8v[j:j + 1, :])
                s_out_ref[base + j] = snew
                acc = acc + snew * r8[j:j + 1, :]
            return acc

        y_ref[t] = lax.fori_loop(0, n_k // 8, p2, jnp.zeros(shape, F32))


def _sample_scan(kk, a, lw, k, r, v, state):
    n_t, w, nb = kk.shape
    nh = state.shape[0]
    spec = pl.BlockSpec((n_t, HEAD, nb), lambda h: (0, h, 0))
    sspec = pl.BlockSpec((None, HEAD, HEAD, nb), lambda h: (h, 0, 0, 0))
    return pl.pallas_call(
        _sscan_kernel,
        grid=(nh,),
        in_specs=[spec] * 6 + [sspec],
        out_specs=[spec, sspec],
        out_shape=[jax.ShapeDtypeStruct((n_t, w, nb), F32),
                   jax.ShapeDtypeStruct(state.shape, F32)],
        compiler_params=pltpu.CompilerParams(
            dimension_semantics=("arbitrary",), vmem_limit_bytes=VMEM_LIMIT),
        name="sample_scan",
    )(kk, a, lw, k, r, v, state)


def kernel(x_prompt, x_sample, state_conv, state_shift, state_wkv, norm1_g, w_in, conv_w, mu, w0,
           w2, a0, a2, g2, k_k, k_a, r_k, lnx_g, lnx_b, w_out, norm2_g, w_ff1, w_ff2, normf_g):
    depth = w_in.shape[0]
    assert depth == 1, "single-layer step only"
    bp, seq, d = x_prompt.shape
    bs, dseq, _ = x_sample.shape
    cw_dim = conv_w.shape[-1]
    rw_dim = w0.shape[-1]
    rcols = mu.shape[-1]
    nh = rw_dim // HEAD
    row = lambda t: t[0].reshape(1, -1)
    seg = jnp.arange(rw_dim) // HEAD
    wts = {
        "norm1_g": row(norm1_g), "w_in": w_in[0].astype(BF16), "conv_w": conv_w[0],
        "mu": row(mu), "w0": row(w0), "w2": w2[0].astype(BF16), "a0": row(a0),
        "a2": a2[0].astype(BF16), "g2": g2[0].astype(BF16), "k_k": row(k_k), "k_a": row(k_a),
        "r_k": row(r_k), "lnx_g": row(lnx_g), "lnx_b": row(lnx_b),
        "w_out": w_out[0].astype(BF16), "norm2_g": row(norm2_g),
        "w_ff1": w_ff1[0].astype(BF16), "w_ff2": w_ff2[0].astype(BF16),
        "normf_g": normf_g.reshape(1, -1),
        "bd": (seg[:, None] == seg[None, :]).astype(BF16),
    }

    tb = 256
    zc = jnp.zeros((bp, 2, cw_dim), F32)
    zs = jnp.zeros((bp, 1, rcols), F32)
    yconv, r, k, v, kk, a, lw, g, nconv_p, nshift_p = _front(x_prompt, zc, zs, 1, tb, wts)
    y, z = _prompt_scan(kk, a, lw, k, r, v)
    y_prompt = _back(x_prompt, yconv, y, r, k, v, g, tb, wts)
    ng = rw_dim // (GROUP * HEAD)
    zb = z.reshape(bp, ng, GROUP, HEAD, GROUP, HEAD)
    idx = jnp.arange(GROUP)
    zh = zb[:, :, idx, :, idx, :]
    wkv_p = jnp.transpose(zh, (1, 2, 0, 4, 3)).reshape(bp, nh, HEAD, HEAD)

    xs = jnp.transpose(x_sample, (1, 0, 2)).reshape(1, dseq * bs, d)
    c0 = jnp.transpose(state_conv[0], (1, 0, 2)).reshape(1, 2 * bs, cw_dim)
    s0 = state_shift[0].reshape(1, bs, rcols)
    tbs = 2 * bs
    yconv, r, k, v, kk, a, lw, g, nconv_s, nshift_s = _front(xs, c0, s0, bs, tbs, wts)
    to_fm = lambda t: jnp.transpose(t.reshape(dseq, bs, rw_dim), (0, 2, 1))
    st = jnp.transpose(state_wkv[0], (1, 3, 2, 0))
    y_fm, st_new = _sample_scan(to_fm(kk), to_fm(a), to_fm(lw), to_fm(k), to_fm(r), to_fm(v), st)
    y = jnp.transpose(y_fm, (0, 2, 1)).reshape(1, dseq * bs, rw_dim)
    ys = _back(xs, yconv, y, r, k, v, g, tbs, wts)
    y_sample = jnp.transpose(ys.reshape(dseq, bs, d), (1, 0, 2))
    wkv_s = jnp.transpose(st_new, (3, 0, 2, 1))
    nconv_s = jnp.transpose(nconv_s.reshape(2, bs, cw_dim), (1, 0, 2))

    return (y_prompt, y_sample,
            nconv_p[None], nshift_p.reshape(1, bp, rcols), wkv_p[None],
            nconv_s[None], nshift_s.reshape(1, bs, rcols), wkv_s[None])
```

```python
import functools

import jax
import jax.numpy as jnp
from jax import lax
from jax.experimental import pallas as pl
from jax.experimental.pallas import tpu as pltpu

F32 = jnp.float32
BF16 = jnp.bfloat16

HEAD = 64
RMS_EPS = 1e-6
GN_EPS = 64e-5
NORM_EPS = 1e-12
VMEM_LIMIT = 56 * 1024 * 1024
CHUNK = 64
GROUP = 4


def _dot(a, b):
    return jnp.dot(a, b, preferred_element_type=F32)


def _rmsnorm(x, g):
    return x * lax.rsqrt(jnp.mean(x * x, axis=-1, keepdims=True) + RMS_EPS) * g


def _split3(x):
    x1 = x.astype(BF16)
    r1 = x - x1.astype(F32)
    x2 = r1.astype(BF16)
    x3 = (r1 - x2.astype(F32)).astype(BF16)
    return x1, x2, x3


def _seg_sum(x, bd):
    x1, x2, x3 = _split3(x)
    return _dot(x1, bd) + _dot(x2, bd) + _dot(x3, bd)


def _shift_rows(x, k, head):
    rows = x.shape[0]
    if k == rows:
        return head
    if k % 8 == 0:
        return jnp.concatenate([head, x[:rows - k]], axis=0)
    rolled = pltpu.roll(x, k, 0)
    row = lax.broadcasted_iota(jnp.int32, x.shape, 0)
    out = rolled
    for j in range(k):
        out = jnp.where(row == j, head[j:j + 1, :], out)
    return out


def _front_kernel(s, cw_dim, rw_dim,
                  x_ref, conv0_ref, shift0_ref, n1_ref, win_ref, cw_ref, mu_ref, w0_ref, w2_ref,
                  a0_ref, a2_ref, g2_ref, kk_w_ref, ka_w_ref, bd_ref,
                  yconv_ref, r_ref, k_ref, v_ref, kk_ref, a_ref, lw_ref, g_ref,
                  nconv_ref, nshift_ref,
                  cu_ref, cp_ref):
    @pl.when(pl.program_id(1) == 0)
    def _():
        cu_ref[...] = conv0_ref[...]
        cp_ref[...] = shift0_ref[...]

    x = x_ref[...]
    rows = x.shape[0]
    h = _rmsnorm(x, n1_ref[...]).astype(BF16)
    proj = _dot(h, win_ref[...])
    c_b = proj[:, :cw_dim]
    c_c = proj[:, cw_dim:2 * cw_dim]
    c_x = proj[:, 2 * cw_dim:3 * cw_dim]
    p = proj[:, 3 * cw_dim:]

    u = c_c * c_x
    cu = cu_ref[...]
    u1 = _shift_rows(u, s, cu[s:2 * s])
    u2 = _shift_rows(u, 2 * s, cu)
    cw = cw_ref[...]
    conv = cw[0:1] * u2 + cw[1:2] * u1 + cw[2:3] * u
    yconv_ref[...] = (c_b * conv).astype(yconv_ref.dtype)
    new_cu = u[rows - 2 * s:]
    cu_ref[...] = new_cu
    nconv_ref[...] = new_cu

    prev = _shift_rows(p, s, cp_ref[...])
    new_cp = p[rows - s:]
    cp_ref[...] = new_cp
    nshift_ref[...] = new_cp
    xm = p + (prev - p) * mu_ref[...]
    r = xm[:, :rw_dim]
    k = xm[:, rw_dim:2 * rw_dim]
    v = xm[:, 2 * rw_dim:3 * rw_dim]
    o = 3 * rw_dim
    d_w = w2_ref.shape[0]
    d_a = a2_ref.shape[0]
    pw = xm[:, o:o + d_w]
    pa = xm[:, o + d_w:o + d_w + d_a]
    pg = xm[:, o + d_w + d_a:]

    wl = w0_ref[...] + _dot(jnp.tanh(pw).astype(BF16), w2_ref[...])
    z = -wl
    softplus = jnp.maximum(z, 0.0) + jnp.log(1.0 + jnp.exp(-jnp.abs(z)))
    lw_ref[...] = -jnp.exp(-softplus - 0.5)
    a = 1.0 / (1.0 + jnp.exp(-(a0_ref[...] + _dot(pa.astype(BF16), a2_ref[...]))))
    g_ref[...] = _dot((1.0 / (1.0 + jnp.exp(-pg))).astype(BF16), g2_ref[...])

    kk = k * kk_w_ref[...]
    nrm = jnp.sqrt(_seg_sum(kk * kk, bd_ref[...]))
    kk_ref[...] = kk / jnp.maximum(nrm, NORM_EPS)
    k_ref[...] = k * (1.0 + (a - 1.0) * ka_w_ref[...])
    r_ref[...] = r
    v_ref[...] = v
    a_ref[...] = a


def _const_spec(shape):
    nd = len(shape)
    return pl.BlockSpec(shape, lambda *_: (0,) * nd, pipeline_mode=pl.Buffered(1))


def _front(x, conv0, shift0, s, tb, wts):
    ns, tt, d = x.shape
    cw_dim = conv0.shape[-1]
    rw_dim = wts["w0"].shape[-1]
    rcols = shift0.shape[-1]
    nblk = tt // tb
    row_spec = lambda w: pl.BlockSpec((None, tb, w), lambda b, i: (b, i, 0))
    st_spec = lambda r, w: pl.BlockSpec((None, r, w), lambda b, i: (b, 0, 0))
    names = ["norm1_g", "w_in", "conv_w", "mu", "w0", "w2", "a0", "a2", "g2", "k_k", "k_a", "bd"]
    w_in = [wts[n] for n in names]
    out_shapes = ([jax.ShapeDtypeStruct((ns, tt, cw_dim), BF16)]
                  + [jax.ShapeDtypeStruct((ns, tt, rw_dim), F32)] * 7
                  + [jax.ShapeDtypeStruct((ns, 2 * s, cw_dim), F32),
                     jax.ShapeDtypeStruct((ns, s, rcols), F32)])
    out_specs = ([row_spec(cw_dim)] + [row_spec(rw_dim)] * 7
                 + [st_spec(2 * s, cw_dim), st_spec(s, rcols)])
    return pl.pallas_call(
        functools.partial(_front_kernel, s, cw_dim, rw_dim),
        grid=(ns, nblk),
        in_specs=[row_spec(d), st_spec(2 * s, cw_dim), st_spec(s, rcols)]
        + [_const_spec(w.shape) for w in w_in],
        out_specs=out_specs,
        out_shape=out_shapes,
        scratch_shapes=[pltpu.VMEM((2 * s, cw_dim), F32), pltpu.VMEM((s, rcols), F32)],
        compiler_params=pltpu.CompilerParams(
            dimension_semantics=("arbitrary", "arbitrary"), vmem_limit_bytes=VMEM_LIMIT),
        name="front",
    )(x, conv0, shift0, *w_in)


def _back_kernel(x_ref, yconv_ref, y_ref, r_ref, k_ref, v_ref, g_ref,
                 rk_ref, lg_ref, lb_ref, wo_ref, n2_ref, f1_ref, f2_ref, nf_ref, bd_ref,
                 out_ref):
    bd = bd_ref[...]
    y = y_ref[...]
    inv_n = 1.0 / HEAD
    mean = _seg_sum(y, bd) * inv_n
    dlt = y - mean
    var = _seg_sum(dlt * dlt, bd) * inv_n
    yn = dlt * lax.rsqrt(var + GN_EPS) * lg_ref[...] + lb_ref[...]
    v = v_ref[...]
    bonus = _seg_sum(r_ref[...] * k_ref[...] * rk_ref[...], bd) * v
    y_rwkv = ((yn + bonus) * g_ref[...]).astype(BF16)
    cw_dim = yconv_ref.shape[-1]
    x = x_ref[...]
    x1 = x + _dot(yconv_ref[...], wo_ref[:cw_dim, :]) + _dot(y_rwkv, wo_ref[cw_dim:, :])
    h2 = _rmsnorm(x1, n2_ref[...]).astype(BF16)
    f = jnp.maximum(_dot(h2, f1_ref[...]), 0.0)
    x2 = x1 + _dot((f * f).astype(BF16), f2_ref[...])
    out_ref[...] = _rmsnorm(x2, nf_ref[...])


def _back(x, yconv, y, r, k, v, g, tb, wts):
    ns, tt, d = x.shape
    nblk = tt // tb
    row_spec = lambda w: pl.BlockSpec((None, tb, w), lambda b, i: (b, i, 0))
    names = ["r_k", "lnx_g", "lnx_b", "w_out", "norm2_g", "w_ff1", "w_ff2", "normf_g", "bd"]
    w_in = [wts[n] for n in names]
    acts = [x, yconv, y, r, k, v, g]
    return pl.pallas_call(
        _back_kernel,
        grid=(ns, nblk),
        in_specs=[row_spec(a.shape[-1]) for a in acts] + [_const_spec(w.shape) for w in w_in],
        out_specs=row_spec(d),
        out_shape=jax.ShapeDtypeStruct((ns, tt, d), F32),
        compiler_params=pltpu.CompilerParams(
            dimension_semantics=("arbitrary", "arbitrary"), vmem_limit_bytes=VMEM_LIMIT),
        name="back",
    )(*acts, *w_in)


def _mm(a, b):
    return jnp.dot(a.astype(BF16), b.astype(BF16), preferred_element_type=F32)


def _mm_nt(a, b):
    return lax.dot_general(a.astype(BF16), b.astype(BF16), (((1,), (1,)), ((), ())),
                           preferred_element_type=F32)


def _mm_tn(a, b):
    return lax.dot_general(a.astype(BF16), b.astype(BF16), (((0,), (0,)), ((), ())),
                           preferred_element_type=F32)


def _cumsum_rows(x):
    rows = x.shape[0]
    row = lax.broadcasted_iota(jnp.int32, x.shape, 0)
    sh = 1
    while sh < rows:
        x = x + jnp.where(row >= sh, pltpu.roll(x, sh, 0), 0.0)
        sh *= 2
    return x


def _scan_masks(c):
    lanes = GROUP * HEAD
    gc = GROUP * c
    i0 = lambda shape: lax.broadcasted_iota(jnp.int32, shape, 0)
    i1 = lambda shape: lax.broadcasted_iota(jnp.int32, shape, 1)
    col_t = i1((c, gc)) % c
    return dict(
        lane_head=(i0((gc, lanes)) // c) == (i1((gc, lanes)) // HEAD),
        strict=col_t < i0((c, gc)),
        incl=col_t <= i0((c, gc)),
        bd_cc=(i0((gc, gc)) // c) == (i1((gc, gc)) // c),
        bd_ll=(i0((lanes, lanes)) // HEAD) == (i1((lanes, lanes)) // HEAD),
    )


def _chunk_precompute(lw, kk, a, r, k, v, masks):
    c = lw.shape[0]

    def bd_rows(m):
        return jnp.where(masks["lane_head"], jnp.concatenate([m] * GROUP, axis=0), 0.0)

    def bd_mat(m):
        return jnp.where(masks["bd_cc"], jnp.concatenate([m] * GROUP, axis=0), 0.0)

    cum = _cumsum_rows(lw)
    cum_last = cum[c - 1:c, :]
    beta = kk * a
    alpha_t = -kk * jnp.exp(cum - lw)
    r_t = r * jnp.exp(cum)
    inv = jnp.exp(-cum)
    rem = jnp.exp(cum_last - cum)
    ar = jnp.concatenate([alpha_t, r_t], axis=0)
    g_b = _mm_nt(ar, bd_rows(beta * inv))
    g_k = _mm_nt(ar, bd_rows(k * inv))
    yield
    a_ab = jnp.where(masks["strict"], g_b[:c], 0.0)
    a_rb = jnp.where(masks["incl"], g_b[c:], 0.0)
    a_ak = jnp.where(masks["strict"], g_k[:c], 0.0)
    a_rk = jnp.where(masks["incl"], g_k[c:], 0.0)

    tm = a_ab
    n_sq = max(c.bit_length() - 2, 0)
    if n_sq:
        xp = _mm(a_ab, bd_mat(a_ab))
    res = _mm(jnp.concatenate([a_ak, a_rk], axis=0), bd_rows(v))
    yield
    akv, rkv = res[:c], res[c:]
    for i in range(n_sq):
        rhs = bd_mat(xp)
        if i < n_sq - 1:
            res = _mm(jnp.concatenate([tm, xp], axis=0), rhs)
            yield
            tm = tm + xp + res[:c]
            xp = res[c:]
        else:
            res = _mm(tm, rhs)
            yield
            tm = tm + xp + res

    wa = alpha_t + _mm(tm, bd_rows(alpha_t))
    uv = akv + _mm(tm, bd_rows(akv))
    yield
    rq = r_t + _mm(a_rb, bd_rows(wa))
    yv = rkv + _mm(a_rb, bd_rows(uv))
    b_h = beta * rem
    k_h = k * rem
    m = jnp.where(masks["bd_ll"], _mm_tn(b_h, wa), 0.0)
    n = jnp.where(masks["bd_ll"], _mm_tn(jnp.concatenate([b_h, k_h], axis=0),
                                          jnp.concatenate([uv, v], axis=0)), 0.0)
    p_col = jnp.transpose(jnp.broadcast_to(jnp.exp(cum_last), (8, cum.shape[1])))[:, 0:1]
    yield
    return rq, yv, m, n, p_col


def _run_interleaved(gens):
    results = [None] * len(gens)
    live = list(range(len(gens)))
    while live:
        for i in list(live):
            try:
                next(gens[i])
            except StopIteration as stop:
                results[i] = stop.value
                live.remove(i)
    return results


def _pscan_kernel(kk_ref, a_ref, lw_ref, k_ref, r_ref, v_ref, y_ref, zout_ref, z_ref):
    @pl.when(pl.program_id(1) == 0)
    def _():
        z_ref[...] = jnp.zeros_like(z_ref)

    lanes = GROUP * HEAD
    n_groups = kk_ref.shape[1] // lanes
    n_chunks = kk_ref.shape[0] // CHUNK
    masks = _scan_masks(CHUNK)
    keys, gens = [], []
    for ci in range(n_chunks):
        rows = slice(ci * CHUNK, (ci + 1) * CHUNK)
        for g in range(n_groups):
            sl = slice(g * lanes, (g + 1) * lanes)
            keys.append((ci, g))
            gens.append(_chunk_precompute(lw_ref[rows, sl], kk_ref[rows, sl], a_ref[rows, sl],
                                          r_ref[rows, sl], k_ref[rows, sl], v_ref[rows, sl], masks))
    pre = dict(zip(keys, _run_interleaved(gens)))
    zs = [z_ref[g] for g in range(n_groups)]
    for ci in range(n_chunks):
        res = [_mm(jnp.concatenate([pre[ci, g][0], pre[ci, g][2]], axis=0), zs[g])
               for g in range(n_groups)]
        for g in range(n_groups):
            _, yv, _, n, p_col = pre[ci, g]
            y_ref[ci * CHUNK:(ci + 1) * CHUNK, g * lanes:(g + 1) * lanes] = res[g][:CHUNK] + yv
            zs[g] = p_col * zs[g] + res[g][CHUNK:] + n
    for g in range(n_groups):
        z_ref[g] = zs[g]

    @pl.when(pl.program_id(1) == pl.num_programs(1) - 1)
    def _():
        zout_ref[...] = z_ref[...]


def _prompt_scan(kk, a, lw, k, r, v, chunks_per_step):
    nb, t, w = kk.shape
    lanes = GROUP * HEAD
    ng = w // lanes
    tb = CHUNK * chunks_per_step
    spec = pl.BlockSpec((None, tb, w), lambda b, i: (b, i, 0))
    return pl.pallas_call(
        _pscan_kernel,
        grid=(nb, t // tb),
        in_specs=[spec] * 6,
        out_specs=[spec, pl.BlockSpec((None, ng, lanes, lanes), lambda b, i: (b, 0, 0, 0))],
        out_shape=[jax.ShapeDtypeStruct((nb, t, w), F32),
                   jax.ShapeDtypeStruct((nb, ng, lanes, lanes), F32)],
        scratch_shapes=[pltpu.VMEM((ng, lanes, lanes), F32)],
        compiler_params=pltpu.CompilerParams(
            dimension_semantics=("arbitrary", "arbitrary"), vmem_limit_bytes=VMEM_LIMIT),
        name="prompt_scan",
    )(kk, a, lw, k, r, v)


def _sscan_kernel(kk_ref, a_ref, lw_ref, k_ref, r_ref, v_ref, s_in_ref, y_ref, s_out_ref):
    n_t = kk_ref.shape[0]
    n_k = s_in_ref.shape[0]
    shape = s_in_ref.shape[1:]
    for t in range(n_t):
        src = s_in_ref if t == 0 else s_out_ref

        def p1(k8, acc, src=src, t=t):
            base = pl.multiple_of(k8 * 8, 8)
            kk8 = kk_ref[t, pl.ds(base, 8), :]
            for j in range(8):
                acc = acc + src[base + j] * kk8[j:j + 1, :]
            return acc

        sk = lax.fori_loop(0, n_k // 8, p1, jnp.zeros(shape, F32))
        vt = v_ref[t]

        def p2(k8, acc, src=src, t=t, sk=sk, vt=vt):
            base = pl.multiple_of(k8 * 8, 8)
            kk8 = kk_ref[t, pl.ds(base, 8), :]
            kka8 = kk8 * a_ref[t, pl.ds(base, 8), :]
            w8 = jnp.exp(lw_ref[t, pl.ds(base, 8), :])
            k8v = k_ref[t, pl.ds(base, 8), :]
            r8 = r_ref[t, pl.ds(base, 8), :]
            for j in range(8):
                snew = (src[base + j] * w8[j:j + 1, :] - sk * kka8[j:j + 1, :]
                        + vt * k8v[j:j + 1, :])
                s_out_ref[base + j] = snew
                acc = acc + snew * r8[j:j + 1, :]
            return acc

        y_ref[t] = lax.fori_loop(0, n_k // 8, p2, jnp.zeros(shape, F32))


def _sample_scan(kk, a, lw, k, r, v, state):
    n_t, w, nb = kk.shape
    nh = state.shape[0]
    spec = pl.BlockSpec((n_t, HEAD, nb), lambda h: (0, h, 0))
    sspec = pl.BlockSpec((None, HEAD, HEAD, nb), lambda h: (h, 0, 0, 0))
    return pl.pallas_call(
        _sscan_kernel,
        grid=(nh,),
        in_specs=[spec] * 6 + [sspec],
        out_specs=[spec, sspec],
        out_shape=[jax.ShapeDtypeStruct((n_t, w, nb), F32),
                   jax.ShapeDtypeStruct(state.shape, F32)],
        compiler_params=pltpu.CompilerParams(
            dimension_semantics=("arbitrary",), vmem_limit_bytes=VMEM_LIMIT),
        name="sample_scan",
    )(kk, a, lw, k, r, v, state)


def kernel(x_prompt, x_sample, state_conv, state_shift, state_wkv, norm1_g, w_in, conv_w, mu, w0,
           w2, a0, a2, g2, k_k, k_a, r_k, lnx_g, lnx_b, w_out, norm2_g, w_ff1, w_ff2, normf_g):
    depth = w_in.shape[0]
    assert depth == 1, "single-layer step only"
    bp, seq, d = x_prompt.shape
    bs, dseq, _ = x_sample.shape
    cw_dim = conv_w.shape[-1]
    rw_dim = w0.shape[-1]
    rcols = mu.shape[-1]
    nh = rw_dim // HEAD
    row = lambda t: t[0].reshape(1, -1)
    seg = jnp.arange(rw_dim) // HEAD
    wts = {
        "norm1_g": row(norm1_g), "w_in": w_in[0].astype(BF16), "conv_w": conv_w[0],
        "mu": row(mu), "w0": row(w0), "w2": w2[0].astype(BF16), "a0": row(a0),
        "a2": a2[0].astype(BF16), "g2": g2[0].astype(BF16), "k_k": row(k_k), "k_a": row(k_a),
        "r_k": row(r_k), "lnx_g": row(lnx_g), "lnx_b": row(lnx_b),
        "w_out": w_out[0].astype(BF16), "norm2_g": row(norm2_g),
        "w_ff1": w_ff1[0].astype(BF16), "w_ff2": w_ff2[0].astype(BF16),
        "normf_g": normf_g.reshape(1, -1),
        "bd": (seg[:, None] == seg[None, :]).astype(BF16),
    }

    tb = 256
    zc = jnp.zeros((bp, 2, cw_dim), F32)
    zs = jnp.zeros((bp, 1, rcols), F32)
    yconv, r, k, v, kk, a, lw, g, nconv_p, nshift_p = _front(x_prompt, zc, zs, 1, tb, wts)
    y, z = _prompt_scan(kk, a, lw, k, r, v, 4)
    y_prompt = _back(x_prompt, yconv, y, r, k, v, g, tb, wts)
    ng = rw_dim // (GROUP * HEAD)
    zb = z.reshape(bp, ng, GROUP, HEAD, GROUP, HEAD)
    idx = jnp.arange(GROUP)
    zh = zb[:, :, idx, :, idx, :]
    wkv_p = jnp.transpose(zh, (1, 2, 0, 4, 3)).reshape(bp, nh, HEAD, HEAD)

    xs = jnp.transpose(x_sample, (1, 0, 2)).reshape(1, dseq * bs, d)
    c0 = jnp.transpose(state_conv[0], (1, 0, 2)).reshape(1, 2 * bs, cw_dim)
    s0 = state_shift[0].reshape(1, bs, rcols)
    tbs = 2 * bs
    yconv, r, k, v, kk, a, lw, g, nconv_s, nshift_s = _front(xs, c0, s0, bs, tbs, wts)
    to_fm = lambda t: jnp.transpose(t.reshape(dseq, bs, rw_dim), (0, 2, 1))
    st = jnp.transpose(state_wkv[0], (1, 3, 2, 0))
    y_fm, st_new = _sample_scan(to_fm(kk), to_fm(a), to_fm(lw), to_fm(k), to_fm(r), to_fm(v), st)
    y = jnp.transpose(y_fm, (0, 2, 1)).reshape(1, dseq * bs, rw_dim)
    ys = _back(xs, yconv, y, r, k, v, g, tbs, wts)
    y_sample = jnp.transpose(ys.reshape(dseq, bs, d), (1, 0, 2))
    wkv_s = jnp.transpose(st_new, (3, 0, 2, 1))
    nconv_s = jnp.transpose(nconv_s.reshape(2, bs, cw_dim), (1, 0, 2))

    return (y_prompt, y_sample,
            nconv_p[None], nshift_p.reshape(1, bp, rcols), wkv_p[None],
            nconv_s[None], nshift_s.reshape(1, bs, rcols), wkv_s[None])
```

```python
import functools

import jax
import jax.numpy as jnp
from jax import lax
from jax.experimental import pallas as pl
from jax.experimental.pallas import tpu as pltpu

F32 = jnp.float32
BF16 = jnp.bfloat16

HEAD = 64
RMS_EPS = 1e-6
GN_EPS = 64e-5
NORM_EPS = 1e-12
VMEM_LIMIT = 56 * 1024 * 1024
CHUNK = 64
GROUP = 4


def _dot(a, b):
    return jnp.dot(a, b, preferred_element_type=F32)


def _rmsnorm(x, g):
    return x * lax.rsqrt(jnp.mean(x * x, axis=-1, keepdims=True) + RMS_EPS) * g


def _seg_sum(x, bd):
    lanes = bd.shape[0]
    xb = x.astype(BF16)
    return jnp.concatenate([_dot(xb[:, i:i + lanes], bd) for i in range(0, x.shape[1], lanes)],
                           axis=1)


def _shift_rows(x, k, head):
    rows = x.shape[0]
    if k == rows:
        return head
    if k % 8 == 0:
        return jnp.concatenate([head, x[:rows - k]], axis=0)
    rolled = pltpu.roll(x, k, 0)
    row = lax.broadcasted_iota(jnp.int32, x.shape, 0)
    out = rolled
    for j in range(k):
        out = jnp.where(row == j, head[j:j + 1, :], out)
    return out


def _front_kernel(s, cw_dim, rw_dim,
                  x_ref, conv0_ref, shift0_ref, n1_ref, win_ref, cw_ref, mu_ref, w0_ref, w2_ref,
                  a0_ref, a2_ref, g2_ref, kk_w_ref, ka_w_ref, bd_ref,
                  yconv_ref, r_ref, k_ref, v_ref, kk_ref, a_ref, lw_ref, g_ref,
                  nconv_ref, nshift_ref,
                  cu_ref, cp_ref):
    @pl.when(pl.program_id(1) == 0)
    def _():
        cu_ref[...] = conv0_ref[...]
        cp_ref[...] = shift0_ref[...]

    x = x_ref[...]
    rows = x.shape[0]
    h = _rmsnorm(x, n1_ref[...]).astype(BF16)
    proj = _dot(h, win_ref[...])
    c_b = proj[:, :cw_dim]
    c_c = proj[:, cw_dim:2 * cw_dim]
    c_x = proj[:, 2 * cw_dim:3 * cw_dim]
    p = proj[:, 3 * cw_dim:]

    u = c_c * c_x
    cu = cu_ref[...]
    u1 = _shift_rows(u, s, cu[s:2 * s])
    u2 = _shift_rows(u, 2 * s, cu)
    cw = cw_ref[...]
    conv = cw[0:1] * u2 + cw[1:2] * u1 + cw[2:3] * u
    yconv_ref[...] = (c_b * conv).astype(yconv_ref.dtype)
    new_cu = u[rows - 2 * s:]
    cu_ref[...] = new_cu
    nconv_ref[...] = new_cu

    prev = _shift_rows(p, s, cp_ref[...])
    new_cp = p[rows - s:]
    cp_ref[...] = new_cp
    nshift_ref[...] = new_cp
    xm = p + (prev - p) * mu_ref[...]
    r = xm[:, :rw_dim]
    k = xm[:, rw_dim:2 * rw_dim]
    v = xm[:, 2 * rw_dim:3 * rw_dim]
    o = 3 * rw_dim
    d_w = w2_ref.shape[0]
    d_a = a2_ref.shape[0]
    pw = xm[:, o:o + d_w]
    pa = xm[:, o + d_w:o + d_w + d_a]
    pg = xm[:, o + d_w + d_a:]

    wl = w0_ref[...] + _dot(jnp.tanh(pw).astype(BF16), w2_ref[...])
    z = -wl
    softplus = jnp.maximum(z, 0.0) + jnp.log(1.0 + jnp.exp(-jnp.abs(z)))
    lw_ref[...] = -jnp.exp(-softplus - 0.5)
    a = 1.0 / (1.0 + jnp.exp(-(a0_ref[...] + _dot(pa.astype(BF16), a2_ref[...]))))
    g_ref[...] = _dot((1.0 / (1.0 + jnp.exp(-pg))).astype(BF16), g2_ref[...])

    kk = k * kk_w_ref[...]
    nrm = jnp.sqrt(_seg_sum(kk * kk, bd_ref[...]))
    kk_ref[...] = kk / jnp.maximum(nrm, NORM_EPS)
    k_ref[...] = k * (1.0 + (a - 1.0) * ka_w_ref[...])
    r_ref[...] = r
    v_ref[...] = v
    a_ref[...] = a


def _const_spec(shape):
    nd = len(shape)
    return pl.BlockSpec(shape, lambda *_: (0,) * nd, pipeline_mode=pl.Buffered(1))


def _front(x, conv0, shift0, s, tb, wts):
    ns, tt, d = x.shape
    cw_dim = conv0.shape[-1]
    rw_dim = wts["w0"].shape[-1]
    rcols = shift0.shape[-1]
    nblk = tt // tb
    row_spec = lambda w: pl.BlockSpec((None, tb, w), lambda b, i: (b, i, 0))
    st_spec = lambda r, w: pl.BlockSpec((None, r, w), lambda b, i: (b, 0, 0))
    names = ["norm1_g", "w_in", "conv_w", "mu", "w0", "w2", "a0", "a2", "g2", "k_k", "k_a", "bd"]
    w_in = [wts[n] for n in names]
    out_shapes = ([jax.ShapeDtypeStruct((ns, tt, cw_dim), BF16)]
                  + [jax.ShapeDtypeStruct((ns, tt, rw_dim), F32)] * 7
                  + [jax.ShapeDtypeStruct((ns, 2 * s, cw_dim), F32),
                     jax.ShapeDtypeStruct((ns, s, rcols), F32)])
    out_specs = ([row_spec(cw_dim)] + [row_spec(rw_dim)] * 7
                 + [st_spec(2 * s, cw_dim), st_spec(s, rcols)])
    return pl.pallas_call(
        functools.partial(_front_kernel, s, cw_dim, rw_dim),
        grid=(ns, nblk),
        in_specs=[row_spec(d), st_spec(2 * s, cw_dim), st_spec(s, rcols)]
        + [_const_spec(w.shape) for w in w_in],
        out_specs=out_specs,
        out_shape=out_shapes,
        scratch_shapes=[pltpu.VMEM((2 * s, cw_dim), F32), pltpu.VMEM((s, rcols), F32)],
        compiler_params=pltpu.CompilerParams(
            dimension_semantics=("arbitrary", "arbitrary"), vmem_limit_bytes=VMEM_LIMIT),
        name="front",
    )(x, conv0, shift0, *w_in)


def _back_kernel(x_ref, yconv_ref, y_ref, r_ref, k_ref, v_ref, g_ref,
                 rk_ref, lg_ref, lb_ref, wo_ref, n2_ref, f1_ref, f2_ref, nf_ref, bd_ref,
                 out_ref):
    bd = bd_ref[...]
    y = y_ref[...]
    inv_n = 1.0 / HEAD
    mean = _seg_sum(y, bd) * inv_n
    dlt = y - mean
    var = _seg_sum(dlt * dlt, bd) * inv_n
    yn = dlt * lax.rsqrt(var + GN_EPS) * lg_ref[...] + lb_ref[...]
    v = v_ref[...]
    bonus = _seg_sum(r_ref[...] * k_ref[...] * rk_ref[...], bd) * v
    y_rwkv = ((yn + bonus) * g_ref[...]).astype(BF16)
    cw_dim = yconv_ref.shape[-1]
    x = x_ref[...]
    x1 = x + _dot(yconv_ref[...], wo_ref[:cw_dim, :]) + _dot(y_rwkv, wo_ref[cw_dim:, :])
    h2 = _rmsnorm(x1, n2_ref[...]).astype(BF16)
    f = jnp.maximum(_dot(h2, f1_ref[...]), 0.0)
    x2 = x1 + _dot((f * f).astype(BF16), f2_ref[...])
    out_ref[...] = _rmsnorm(x2, nf_ref[...])


def _back(x, yconv, y, r, k, v, g, tb, wts):
    ns, tt, d = x.shape
    nblk = tt // tb
    row_spec = lambda w: pl.BlockSpec((None, tb, w), lambda b, i: (b, i, 0))
    names = ["r_k", "lnx_g", "lnx_b", "w_out", "norm2_g", "w_ff1", "w_ff2", "normf_g", "bd"]
    w_in = [wts[n] for n in names]
    acts = [x, yconv, y, r, k, v, g]
    return pl.pallas_call(
        _back_kernel,
        grid=(ns, nblk),
        in_specs=[row_spec(a.shape[-1]) for a in acts] + [_const_spec(w.shape) for w in w_in],
        out_specs=row_spec(d),
        out_shape=jax.ShapeDtypeStruct((ns, tt, d), F32),
        compiler_params=pltpu.CompilerParams(
            dimension_semantics=("arbitrary", "arbitrary"), vmem_limit_bytes=VMEM_LIMIT),
        name="back",
    )(*acts, *w_in)


def _mm(a, b):
    return jnp.dot(a.astype(BF16), b.astype(BF16), preferred_element_type=F32)


def _mm_nt(a, b):
    return lax.dot_general(a.astype(BF16), b.astype(BF16), (((1,), (1,)), ((), ())),
                           preferred_element_type=F32)


def _mm_tn(a, b):
    return lax.dot_general(a.astype(BF16), b.astype(BF16), (((0,), (0,)), ((), ())),
                           preferred_element_type=F32)


def _cumsum_rows(x):
    rows = x.shape[0]
    row = lax.broadcasted_iota(jnp.int32, x.shape, 0)
    sh = 1
    while sh < rows:
        x = x + jnp.where(row >= sh, pltpu.roll(x, sh, 0), 0.0)
        sh *= 2
    return x


def _scan_masks(c):
    lanes = GROUP * HEAD
    gc = GROUP * c
    i0 = lambda shape: lax.broadcasted_iota(jnp.int32, shape, 0)
    i1 = lambda shape: lax.broadcasted_iota(jnp.int32, shape, 1)
    col_t = i1((c, gc)) % c
    return dict(
        lane_head=(i0((gc, lanes)) // c) == (i1((gc, lanes)) // HEAD),
        strict=col_t < i0((c, gc)),
        incl=col_t <= i0((c, gc)),
        bd_cc=(i0((gc, gc)) // c) == (i1((gc, gc)) // c),
        bd_ll=(i0((lanes, lanes)) // HEAD) == (i1((lanes, lanes)) // HEAD),
    )


def _chunk_precompute(lw, kk, a, r, k, v, masks):
    c = lw.shape[0]

    def bd_rows(m):
        return jnp.where(masks["lane_head"], jnp.concatenate([m] * GROUP, axis=0), 0.0)

    def bd_mat(m):
        return jnp.where(masks["bd_cc"], jnp.concatenate([m] * GROUP, axis=0), 0.0)

    cum = _cumsum_rows(lw)
    cum_last = cum[c - 1:c, :]
    beta = kk * a
    alpha_t = -kk * jnp.exp(cum - lw)
    r_t = r * jnp.exp(cum)
    inv = jnp.exp(-cum)
    rem = jnp.exp(cum_last - cum)
    ar = jnp.concatenate([alpha_t, r_t], axis=0)
    g_b = _mm_nt(ar, bd_rows(beta * inv))
    g_k = _mm_nt(ar, bd_rows(k * inv))
    yield
    a_ab = jnp.where(masks["strict"], g_b[:c], 0.0)
    a_rb = jnp.where(masks["incl"], g_b[c:], 0.0)
    a_ak = jnp.where(masks["strict"], g_k[:c], 0.0)
    a_rk = jnp.where(masks["incl"], g_k[c:], 0.0)

    tm = a_ab
    n_sq = max(c.bit_length() - 2, 0)
    if n_sq:
        xp = _mm(a_ab, bd_mat(a_ab))
    res = _mm(jnp.concatenate([a_ak, a_rk], axis=0), bd_rows(v))
    yield
    akv, rkv = res[:c], res[c:]
    for i in range(n_sq):
        rhs = bd_mat(xp)
        if i < n_sq - 1:
            res = _mm(jnp.concatenate([tm, xp], axis=0), rhs)
            yield
            tm = tm + xp + res[:c]
            xp = res[c:]
        else:
            res = _mm(tm, rhs)
            yield
            tm = tm + xp + res

    wa = alpha_t + _mm(tm, bd_rows(alpha_t))
    uv = akv + _mm(tm, bd_rows(akv))
    yield
    rq = r_t + _mm(a_rb, bd_rows(wa))
    yv = rkv + _mm(a_rb, bd_rows(uv))
    b_h = beta * rem
    k_h = k * rem
    m = jnp.where(masks["bd_ll"], _mm_tn(b_h, wa), 0.0)
    n = jnp.where(masks["bd_ll"], _mm_tn(jnp.concatenate([b_h, k_h], axis=0),
                                          jnp.concatenate([uv, v], axis=0)), 0.0)
    p_col = jnp.transpose(jnp.broadcast_to(jnp.exp(cum_last), (8, cum.shape[1])))[:, 0:1]
    yield
    return rq, yv, m, n, p_col


def _run_interleaved(gens):
    results = [None] * len(gens)
    live = list(range(len(gens)))
    while live:
        for i in list(live):
            try:
                next(gens[i])
            except StopIteration as stop:
                results[i] = stop.value
                live.remove(i)
    return results


def _pscan_kernel(kk_ref, a_ref, lw_ref, k_ref, r_ref, v_ref, y_ref, zout_ref, z_ref):
    @pl.when(pl.program_id(1) == 0)
    def _():
        z_ref[...] = jnp.zeros_like(z_ref)

    lanes = GROUP * HEAD
    n_groups = kk_ref.shape[1] // lanes
    n_chunks = kk_ref.shape[0] // CHUNK
    masks = _scan_masks(CHUNK)
    keys, gens = [], []
    for ci in range(n_chunks):
        rows = slice(ci * CHUNK, (ci + 1) * CHUNK)
        for g in range(n_groups):
            sl = slice(g * lanes, (g + 1) * lanes)
            keys.append((ci, g))
            gens.append(_chunk_precompute(lw_ref[rows, sl], kk_ref[rows, sl], a_ref[rows, sl],
                                          r_ref[rows, sl], k_ref[rows, sl], v_ref[rows, sl], masks))
    pre = dict(zip(keys, _run_interleaved(gens)))
    zs = [z_ref[g] for g in range(n_groups)]
    for ci in range(n_chunks):
        res = [_mm(jnp.concatenate([pre[ci, g][0], pre[ci, g][2]], axis=0), zs[g])
               for g in range(n_groups)]
        for g in range(n_groups):
            _, yv, _, n, p_col = pre[ci, g]
            y_ref[ci * CHUNK:(ci + 1) * CHUNK, g * lanes:(g + 1) * lanes] = res[g][:CHUNK] + yv
            zs[g] = p_col * zs[g] + res[g][CHUNK:] + n
    for g in range(n_groups):
        z_ref[g] = zs[g]

    @pl.when(pl.program_id(1) == pl.num_programs(1) - 1)
    def _():
        zout_ref[...] = z_ref[...]


def _prompt_scan(kk, a, lw, k, r, v, chunks_per_step):
    nb, t, w = kk.shape
    lanes = GROUP * HEAD
    ng = w // lanes
    tb = CHUNK * chunks_per_step
    spec = pl.BlockSpec((None, tb, w), lambda b, i: (b, i, 0))
    return pl.pallas_call(
        _pscan_kernel,
        grid=(nb, t // tb),
        in_specs=[spec] * 6,
        out_specs=[spec, pl.BlockSpec((None, ng, lanes, lanes), lambda b, i: (b, 0, 0, 0))],
        out_shape=[jax.ShapeDtypeStruct((nb, t, w), F32),
                   jax.ShapeDtypeStruct((nb, ng, lanes, lanes), F32)],
        scratch_shapes=[pltpu.VMEM((ng, lanes, lanes), F32)],
        compiler_params=pltpu.CompilerParams(
            dimension_semantics=("arbitrary", "arbitrary"), vmem_limit_bytes=VMEM_LIMIT),
        name="prompt_scan",
    )(kk, a, lw, k, r, v)


def _sscan_kernel(kk_ref, a_ref, lw_ref, k_ref, r_ref, v_ref, s_in_ref, y_ref, s_out_ref):
    n_t = kk_ref.shape[0]
    n_k = s_in_ref.shape[0]
    shape = s_in_ref.shape[1:]
    for t in range(n_t):
        src = s_in_ref if t == 0 else s_out_ref

        def p1(k8, acc, src=src, t=t):
            base = pl.multiple_of(k8 * 8, 8)
            kk8 = kk_ref[t, pl.ds(base, 8), :]
            for j in range(8):
                acc = acc + src[base + j] * kk8[j:j + 1, :]
            return acc

        sk = lax.fori_loop(0, n_k // 8, p1, jnp.zeros(shape, F32))
        vt = v_ref[t]

        def p2(k8, acc, src=src, t=t, sk=sk, vt=vt):
            base = pl.multiple_of(k8 * 8, 8)
            kk8 = kk_ref[t, pl.ds(base, 8), :]
            kka8 = kk8 * a_ref[t, pl.ds(base, 8), :]
            w8 = jnp.exp(lw_ref[t, pl.ds(base, 8), :])
            k8v = k_ref[t, pl.ds(base, 8), :]
            r8 = r_ref[t, pl.ds(base, 8), :]
            for j in range(8):
                snew = (src[base + j] * w8[j:j + 1, :] - sk * kka8[j:j + 1, :]
                        + vt * k8v[j:j + 1, :])
                s_out_ref[base + j] = snew
                acc = acc + snew * r8[j:j + 1, :]
            return acc

        y_ref[t] = lax.fori_loop(0, n_k // 8, p2, jnp.zeros(shape, F32))


def _sample_scan(kk, a, lw, k, r, v, state):
    n_t, w, nb = kk.shape
    nh = state.shape[0]
    spec = pl.BlockSpec((n_t, HEAD, nb), lambda h: (0, h, 0))
    sspec = pl.BlockSpec((None, HEAD, HEAD, nb), lambda h: (h, 0, 0, 0))
    return pl.pallas_call(
        _sscan_kernel,
        grid=(nh,),
        in_specs=[spec] * 6 + [sspec],
        out_specs=[spec, sspec],
        out_shape=[jax.ShapeDtypeStruct((n_t, w, nb), F32),
                   jax.ShapeDtypeStruct(state.shape, F32)],
        compiler_params=pltpu.CompilerParams(
            dimension_semantics=("arbitrary",), vmem_limit_bytes=VMEM_LIMIT),
        name="sample_scan",
    )(kk, a, lw, k, r, v, state)


def kernel(x_prompt, x_sample, state_conv, state_shift, state_wkv, norm1_g, w_in, conv_w, mu, w0,
           w2, a0, a2, g2, k_k, k_a, r_k, lnx_g, lnx_b, w_out, norm2_g, w_ff1, w_ff2, normf_g):
    depth = w_in.shape[0]
    assert depth == 1, "single-layer step only"
    bp, seq, d = x_prompt.shape
    bs, dseq, _ = x_sample.shape
    cw_dim = conv_w.shape[-1]
    rw_dim = w0.shape[-1]
    rcols = mu.shape[-1]
    nh = rw_dim // HEAD
    row = lambda t: t[0].reshape(1, -1)
    seg = jnp.arange(GROUP * HEAD) // HEAD
    wts = {
        "norm1_g": row(norm1_g), "w_in": w_in[0].astype(BF16), "conv_w": conv_w[0],
        "mu": row(mu), "w0": row(w0), "w2": w2[0].astype(BF16), "a0": row(a0),
        "a2": a2[0].astype(BF16), "g2": g2[0].astype(BF16), "k_k": row(k_k), "k_a": row(k_a),
        "r_k": row(r_k), "lnx_g": row(lnx_g), "lnx_b": row(lnx_b),
        "w_out": w_out[0].astype(BF16), "norm2_g": row(norm2_g),
        "w_ff1": w_ff1[0].astype(BF16), "w_ff2": w_ff2[0].astype(BF16),
        "normf_g": normf_g.reshape(1, -1),
        "bd": (seg[:, None] == seg[None, :]).astype(BF16),
    }

    tb = 256
    zc = jnp.zeros((bp, 2, cw_dim), F32)
    zs = jnp.zeros((bp, 1, rcols), F32)
    yconv, r, k, v, kk, a, lw, g, nconv_p, nshift_p = _front(x_prompt, zc, zs, 1, tb, wts)
    y, z = _prompt_scan(kk, a, lw, k, r, v, 4)
    y_prompt = _back(x_prompt, yconv, y, r, k, v, g, tb, wts)
    ng = rw_dim // (GROUP * HEAD)
    zb = z.reshape(bp, ng, GROUP, HEAD, GROUP, HEAD)
    idx = jnp.arange(GROUP)
    zh = zb[:, :, idx, :, idx, :]
    wkv_p = jnp.transpose(zh, (1, 2, 0, 4, 3)).reshape(bp, nh, HEAD, HEAD)

    xs = jnp.transpose(x_sample, (1, 0, 2)).reshape(1, dseq * bs, d)
    c0 = jnp.transpose(state_conv[0], (1, 0, 2)).reshape(1, 2 * bs, cw_dim)
    s0 = state_shift[0].reshape(1, bs, rcols)
    tbs = 2 * bs
    yconv, r, k, v, kk, a, lw, g, nconv_s, nshift_s = _front(xs, c0, s0, bs, tbs, wts)
    to_fm = lambda t: jnp.transpose(t.reshape(dseq, bs, rw_dim), (0, 2, 1))
    st = jnp.transpose(state_wkv[0], (1, 3, 2, 0))
    y_fm, st_new = _sample_scan(to_fm(kk), to_fm(a), to_fm(lw), to_fm(k), to_fm(r), to_fm(v), st)
    y = jnp.transpose(y_fm, (0, 2, 1)).reshape(1, dseq * bs, rw_dim)
    ys = _back(xs, yconv, y, r, k, v, g, tbs, wts)
    y_sample = jnp.transpose(ys.reshape(dseq, bs, d), (1, 0, 2))
    wkv_s = jnp.transpose(st_new, (3, 0, 2, 1))
    nconv_s = jnp.transpose(nconv_s.reshape(2, bs, cw_dim), (1, 0, 2))

    return (y_prompt, y_sample,
            nconv_p[None], nshift_p.reshape(1, bp, rcols), wkv_p[None],
            nconv_s[None], nshift_s.reshape(1, bs, rcols), wkv_s[None])
```

```python
import functools

import jax
import jax.numpy as jnp
from jax import lax
from jax.experimental import pallas as pl
from jax.experimental.pallas import tpu as pltpu

F32 = jnp.float32
BF16 = jnp.bfloat16

HEAD = 64
RMS_EPS = 1e-6
GN_EPS = 64e-5
NORM_EPS = 1e-12
VMEM_LIMIT = 56 * 1024 * 1024
CHUNK = 64
GROUP = 4
SCAN_SKEW = 1
SIDE_RATE = 1


def _dot(a, b):
    return jnp.dot(a, b, preferred_element_type=F32)


def _rmsnorm(x, g):
    return x * lax.rsqrt(jnp.mean(x * x, axis=-1, keepdims=True) + RMS_EPS) * g


def _seg_sum(x, bd):
    lanes = bd.shape[0]
    xb = x.astype(BF16)
    return jnp.concatenate([_dot(xb[:, i:i + lanes], bd) for i in range(0, x.shape[1], lanes)],
                           axis=1)


def _shift_rows(x, k, head):
    rows = x.shape[0]
    if k == rows:
        return head
    if k % 8 == 0:
        return jnp.concatenate([head, x[:rows - k]], axis=0)
    rolled = pltpu.roll(x, k, 0)
    row = lax.broadcasted_iota(jnp.int32, x.shape, 0)
    out = rolled
    for j in range(k):
        out = jnp.where(row == j, head[j:j + 1, :], out)
    return out


FRONT_WEIGHTS = ("norm1_g", "w_in", "conv_w", "mu", "w0", "w2", "a0", "a2", "g2", "k_k", "k_a",
                 "bd")
PROJ_PIECE = 512


def _front_stages(x, cu, cp, s, w, store):
    rows = x.shape[0]
    cw_dim = cu.shape[1]
    rw_dim = w["w0"].shape[1]
    win = w["w_in"]
    ncols = win.shape[1]
    h = _rmsnorm(x, w["norm1_g"][...]).astype(BF16)
    pieces = []
    for c0 in range(0, ncols, PROJ_PIECE):
        pieces.append(_dot(h, win[:, c0:min(c0 + PROJ_PIECE, ncols)]))
        yield
    proj = jnp.concatenate(pieces, axis=1)
    c_b = proj[:, :cw_dim]
    c_c = proj[:, cw_dim:2 * cw_dim]
    c_x = proj[:, 2 * cw_dim:3 * cw_dim]
    p = proj[:, 3 * cw_dim:]

    u = c_c * c_x
    u1 = _shift_rows(u, s, cu[s:2 * s])
    u2 = _shift_rows(u, 2 * s, cu)
    cw = w["conv_w"][...]
    conv = cw[0:1] * u2 + cw[1:2] * u1 + cw[2:3] * u
    store("yconv", (c_b * conv).astype(BF16))
    store("new_cu", u[rows - 2 * s:])
    yield

    prev = _shift_rows(p, s, cp)
    store("new_cp", p[rows - s:])
    xm = p + (prev - p) * w["mu"][...]
    r = xm[:, :rw_dim]
    k = xm[:, rw_dim:2 * rw_dim]
    v = xm[:, 2 * rw_dim:3 * rw_dim]
    store("r", r)
    store("v", v)
    o = 3 * rw_dim
    d_w = w["w2"].shape[0]
    d_a = w["a2"].shape[0]
    pw = xm[:, o:o + d_w]
    pa = xm[:, o + d_w:o + d_w + d_a]
    pg = xm[:, o + d_w + d_a:]
    yield

    wl = w["w0"][...] + _dot(jnp.tanh(pw).astype(BF16), w["w2"][...])
    z = -wl
    softplus = jnp.maximum(z, 0.0) + jnp.log(1.0 + jnp.exp(-jnp.abs(z)))
    store("lw", -jnp.exp(-softplus - 0.5))
    yield
    a = 1.0 / (1.0 + jnp.exp(-(w["a0"][...] + _dot(pa.astype(BF16), w["a2"][...]))))
    store("a", a)
    store("g", _dot((1.0 / (1.0 + jnp.exp(-pg))).astype(BF16), w["g2"][...]))
    yield
    kk = k * w["k_k"][...]
    nrm = jnp.sqrt(_seg_sum(kk * kk, w["bd"][...]))
    store("kk", kk / jnp.maximum(nrm, NORM_EPS))
    store("k", k * (1.0 + (a - 1.0) * w["k_a"][...]))
    yield


FRONT_OUTPUTS = ("yconv", "r", "k", "v", "kk", "a", "lw", "g", "new_cu", "new_cp")


def _front_kernel(s, x_ref, conv0_ref, shift0_ref, *refs):
    nw = len(FRONT_WEIGHTS)
    w = dict(zip(FRONT_WEIGHTS, refs[:nw]))
    outs = dict(zip(FRONT_OUTPUTS, refs[nw:nw + len(FRONT_OUTPUTS)]))
    cu_ref, cp_ref = refs[nw + len(FRONT_OUTPUTS):]

    @pl.when(pl.program_id(1) == 0)
    def _():
        cu_ref[...] = conv0_ref[...]
        cp_ref[...] = shift0_ref[...]

    carry = {}

    def store(name, value):
        outs[name][...] = value
        carry[name] = value

    for _ in _front_stages(x_ref[...], cu_ref[...], cp_ref[...], s, w, store):
        pass
    cu_ref[...] = carry["new_cu"]
    cp_ref[...] = carry["new_cp"]


def _const_spec(shape):
    nd = len(shape)
    return pl.BlockSpec(shape, lambda *_: (0,) * nd, pipeline_mode=pl.Buffered(1))


def _front(x, conv0, shift0, s, tb, wts):
    ns, tt, d = x.shape
    cw_dim = conv0.shape[-1]
    rw_dim = wts["w0"].shape[-1]
    rcols = shift0.shape[-1]
    nblk = tt // tb
    row_spec = lambda w: pl.BlockSpec((None, tb, w), lambda b, i: (b, i, 0))
    st_spec = lambda r, w: pl.BlockSpec((None, r, w), lambda b, i: (b, 0, 0))
    w_in = [wts[n] for n in FRONT_WEIGHTS]
    out_shapes = ([jax.ShapeDtypeStruct((ns, tt, cw_dim), BF16)]
                  + [jax.ShapeDtypeStruct((ns, tt, rw_dim), F32)] * 7
                  + [jax.ShapeDtypeStruct((ns, 2 * s, cw_dim), F32),
                     jax.ShapeDtypeStruct((ns, s, rcols), F32)])
    out_specs = ([row_spec(cw_dim)] + [row_spec(rw_dim)] * 7
                 + [st_spec(2 * s, cw_dim), st_spec(s, rcols)])
    return pl.pallas_call(
        functools.partial(_front_kernel, s),
        grid=(ns, nblk),
        in_specs=[row_spec(d), st_spec(2 * s, cw_dim), st_spec(s, rcols)]
        + [_const_spec(w.shape) for w in w_in],
        out_specs=out_specs,
        out_shape=out_shapes,
        scratch_shapes=[pltpu.VMEM((2 * s, cw_dim), F32), pltpu.VMEM((s, rcols), F32)],
        compiler_params=pltpu.CompilerParams(
            dimension_semantics=("arbitrary", "arbitrary"), vmem_limit_bytes=VMEM_LIMIT),
        name="front",
    )(x, conv0, shift0, *w_in)


def _back_kernel(x_ref, yconv_ref, y_ref, r_ref, k_ref, v_ref, g_ref,
                 rk_ref, lg_ref, lb_ref, wo_ref, n2_ref, f1_ref, f2_ref, nf_ref, bd_ref,
                 out_ref):
    bd = bd_ref[...]
    y = y_ref[...]
    inv_n = 1.0 / HEAD
    mean = _seg_sum(y, bd) * inv_n
    dlt = y - mean
    var = _seg_sum(dlt * dlt, bd) * inv_n
    yn = dlt * lax.rsqrt(var + GN_EPS) * lg_ref[...] + lb_ref[...]
    v = v_ref[...]
    bonus = _seg_sum(r_ref[...] * k_ref[...] * rk_ref[...], bd) * v
    y_rwkv = ((yn + bonus) * g_ref[...]).astype(BF16)
    cw_dim = yconv_ref.shape[-1]
    x = x_ref[...]
    x1 = x + _dot(yconv_ref[...], wo_ref[:cw_dim, :]) + _dot(y_rwkv, wo_ref[cw_dim:, :])
    h2 = _rmsnorm(x1, n2_ref[...]).astype(BF16)
    f = jnp.maximum(_dot(h2, f1_ref[...]), 0.0)
    x2 = x1 + _dot((f * f).astype(BF16), f2_ref[...])
    out_ref[...] = _rmsnorm(x2, nf_ref[...])


def _back(x, yconv, y, r, k, v, g, tb, wts):
    ns, tt, d = x.shape
    nblk = tt // tb
    row_spec = lambda w: pl.BlockSpec((None, tb, w), lambda b, i: (b, i, 0))
    names = ["r_k", "lnx_g", "lnx_b", "w_out", "norm2_g", "w_ff1", "w_ff2", "normf_g", "bd"]
    w_in = [wts[n] for n in names]
    acts = [x, yconv, y, r, k, v, g]
    return pl.pallas_call(
        _back_kernel,
        grid=(ns, nblk),
        in_specs=[row_spec(a.shape[-1]) for a in acts] + [_const_spec(w.shape) for w in w_in],
        out_specs=row_spec(d),
        out_shape=jax.ShapeDtypeStruct((ns, tt, d), F32),
        compiler_params=pltpu.CompilerParams(
            dimension_semantics=("arbitrary", "arbitrary"), vmem_limit_bytes=VMEM_LIMIT),
        name="back",
    )(*acts, *w_in)


def _mm(a, b):
    return jnp.dot(a.astype(BF16), b.astype(BF16), preferred_element_type=F32)


def _mm_nt(a, b):
    return lax.dot_general(a.astype(BF16), b.astype(BF16), (((1,), (1,)), ((), ())),
                           preferred_element_type=F32)


def _cumsum_rows(x):
    rows = x.shape[0]
    row = lax.broadcasted_iota(jnp.int32, x.shape, 0)
    sh = 1
    while sh < rows:
        x = x + jnp.where(row >= sh, pltpu.roll(x, sh, 0), 0.0)
        sh *= 2
    return x


def _scan_masks(c):
    lanes = GROUP * HEAD
    gc = GROUP * c
    i0 = lambda shape: lax.broadcasted_iota(jnp.int32, shape, 0)
    i1 = lambda shape: lax.broadcasted_iota(jnp.int32, shape, 1)
    col_t = i1((c, gc)) % c
    return dict(
        lane_head=(i0((gc, lanes)) // c) == (i1((gc, lanes)) // HEAD),
        strict=col_t < i0((c, gc)),
        incl=col_t <= i0((c, gc)),
        bd_cc=(i0((gc, gc)) // c) == (i1((gc, gc)) // c),
        bd_ll=(i0((lanes, lanes)) // HEAD) == (i1((lanes, lanes)) // HEAD),
    )


def _chunk_precompute(load, masks):
    lw, kk, a, r, k, v = load()
    c = lw.shape[0]
    assert c == HEAD, "side-by-side forms share lane offsets only when CHUNK == HEAD"

    def bd_rows(m):
        return jnp.where(masks["lane_head"], jnp.concatenate([m] * GROUP, axis=0), 0.0)

    def bd_mat(m):
        return jnp.where(masks["bd_cc"], jnp.concatenate([m] * GROUP, axis=0), 0.0)

    def heads_t(m):
        mt = jnp.transpose(bd_rows(m))
        return sum(mt[h * HEAD:(h + 1) * HEAD, :] for h in range(GROUP))

    cum = _cumsum_rows(lw)
    cum_last = cum[c - 1:c, :]
    beta = kk * a
    alpha_t = -kk * jnp.exp(cum - lw)
    r_t = r * jnp.exp(cum)
    inv = jnp.exp(-cum)
    rem = jnp.exp(cum_last - cum)
    ar = jnp.concatenate([alpha_t, r_t], axis=0)
    g_b = _mm_nt(ar, bd_rows(beta * inv))
    g_k = _mm_nt(ar, bd_rows(k * inv))
    yield
    a_ab = jnp.where(masks["strict"], g_b[:c], 0.0)
    a_rb = jnp.where(masks["incl"], g_b[c:], 0.0)
    a_ak = jnp.where(masks["strict"], g_k[:c], 0.0)
    a_rk = jnp.where(masks["incl"], g_k[c:], 0.0)

    tm = a_ab
    n_sq = max(c.bit_length() - 2, 0)
    if n_sq:
        xp = _mm(a_ab, bd_mat(a_ab))
    b_t = heads_t(beta * rem)
    k_t = heads_t(k * rem)
    res = _mm(jnp.concatenate([a_ak, a_rk, k_t], axis=0), bd_rows(v))
    yield
    akv, rkv, nh_k = res[:c], res[c:2 * c], res[2 * c:]
    for i in range(n_sq):
        rhs = bd_mat(xp)
        if i < n_sq - 1:
            res = _mm(jnp.concatenate([tm, xp], axis=0), rhs)
            yield
            tm = tm + xp + res[:c]
            xp = res[c:]
        else:
            res = _mm(tm, rhs)
            yield
            tm = tm + xp + res

    wa = alpha_t + _mm(tm, bd_rows(alpha_t))
    uv = akv + _mm(tm, bd_rows(akv))
    yield
    lhs = jnp.concatenate([a_rb, b_t], axis=0)
    res_w = _mm(lhs, bd_rows(wa))
    res_u = _mm(lhs, bd_rows(uv))
    ph = heads_t(jnp.broadcast_to(jnp.exp(cum_last), cum.shape))
    yield
    rq = r_t + res_w[:c]
    yv = rkv + res_u[:c]
    mh = res_w[c:]
    nh = res_u[c:] + nh_k
    return rq, yv, mh, nh, ph


def _run_staggered(gens_by_chunk, skew, on_chunk_done, side=()):
    n = len(gens_by_chunk)
    results = [[None] * len(gens) for gens in gens_by_chunk]
    live = {(ci, j) for ci, gens in enumerate(gens_by_chunk) for j in range(len(gens))}
    side = list(side)
    n_done = 0
    rnd = 0
    while live or side:
        for gen in list(side) * SIDE_RATE:
            if gen in side and next(gen, side) is side:
                side.remove(gen)
        for ci in range(min(n, rnd // skew + 1)):
            for j, gen in enumerate(gens_by_chunk[ci]):
                if (ci, j) in live:
                    try:
                        next(gen)
                    except StopIteration as stop:
                        results[ci][j] = stop.value
                        live.discard((ci, j))
        while n_done < n and not any(ci == n_done for ci, _ in live):
            on_chunk_done(n_done, results[n_done])
            n_done += 1
        rnd += 1


SCAN_OPERANDS = ("lw", "kk", "a", "r", "k", "v")


def _front_scan_kernel(blocks_per_seq, x_ref, *refs):
    nw = len(FRONT_WEIGHTS)
    w = dict(zip(FRONT_WEIGHTS, refs[:nw]))
    (yconv_ref, r_ref, k_ref, v_ref, g_ref, y_ref, nconv_ref, nshift_ref, zout_ref,
     stage_ref, z_ref, cu_ref, cp_ref) = refs[nw:]
    j = pl.program_id(0)
    last_front = pl.num_programs(0) - 2

    @pl.when(j == 0)
    def _():
        stage_ref[...] = jnp.zeros_like(stage_ref)
        z_ref[...] = jnp.zeros_like(z_ref)
        cu_ref[...] = jnp.zeros_like(cu_ref)
        cp_ref[...] = jnp.zeros_like(cp_ref)

    wpar = j % 2
    rpar = 1 - wpar
    keep_f = jnp.where(jnp.minimum(j, last_front) % blocks_per_seq == 0, 0.0, 1.0)
    keep_s = jnp.where((j + blocks_per_seq - 1) % blocks_per_seq == 0, 0.0, 1.0)

    outs = dict(yconv=yconv_ref, r=r_ref, k=k_ref, v=v_ref, g=g_ref)
    slot = {name: i for i, name in enumerate(SCAN_OPERANDS)}
    carry = {}

    def store(name, value):
        if name in outs:
            outs[name][...] = value
        if name in slot:
            stage_ref[wpar, slot[name]] = value
        carry[name] = value

    cu_old = cu_ref[...]
    cp_old = cp_ref[...]
    front = _front_stages(x_ref[...], cu_old * keep_f, cp_old * keep_f, 1, w, store)

    lanes = GROUP * HEAD
    n_groups = stage_ref.shape[3] // lanes
    n_chunks = stage_ref.shape[2] // CHUNK
    masks = _scan_masks(CHUNK)

    def loader(ci, g):
        rows = slice(ci * CHUNK, (ci + 1) * CHUNK)
        sl = slice(g * lanes, (g + 1) * lanes)
        return lambda: tuple(stage_ref[rpar, i, rows, sl] for i in range(len(SCAN_OPERANDS)))

    gens = [[_chunk_precompute(loader(ci, g), masks) for g in range(n_groups)]
            for ci in range(n_chunks)]
    zs = [z_ref[g] * keep_s for g in range(n_groups)]

    def apply_chunk(ci, pre):
        res = [_mm(jnp.concatenate([pre[g][0], pre[g][2]], axis=0),
                   jnp.where(masks["bd_ll"], jnp.concatenate([zs[g]] * GROUP, axis=0), 0.0))
               for g in range(n_groups)]
        for g in range(n_groups):
            _, yv, _, nh, ph = pre[g]
            y_ref[ci * CHUNK:(ci + 1) * CHUNK, g * lanes:(g + 1) * lanes] = res[g][:CHUNK] + yv
            zs[g] = ph * zs[g] + res[g][CHUNK:] + nh

    _run_staggered(gens, SCAN_SKEW, apply_chunk, side=[front])
    for g in range(n_groups):
        z_ref[g] = zs[g]
        zout_ref[g] = zs[g]
    nconv_ref[...] = carry["new_cu"]
    nshift_ref[...] = carry["new_cp"]
    hold = j >= last_front
    cu_ref[...] = jnp.where(hold, cu_old, carry["new_cu"])
    cp_ref[...] = jnp.where(hold, cp_old, carry["new_cp"])


def _front_scan(x, tb, wts):
    bp, seq, d = x.shape
    n = bp * seq
    nb = n // tb
    bps = seq // tb
    cw_dim = wts["conv_w"].shape[-1]
    rw_dim = wts["w0"].shape[-1]
    rcols = wts["mu"].shape[-1]
    lanes = GROUP * HEAD
    ng = rw_dim // lanes
    cur = lambda j: jnp.minimum(j, nb - 1)
    prv = lambda j: jnp.maximum(j - 1, 0)
    row_cur = lambda width: pl.BlockSpec((tb, width), lambda j: (cur(j), 0))
    w_in = [wts[name] for name in FRONT_WEIGHTS]
    out_shape = ([jax.ShapeDtypeStruct((n, cw_dim), BF16)]
                 + [jax.ShapeDtypeStruct((n, rw_dim), F32)] * 5
                 + [jax.ShapeDtypeStruct((bp, 2, cw_dim), F32),
                    jax.ShapeDtypeStruct((bp, 1, rcols), F32),
                    jax.ShapeDtypeStruct((bp, ng, HEAD, lanes), F32)])
    out_specs = ([row_cur(cw_dim)] + [row_cur(rw_dim)] * 4
                 + [pl.BlockSpec((tb, rw_dim), lambda j: (prv(j), 0)),
                    pl.BlockSpec((None, 2, cw_dim), lambda j: (cur(j) // bps, 0, 0)),
                    pl.BlockSpec((None, 1, rcols), lambda j: (cur(j) // bps, 0, 0)),
                    pl.BlockSpec((None, ng, HEAD, lanes), lambda j: (prv(j) // bps, 0, 0, 0))])
    outs = pl.pallas_call(
        functools.partial(_front_scan_kernel, bps),
        grid=(nb + 1,),
        in_specs=[row_cur(d)] + [_const_spec(wt.shape) for wt in w_in],
        out_specs=out_specs,
        out_shape=out_shape,
        scratch_shapes=[pltpu.VMEM((2, len(SCAN_OPERANDS), tb, rw_dim), F32),
                        pltpu.VMEM((ng, HEAD, lanes), F32),
                        pltpu.VMEM((2, cw_dim), F32), pltpu.VMEM((1, rcols), F32)],
        compiler_params=pltpu.CompilerParams(
            dimension_semantics=("arbitrary",), vmem_limit_bytes=VMEM_LIMIT),
        name="front_scan",
    )(x.reshape(n, d), *w_in)
    yconv, r, k, v, g, y, nconv, nshift, z = outs
    per_seq = lambda t: t.reshape(bp, seq, t.shape[-1])
    return [per_seq(t) for t in (yconv, r, k, v, g, y)] + [nconv, nshift, z]


def _sscan_kernel(kk_ref, a_ref, lw_ref, k_ref, r_ref, v_ref, s_in_ref, y_ref, s_out_ref):
    n_t = kk_ref.shape[0]
    n_k = s_in_ref.shape[0]
    shape = s_in_ref.shape[1:]
    for t in range(n_t):
        src = s_in_ref if t == 0 else s_out_ref

        def p1(k8, acc, src=src, t=t):
            base = pl.multiple_of(k8 * 8, 8)
            kk8 = kk_ref[t, pl.ds(base, 8), :]
            for j in range(8):
                acc = acc + src[base + j] * kk8[j:j + 1, :]
            return acc

        sk = lax.fori_loop(0, n_k // 8, p1, jnp.zeros(shape, F32))
        vt = v_ref[t]

        def p2(k8, acc, src=src, t=t, sk=sk, vt=vt):
            base = pl.multiple_of(k8 * 8, 8)
            kk8 = kk_ref[t, pl.ds(base, 8), :]
            kka8 = kk8 * a_ref[t, pl.ds(base, 8), :]
            w8 = jnp.exp(lw_ref[t, pl.ds(base, 8), :])
            k8v = k_ref[t, pl.ds(base, 8), :]
            r8 = r_ref[t, pl.ds(base, 8), :]
            for j in range(8):
                snew = (src[base + j] * w8[j:j + 1, :] - sk * kka8[j:j + 1, :]
                        + vt * k8v[j:j + 1, :])
                s_out_ref[base + j] = snew
                acc = acc + snew * r8[j:j + 1, :]
            return acc

        y_ref[t] = lax.fori_loop(0, n_k // 8, p2, jnp.zeros(shape, F32))


def _sample_scan(kk, a, lw, k, r, v, state):
    n_t, w, nb = kk.shape
    nh = state.shape[0]
    spec = pl.BlockSpec((n_t, HEAD, nb), lambda h: (0, h, 0))
    sspec = pl.BlockSpec((None, HEAD, HEAD, nb), lambda h: (h, 0, 0, 0))
    return pl.pallas_call(
        _sscan_kernel,
        grid=(nh,),
        in_specs=[spec] * 6 + [sspec],
        out_specs=[spec, sspec],
        out_shape=[jax.ShapeDtypeStruct((n_t, w, nb), F32),
                   jax.ShapeDtypeStruct(state.shape, F32)],
        compiler_params=pltpu.CompilerParams(
            dimension_semantics=("arbitrary",), vmem_limit_bytes=VMEM_LIMIT),
        name="sample_scan",
    )(kk, a, lw, k, r, v, state)


def kernel(x_prompt, x_sample, state_conv, state_shift, state_wkv, norm1_g, w_in, conv_w, mu, w0,
           w2, a0, a2, g2, k_k, k_a, r_k, lnx_g, lnx_b, w_out, norm2_g, w_ff1, w_ff2, normf_g):
    depth = w_in.shape[0]
    assert depth == 1, "single-layer step only"
    bp, seq, d = x_prompt.shape
    bs, dseq, _ = x_sample.shape
    cw_dim = conv_w.shape[-1]
    rw_dim = w0.shape[-1]
    rcols = mu.shape[-1]
    nh = rw_dim // HEAD
    row = lambda t: t[0].reshape(1, -1)
    seg = jnp.arange(GROUP * HEAD) // HEAD
    wts = {
        "norm1_g": row(norm1_g), "w_in": w_in[0].astype(BF16), "conv_w": conv_w[0],
        "mu": row(mu), "w0": row(w0), "w2": w2[0].astype(BF16), "a0": row(a0),
        "a2": a2[0].astype(BF16), "g2": g2[0].astype(BF16), "k_k": row(k_k), "k_a": row(k_a),
        "r_k": row(r_k), "lnx_g": row(lnx_g), "lnx_b": row(lnx_b),
        "w_out": w_out[0].astype(BF16), "norm2_g": row(norm2_g),
        "w_ff1": w_ff1[0].astype(BF16), "w_ff2": w_ff2[0].astype(BF16),
        "normf_g": normf_g.reshape(1, -1),
        "bd": (seg[:, None] == seg[None, :]).astype(BF16),
    }

    tb = 256
    yconv, r, k, v, g, y, nconv_p, nshift_p, z = _front_scan(x_prompt, tb, wts)
    y_prompt = _back(x_prompt, yconv, y, r, k, v, g, tb, wts)
    ng = rw_dim // (GROUP * HEAD)
    zb = z.reshape(bp, ng, HEAD, GROUP, HEAD)
    wkv_p = jnp.transpose(zb, (0, 1, 3, 4, 2)).reshape(bp, nh, HEAD, HEAD)

    xs = jnp.transpose(x_sample, (1, 0, 2)).reshape(1, dseq * bs, d)
    c0 = jnp.transpose(state_conv[0], (1, 0, 2)).reshape(1, 2 * bs, cw_dim)
    s0 = state_shift[0].reshape(1, bs, rcols)
    tbs = 2 * bs
    yconv, r, k, v, kk, a, lw, g, nconv_s, nshift_s = _front(xs, c0, s0, bs, tbs, wts)
    to_fm = lambda t: jnp.transpose(t.reshape(dseq, bs, rw_dim), (0, 2, 1))
    st = jnp.transpose(state_wkv[0], (1, 3, 2, 0))
    y_fm, st_new = _sample_scan(to_fm(kk), to_fm(a), to_fm(lw), to_fm(k), to_fm(r), to_fm(v), st)
    y = jnp.transpose(y_fm, (0, 2, 1)).reshape(1, dseq * bs, rw_dim)
    ys = _back(xs, yconv, y, r, k, v, g, tbs, wts)
    y_sample = jnp.transpose(ys.reshape(dseq, bs, d), (1, 0, 2))
    wkv_s = jnp.transpose(st_new, (3, 0, 2, 1))
    nconv_s = jnp.transpose(nconv_s.reshape(2, bs, cw_dim), (1, 0, 2))

    return (y_prompt, y_sample,
            nconv_p[None], nshift_p.reshape(1, bp, rcols), wkv_p[None],
            nconv_s[None], nshift_s.reshape(1, bs, rcols), wkv_s[None])
```

```python
import functools

import jax
import jax.numpy as jnp
from jax import lax
from jax.experimental import pallas as pl
from jax.experimental.pallas import tpu as pltpu

F32 = jnp.float32
BF16 = jnp.bfloat16

HEAD = 64
RMS_EPS = 1e-6
GN_EPS = 64e-5
NORM_EPS = 1e-12
VMEM_LIMIT = 56 * 1024 * 1024
CHUNK = 64
GROUP = 4
SCAN_SKEW = 1
SIDE_RATE = 1
SCAN_DELAY = 0


def _dot(a, b):
    return jnp.dot(a, b, preferred_element_type=F32)


def _rmsnorm(x, g):
    return x * lax.rsqrt(jnp.mean(x * x, axis=-1, keepdims=True) + RMS_EPS) * g


def _seg_sum(x, bd):
    lanes = bd.shape[0]
    xb = x.astype(BF16)
    return jnp.concatenate([_dot(xb[:, i:i + lanes], bd) for i in range(0, x.shape[1], lanes)],
                           axis=1)


def _shift_rows(x, k, head):
    rows = x.shape[0]
    if k == rows:
        return head
    if k % 8 == 0:
        return jnp.concatenate([head, x[:rows - k]], axis=0)
    rolled = pltpu.roll(x, k, 0)
    row = lax.broadcasted_iota(jnp.int32, x.shape, 0)
    out = rolled
    for j in range(k):
        out = jnp.where(row == j, head[j:j + 1, :], out)
    return out


FRONT_WEIGHTS = ("norm1_g", "w_in", "conv_w", "mu", "w0", "w2", "a0", "a2", "g2", "k_k", "k_a",
                 "bd")
PROJ_PIECE = 512


def _front_stages(x, cu, cp, s, w, store):
    rows = x.shape[0]
    cw_dim = cu.shape[1]
    rw_dim = w["w0"].shape[1]
    win = w["w_in"]
    ncols = win.shape[1]
    h = _rmsnorm(x, w["norm1_g"][...]).astype(BF16)
    pieces = []
    for c0 in range(0, ncols, PROJ_PIECE):
        pieces.append(_dot(h, win[:, c0:min(c0 + PROJ_PIECE, ncols)]))
        yield
    proj = jnp.concatenate(pieces, axis=1)
    c_b = proj[:, :cw_dim]
    c_c = proj[:, cw_dim:2 * cw_dim]
    c_x = proj[:, 2 * cw_dim:3 * cw_dim]
    p = proj[:, 3 * cw_dim:]

    u = c_c * c_x
    u1 = _shift_rows(u, s, cu[s:2 * s])
    u2 = _shift_rows(u, 2 * s, cu)
    cw = w["conv_w"][...]
    conv = cw[0:1] * u2 + cw[1:2] * u1 + cw[2:3] * u
    store("yconv", (c_b * conv).astype(BF16))
    store("new_cu", u[rows - 2 * s:])
    yield

    prev = _shift_rows(p, s, cp)
    store("new_cp", p[rows - s:])
    xm = p + (prev - p) * w["mu"][...]
    r = xm[:, :rw_dim]
    k = xm[:, rw_dim:2 * rw_dim]
    v = xm[:, 2 * rw_dim:3 * rw_dim]
    store("r", r)
    store("v", v)
    o = 3 * rw_dim
    d_w = w["w2"].shape[0]
    d_a = w["a2"].shape[0]
    pw = xm[:, o:o + d_w]
    pa = xm[:, o + d_w:o + d_w + d_a]
    pg = xm[:, o + d_w + d_a:]
    yield

    wl = w["w0"][...] + _dot(jnp.tanh(pw).astype(BF16), w["w2"][...])
    z = -wl
    softplus = jnp.maximum(z, 0.0) + jnp.log(1.0 + jnp.exp(-jnp.abs(z)))
    store("lw", -jnp.exp(-softplus - 0.5))
    yield
    a = 1.0 / (1.0 + jnp.exp(-(w["a0"][...] + _dot(pa.astype(BF16), w["a2"][...]))))
    store("a", a)
    store("g", _dot((1.0 / (1.0 + jnp.exp(-pg))).astype(BF16), w["g2"][...]))
    yield
    kk = k * w["k_k"][...]
    nrm = jnp.sqrt(_seg_sum(kk * kk, w["bd"][...]))
    store("kk", kk / jnp.maximum(nrm, NORM_EPS))
    store("k", k * (1.0 + (a - 1.0) * w["k_a"][...]))
    yield


FRONT_OUTPUTS = ("yconv", "r", "k", "v", "kk", "a", "lw", "g", "new_cu", "new_cp")


def _front_kernel(s, x_ref, conv0_ref, shift0_ref, *refs):
    nw = len(FRONT_WEIGHTS)
    w = dict(zip(FRONT_WEIGHTS, refs[:nw]))
    outs = dict(zip(FRONT_OUTPUTS, refs[nw:nw + len(FRONT_OUTPUTS)]))
    cu_ref, cp_ref = refs[nw + len(FRONT_OUTPUTS):]

    @pl.when(pl.program_id(1) == 0)
    def _():
        cu_ref[...] = conv0_ref[...]
        cp_ref[...] = shift0_ref[...]

    carry = {}

    def store(name, value):
        outs[name][...] = value
        carry[name] = value

    for _ in _front_stages(x_ref[...], cu_ref[...], cp_ref[...], s, w, store):
        pass
    cu_ref[...] = carry["new_cu"]
    cp_ref[...] = carry["new_cp"]


def _const_spec(shape):
    nd = len(shape)
    return pl.BlockSpec(shape, lambda *_: (0,) * nd, pipeline_mode=pl.Buffered(1))


def _front(x, conv0, shift0, s, tb, wts):
    ns, tt, d = x.shape
    cw_dim = conv0.shape[-1]
    rw_dim = wts["w0"].shape[-1]
    rcols = shift0.shape[-1]
    nblk = tt // tb
    row_spec = lambda w: pl.BlockSpec((None, tb, w), lambda b, i: (b, i, 0))
    st_spec = lambda r, w: pl.BlockSpec((None, r, w), lambda b, i: (b, 0, 0))
    w_in = [wts[n] for n in FRONT_WEIGHTS]
    out_shapes = ([jax.ShapeDtypeStruct((ns, tt, cw_dim), BF16)]
                  + [jax.ShapeDtypeStruct((ns, tt, rw_dim), F32)] * 7
                  + [jax.ShapeDtypeStruct((ns, 2 * s, cw_dim), F32),
                     jax.ShapeDtypeStruct((ns, s, rcols), F32)])
    out_specs = ([row_spec(cw_dim)] + [row_spec(rw_dim)] * 7
                 + [st_spec(2 * s, cw_dim), st_spec(s, rcols)])
    return pl.pallas_call(
        functools.partial(_front_kernel, s),
        grid=(ns, nblk),
        in_specs=[row_spec(d), st_spec(2 * s, cw_dim), st_spec(s, rcols)]
        + [_const_spec(w.shape) for w in w_in],
        out_specs=out_specs,
        out_shape=out_shapes,
        scratch_shapes=[pltpu.VMEM((2 * s, cw_dim), F32), pltpu.VMEM((s, rcols), F32)],
        compiler_params=pltpu.CompilerParams(
            dimension_semantics=("arbitrary", "arbitrary"), vmem_limit_bytes=VMEM_LIMIT),
        name="front",
    )(x, conv0, shift0, *w_in)


def _back_kernel(x_ref, yconv_ref, y_ref, r_ref, k_ref, v_ref, g_ref,
                 rk_ref, lg_ref, lb_ref, wo_ref, n2_ref, f1_ref, f2_ref, nf_ref, bd_ref,
                 out_ref):
    bd = bd_ref[...]
    y = y_ref[...]
    inv_n = 1.0 / HEAD
    mean = _seg_sum(y, bd) * inv_n
    dlt = y - mean
    var = _seg_sum(dlt * dlt, bd) * inv_n
    yn = dlt * lax.rsqrt(var + GN_EPS) * lg_ref[...] + lb_ref[...]
    v = v_ref[...]
    bonus = _seg_sum(r_ref[...] * k_ref[...] * rk_ref[...], bd) * v
    y_rwkv = ((yn + bonus) * g_ref[...]).astype(BF16)
    cw_dim = yconv_ref.shape[-1]
    x = x_ref[...]
    x1 = x + _dot(yconv_ref[...], wo_ref[:cw_dim, :]) + _dot(y_rwkv, wo_ref[cw_dim:, :])
    h2 = _rmsnorm(x1, n2_ref[...]).astype(BF16)
    f = jnp.maximum(_dot(h2, f1_ref[...]), 0.0)
    x2 = x1 + _dot((f * f).astype(BF16), f2_ref[...])
    out_ref[...] = _rmsnorm(x2, nf_ref[...])


def _back(x, yconv, y, r, k, v, g, tb, wts):
    ns, tt, d = x.shape
    nblk = tt // tb
    row_spec = lambda w: pl.BlockSpec((None, tb, w), lambda b, i: (b, i, 0))
    names = ["r_k", "lnx_g", "lnx_b", "w_out", "norm2_g", "w_ff1", "w_ff2", "normf_g", "bd"]
    w_in = [wts[n] for n in names]
    acts = [x, yconv, y, r, k, v, g]
    return pl.pallas_call(
        _back_kernel,
        grid=(ns, nblk),
        in_specs=[row_spec(a.shape[-1]) for a in acts] + [_const_spec(w.shape) for w in w_in],
        out_specs=row_spec(d),
        out_shape=jax.ShapeDtypeStruct((ns, tt, d), F32),
        compiler_params=pltpu.CompilerParams(
            dimension_semantics=("arbitrary", "arbitrary"), vmem_limit_bytes=VMEM_LIMIT),
        name="back",
    )(*acts, *w_in)


def _mm(a, b):
    return jnp.dot(a.astype(BF16), b.astype(BF16), preferred_element_type=F32)


def _mm_nt(a, b):
    return lax.dot_general(a.astype(BF16), b.astype(BF16), (((1,), (1,)), ((), ())),
                           preferred_element_type=F32)


def _cumsum_rows(x):
    rows = x.shape[0]
    row = lax.broadcasted_iota(jnp.int32, x.shape, 0)
    sh = 1
    while sh < rows:
        x = x + jnp.where(row >= sh, pltpu.roll(x, sh, 0), 0.0)
        sh *= 2
    return x


def _scan_masks(c):
    lanes = GROUP * HEAD
    gc = GROUP * c
    i0 = lambda shape: lax.broadcasted_iota(jnp.int32, shape, 0)
    i1 = lambda shape: lax.broadcasted_iota(jnp.int32, shape, 1)
    col_t = i1((c, gc)) % c
    return dict(
        lane_head=(i0((gc, lanes)) // c) == (i1((gc, lanes)) // HEAD),
        strict=col_t < i0((c, gc)),
        incl=col_t <= i0((c, gc)),
        bd_cc=(i0((gc, gc)) // c) == (i1((gc, gc)) // c),
        bd_ll=(i0((lanes, lanes)) // HEAD) == (i1((lanes, lanes)) // HEAD),
    )


def _chunk_precompute(load, masks):
    lw, kk, a, r, k, v = load()
    c = lw.shape[0]
    assert c == HEAD, "side-by-side forms share lane offsets only when CHUNK == HEAD"

    def bd_rows(m):
        return jnp.where(masks["lane_head"], jnp.concatenate([m] * GROUP, axis=0), 0.0)

    def bd_mat(m):
        return jnp.where(masks["bd_cc"], jnp.concatenate([m] * GROUP, axis=0), 0.0)

    def heads_t(m):
        mt = jnp.transpose(bd_rows(m))
        return sum(mt[h * HEAD:(h + 1) * HEAD, :] for h in range(GROUP))

    cum = _cumsum_rows(lw)
    cum_last = cum[c - 1:c, :]
    beta = kk * a
    alpha_t = -kk * jnp.exp(cum - lw)
    r_t = r * jnp.exp(cum)
    inv = jnp.exp(-cum)
    rem = jnp.exp(cum_last - cum)
    ar = jnp.concatenate([alpha_t, r_t], axis=0)
    g_b = _mm_nt(ar, bd_rows(beta * inv))
    g_k = _mm_nt(ar, bd_rows(k * inv))
    yield
    a_ab = jnp.where(masks["strict"], g_b[:c], 0.0)
    a_rb = jnp.where(masks["incl"], g_b[c:], 0.0)
    a_ak = jnp.where(masks["strict"], g_k[:c], 0.0)
    a_rk = jnp.where(masks["incl"], g_k[c:], 0.0)

    tm = a_ab
    n_sq = max(c.bit_length() - 2, 0)
    if n_sq:
        xp = _mm(a_ab, bd_mat(a_ab))
    b_t = heads_t(beta * rem)
    k_t = heads_t(k * rem)
    res = _mm(jnp.concatenate([a_ak, a_rk, k_t], axis=0), bd_rows(v))
    yield
    akv, rkv, nh_k = res[:c], res[c:2 * c], res[2 * c:]
    for i in range(n_sq):
        rhs = bd_mat(xp)
        if i < n_sq - 1:
            res = _mm(jnp.concatenate([tm, xp], axis=0), rhs)
            yield
            tm = tm + xp + res[:c]
            xp = res[c:]
        else:
            res = _mm(tm, rhs)
            yield
            tm = tm + xp + res

    wa = alpha_t + _mm(tm, bd_rows(alpha_t))
    uv = akv + _mm(tm, bd_rows(akv))
    yield
    lhs = jnp.concatenate([a_rb, b_t], axis=0)
    res_w = _mm(lhs, bd_rows(wa))
    res_u = _mm(lhs, bd_rows(uv))
    ph = heads_t(jnp.broadcast_to(jnp.exp(cum_last), cum.shape))
    yield
    rq = r_t + res_w[:c]
    yv = rkv + res_u[:c]
    mh = res_w[c:]
    nh = res_u[c:] + nh_k
    return rq, yv, mh, nh, ph


def _run_staggered(gens_by_chunk, skew, on_chunk_done, side=(), delay=0):
    n = len(gens_by_chunk)
    results = [[None] * len(gens) for gens in gens_by_chunk]
    live = {(ci, j) for ci, gens in enumerate(gens_by_chunk) for j in range(len(gens))}
    side = list(side)
    n_done = 0
    rnd = 0
    while live or side:
        for gen in list(side) * SIDE_RATE:
            if gen in side and next(gen, side) is side:
                side.remove(gen)
        started = 0 if rnd < delay else min(n, (rnd - delay) // skew + 1)
        for ci in range(started):
            for j, gen in enumerate(gens_by_chunk[ci]):
                if (ci, j) in live:
                    try:
                        next(gen)
                    except StopIteration as stop:
                        results[ci][j] = stop.value
                        live.discard((ci, j))
        while n_done < n and not any(ci == n_done for ci, _ in live):
            on_chunk_done(n_done, results[n_done])
            n_done += 1
        rnd += 1


SCAN_OPERANDS = ("lw", "kk", "a", "r", "k", "v")


def _front_scan_kernel(blocks_per_seq, x_ref, *refs):
    nw = len(FRONT_WEIGHTS)
    w = dict(zip(FRONT_WEIGHTS, refs[:nw]))
    (yconv_ref, r_ref, k_ref, v_ref, g_ref, y_ref, nconv_ref, nshift_ref, zout_ref,
     stage_ref, z_ref, cu_ref, cp_ref) = refs[nw:]
    j = pl.program_id(0)
    last_front = pl.num_programs(0) - 2

    @pl.when(j == 0)
    def _():
        stage_ref[...] = jnp.zeros_like(stage_ref)
        z_ref[...] = jnp.zeros_like(z_ref)
        cu_ref[...] = jnp.zeros_like(cu_ref)
        cp_ref[...] = jnp.zeros_like(cp_ref)

    wpar = j % 2
    rpar = 1 - wpar
    keep_f = jnp.where(jnp.minimum(j, last_front) % blocks_per_seq == 0, 0.0, 1.0)
    keep_s = jnp.where((j + blocks_per_seq - 1) % blocks_per_seq == 0, 0.0, 1.0)

    outs = dict(yconv=yconv_ref, r=r_ref, k=k_ref, v=v_ref, g=g_ref)
    slot = {name: i for i, name in enumerate(SCAN_OPERANDS)}
    carry = {}

    def store(name, value):
        if name in outs:
            outs[name][...] = value
        if name in slot:
            stage_ref[wpar, slot[name]] = value
        carry[name] = value

    cu_old = cu_ref[...]
    cp_old = cp_ref[...]
    front = _front_stages(x_ref[...], cu_old * keep_f, cp_old * keep_f, 1, w, store)

    lanes = GROUP * HEAD
    n_groups = stage_ref.shape[3] // lanes
    n_chunks = stage_ref.shape[2] // CHUNK
    masks = _scan_masks(CHUNK)

    def loader(ci, g):
        rows = slice(ci * CHUNK, (ci + 1) * CHUNK)
        sl = slice(g * lanes, (g + 1) * lanes)
        return lambda: tuple(stage_ref[rpar, i, rows, sl] for i in range(len(SCAN_OPERANDS)))

    gens = [[_chunk_precompute(loader(ci, g), masks) for g in range(n_groups)]
            for ci in range(n_chunks)]
    zs = [z_ref[g] * keep_s for g in range(n_groups)]

    def apply_chunk(ci, pre):
        res = [_mm(jnp.concatenate([pre[g][0], pre[g][2]], axis=0),
                   jnp.where(masks["bd_ll"], jnp.concatenate([zs[g]] * GROUP, axis=0), 0.0))
               for g in range(n_groups)]
        for g in range(n_groups):
            _, yv, _, nh, ph = pre[g]
            y_ref[ci * CHUNK:(ci + 1) * CHUNK, g * lanes:(g + 1) * lanes] = res[g][:CHUNK] + yv
            zs[g] = ph * zs[g] + res[g][CHUNK:] + nh

    _run_staggered(gens, SCAN_SKEW, apply_chunk, side=[front], delay=SCAN_DELAY)
    for g in range(n_groups):
        z_ref[g] = zs[g]
        zout_ref[g] = zs[g]
    nconv_ref[...] = carry["new_cu"]
    nshift_ref[...] = carry["new_cp"]
    hold = j >= last_front
    cu_ref[...] = jnp.where(hold, cu_old, carry["new_cu"])
    cp_ref[...] = jnp.where(hold, cp_old, carry["new_cp"])


def _front_scan(x, tb, wts):
    bp, seq, d = x.shape
    n = bp * seq
    nb = n // tb
    bps = seq // tb
    cw_dim = wts["conv_w"].shape[-1]
    rw_dim = wts["w0"].shape[-1]
    rcols = wts["mu"].shape[-1]
    lanes = GROUP * HEAD
    ng = rw_dim // lanes
    cur = lambda j: jnp.minimum(j, nb - 1)
    prv = lambda j: jnp.maximum(j - 1, 0)
    row_cur = lambda width: pl.BlockSpec((tb, width), lambda j: (cur(j), 0))
    w_in = [wts[name] for name in FRONT_WEIGHTS]
    out_shape = ([jax.ShapeDtypeStruct((n, cw_dim), BF16)]
                 + [jax.ShapeDtypeStruct((n, rw_dim), F32)] * 5
                 + [jax.ShapeDtypeStruct((bp, 2, cw_dim), F32),
                    jax.ShapeDtypeStruct((bp, 1, rcols), F32),
                    jax.ShapeDtypeStruct((bp, ng, HEAD, lanes), F32)])
    out_specs = ([row_cur(cw_dim)] + [row_cur(rw_dim)] * 4
                 + [pl.BlockSpec((tb, rw_dim), lambda j: (prv(j), 0)),
                    pl.BlockSpec((None, 2, cw_dim), lambda j: (cur(j) // bps, 0, 0)),
                    pl.BlockSpec((None, 1, rcols), lambda j: (cur(j) // bps, 0, 0)),
                    pl.BlockSpec((None, ng, HEAD, lanes), lambda j: (prv(j) // bps, 0, 0, 0))])
    outs = pl.pallas_call(
        functools.partial(_front_scan_kernel, bps),
        grid=(nb + 1,),
        in_specs=[row_cur(d)] + [_const_spec(wt.shape) for wt in w_in],
        out_specs=out_specs,
        out_shape=out_shape,
        scratch_shapes=[pltpu.VMEM((2, len(SCAN_OPERANDS), tb, rw_dim), F32),
                        pltpu.VMEM((ng, HEAD, lanes), F32),
                        pltpu.VMEM((2, cw_dim), F32), pltpu.VMEM((1, rcols), F32)],
        compiler_params=pltpu.CompilerParams(
            dimension_semantics=("arbitrary",), vmem_limit_bytes=VMEM_LIMIT),
        name="front_scan",
    )(x.reshape(n, d), *w_in)
    yconv, r, k, v, g, y, nconv, nshift, z = outs
    per_seq = lambda t: t.reshape(bp, seq, t.shape[-1])
    return [per_seq(t) for t in (yconv, r, k, v, g, y)] + [nconv, nshift, z]


LANE_SLAB = 128


def _sscan_kernel(kk_ref, a_ref, lw_ref, k_ref, r_ref, v_ref, s_in_ref, y_ref, s_out_ref,
                  fm_ref, yt_ref):
    n_t = kk_ref.shape[0]
    heads, n_k = s_in_ref.shape[:2]
    shape = s_in_ref.shape[2:]
    i_kk, i_kka, i_w, i_k, i_r, i_v = range(6)
    for t in range(n_t):
        kk_t = jnp.transpose(kk_ref[t])
        fm_ref[i_kk, t] = kk_t
        fm_ref[i_kka, t] = kk_t * jnp.transpose(a_ref[t])
        fm_ref[i_w, t] = jnp.exp(jnp.transpose(lw_ref[t]))
        fm_ref[i_k, t] = jnp.transpose(k_ref[t])
        fm_ref[i_r, t] = jnp.transpose(r_ref[t])
        fm_ref[i_v, t] = jnp.transpose(v_ref[t])

    zeros = jnp.zeros(shape, F32)
    for hd in range(heads):
        off = hd * HEAD

        def rows8(i, t, k8, off=off):
            return fm_ref[i, t, pl.ds(pl.multiple_of(off + k8 * 8, 8), 8), :]

        def first_sk(k8, acc, hd=hd, rows8=rows8):
            kk8 = rows8(i_kk, 0, k8)
            for j in range(8):
                acc = acc + s_in_ref[hd, k8 * 8 + j] * kk8[j:j + 1, :]
            return acc

        sk = lax.fori_loop(0, n_k // 8, first_sk, zeros)
        for t in range(n_t):
            src = s_in_ref if t == 0 else s_out_ref
            vt = fm_ref[i_v, t, off:off + HEAD, :]
            last = t == n_t - 1

            def step(k8, carry, hd=hd, rows8=rows8, t=t, src=src, vt=vt, sk=sk, last=last):
                acc, sk_next = carry
                kka8 = rows8(i_kka, t, k8)
                w8 = rows8(i_w, t, k8)
                k8v = rows8(i_k, t, k8)
                r8 = rows8(i_r, t, k8)
                kkn8 = None if last else rows8(i_kk, t + 1, k8)
                for j in range(8):
                    snew = (src[hd, k8 * 8 + j] * w8[j:j + 1, :] - sk * kka8[j:j + 1, :]
                            + vt * k8v[j:j + 1, :])
                    s_out_ref[hd, k8 * 8 + j] = snew
                    acc = acc + snew * r8[j:j + 1, :]
                    if not last:
                        sk_next = sk_next + snew * kkn8[j:j + 1, :]
                return acc, sk_next

            y, sk = lax.fori_loop(0, n_k // 8, step, (zeros, zeros))
            yt_ref[t, off:off + HEAD, :] = y
    for t in range(n_t):
        y_ref[t] = jnp.transpose(yt_ref[t])


def _sample_scan(kk, a, lw, k, r, v, state):
    n_t, nb, w = kk.shape
    heads = LANE_SLAB // HEAD
    spec = pl.BlockSpec((n_t, nb, LANE_SLAB), lambda i: (0, 0, i))
    sspec = pl.BlockSpec((heads, HEAD, HEAD, nb), lambda i: (i, 0, 0, 0))
    return pl.pallas_call(
        _sscan_kernel,
        grid=(w // LANE_SLAB,),
        in_specs=[spec] * 6 + [sspec],
        out_specs=[spec, sspec],
        out_shape=[jax.ShapeDtypeStruct((n_t, nb, w), F32),
                   jax.ShapeDtypeStruct(state.shape, F32)],
        scratch_shapes=[pltpu.VMEM((6, n_t, LANE_SLAB, nb), F32),
                        pltpu.VMEM((n_t, LANE_SLAB, nb), F32)],
        compiler_params=pltpu.CompilerParams(
            dimension_semantics=("arbitrary",), vmem_limit_bytes=VMEM_LIMIT),
        name="sample_scan",
    )(kk, a, lw, k, r, v, state)


def kernel(x_prompt, x_sample, state_conv, state_shift, state_wkv, norm1_g, w_in, conv_w, mu, w0,
           w2, a0, a2, g2, k_k, k_a, r_k, lnx_g, lnx_b, w_out, norm2_g, w_ff1, w_ff2, normf_g):
    depth = w_in.shape[0]
    assert depth == 1, "single-layer step only"
    bp, seq, d = x_prompt.shape
    bs, dseq, _ = x_sample.shape
    cw_dim = conv_w.shape[-1]
    rw_dim = w0.shape[-1]
    rcols = mu.shape[-1]
    nh = rw_dim // HEAD
    row = lambda t: t[0].reshape(1, -1)
    seg = jnp.arange(GROUP * HEAD) // HEAD
    wts = {
        "norm1_g": row(norm1_g), "w_in": w_in[0].astype(BF16), "conv_w": conv_w[0],
        "mu": row(mu), "w0": row(w0), "w2": w2[0].astype(BF16), "a0": row(a0),
        "a2": a2[0].astype(BF16), "g2": g2[0].astype(BF16), "k_k": row(k_k), "k_a": row(k_a),
        "r_k": row(r_k), "lnx_g": row(lnx_g), "lnx_b": row(lnx_b),
        "w_out": w_out[0].astype(BF16), "norm2_g": row(norm2_g),
        "w_ff1": w_ff1[0].astype(BF16), "w_ff2": w_ff2[0].astype(BF16),
        "normf_g": normf_g.reshape(1, -1),
        "bd": (seg[:, None] == seg[None, :]).astype(BF16),
    }

    tb = 256
    yconv, r, k, v, g, y, nconv_p, nshift_p, z = _front_scan(x_prompt, tb, wts)
    y_prompt = _back(x_prompt, yconv, y, r, k, v, g, 512, wts)
    ng = rw_dim // (GROUP * HEAD)
    zb = z.reshape(bp, ng, HEAD, GROUP, HEAD)
    wkv_p = jnp.transpose(zb, (0, 1, 3, 4, 2)).reshape(bp, nh, HEAD, HEAD)

    xs = jnp.transpose(x_sample, (1, 0, 2)).reshape(1, dseq * bs, d)
    c0 = jnp.transpose(state_conv[0], (1, 0, 2)).reshape(1, 2 * bs, cw_dim)
    s0 = state_shift[0].reshape(1, bs, rcols)
    tbs = 2 * bs
    yconv, r, k, v, kk, a, lw, g, nconv_s, nshift_s = _front(xs, c0, s0, bs, tbs, wts)
    tm = lambda t: t.reshape(dseq, bs, rw_dim)
    st = jnp.transpose(state_wkv[0], (1, 3, 2, 0))
    y_tm, st_new = _sample_scan(tm(kk), tm(a), tm(lw), tm(k), tm(r), tm(v), st)
    y = y_tm.reshape(1, dseq * bs, rw_dim)
    ys = _back(xs, yconv, y, r, k, v, g, tbs, wts)
    y_sample = jnp.transpose(ys.reshape(dseq, bs, d), (1, 0, 2))
    wkv_s = jnp.transpose(st_new, (3, 0, 2, 1))
    nconv_s = jnp.transpose(nconv_s.reshape(2, bs, cw_dim), (1, 0, 2))

    return (y_prompt, y_sample,
            nconv_p[None], nshift_p.reshape(1, bp, rcols), wkv_p[None],
            nconv_s[None], nshift_s.reshape(1, bs, rcols), wkv_s[None])
```

```python
import functools

import jax
import jax.numpy as jnp
from jax import lax
from jax.experimental import pallas as pl
from jax.experimental.pallas import tpu as pltpu

F32 = jnp.float32
BF16 = jnp.bfloat16

HEAD = 64
RMS_EPS = 1e-6
GN_EPS = 64e-5
NORM_EPS = 1e-12
VMEM_LIMIT = 56 * 1024 * 1024
CHUNK = 64
GROUP = 4
SCAN_SKEW = 1
FRONT_SCAN_ROWS = 256
BACK_ROWS = 512


def _dot(a, b):
    return jnp.dot(a, b, preferred_element_type=F32)


def _rmsnorm(x, g):
    return x * lax.rsqrt(jnp.mean(x * x, axis=-1, keepdims=True) + RMS_EPS) * g


def _seg_sum(x, bd):
    lanes = bd.shape[0]
    xb = x.astype(BF16)
    return jnp.concatenate([_dot(xb[:, i:i + lanes], bd) for i in range(0, x.shape[1], lanes)],
                           axis=1)


def _shift_rows(x, k, head):
    rows = x.shape[0]
    if k == rows:
        return head
    if k % 8 == 0:
        return jnp.concatenate([head, x[:rows - k]], axis=0)
    rolled = pltpu.roll(x, k, 0)
    row = lax.broadcasted_iota(jnp.int32, x.shape, 0)
    out = rolled
    for j in range(k):
        out = jnp.where(row == j, head[j:j + 1, :], out)
    return out


FRONT_WEIGHTS = ("norm1_g", "w_in", "conv_w", "mu", "w0", "w2", "a0", "a2", "g2", "k_k", "k_a",
                 "bd")
PROJ_PIECE = 512


def _front_stages(x, cu, cp, s, w, store):
    rows = x.shape[0]
    cw_dim = cu.shape[1]
    rw_dim = w["w0"].shape[1]
    win = w["w_in"]
    ncols = win.shape[1]
    h = _rmsnorm(x, w["norm1_g"][...]).astype(BF16)
    pieces = []
    for c0 in range(0, ncols, PROJ_PIECE):
        pieces.append(_dot(h, win[:, c0:min(c0 + PROJ_PIECE, ncols)]))
        yield
    proj = jnp.concatenate(pieces, axis=1)
    c_b = proj[:, :cw_dim]
    c_c = proj[:, cw_dim:2 * cw_dim]
    c_x = proj[:, 2 * cw_dim:3 * cw_dim]
    p = proj[:, 3 * cw_dim:]

    u = c_c * c_x
    u1 = _shift_rows(u, s, cu[s:2 * s])
    u2 = _shift_rows(u, 2 * s, cu)
    cw = w["conv_w"][...]
    conv = cw[0:1] * u2 + cw[1:2] * u1 + cw[2:3] * u
    store("yconv", (c_b * conv).astype(BF16))
    store("new_cu", u[rows - 2 * s:])
    yield

    prev = _shift_rows(p, s, cp)
    store("new_cp", p[rows - s:])
    xm = p + (prev - p) * w["mu"][...]
    r = xm[:, :rw_dim]
    k = xm[:, rw_dim:2 * rw_dim]
    v = xm[:, 2 * rw_dim:3 * rw_dim]
    store("r", r)
    store("v", v)
    o = 3 * rw_dim
    d_w = w["w2"].shape[0]
    d_a = w["a2"].shape[0]
    pw = xm[:, o:o + d_w]
    pa = xm[:, o + d_w:o + d_w + d_a]
    pg = xm[:, o + d_w + d_a:]
    yield

    wl = w["w0"][...] + _dot(jnp.tanh(pw).astype(BF16), w["w2"][...])
    z = -wl
    softplus = jnp.maximum(z, 0.0) + jnp.log(1.0 + jnp.exp(-jnp.abs(z)))
    store("lw", -jnp.exp(-softplus - 0.5))
    yield
    a = 1.0 / (1.0 + jnp.exp(-(w["a0"][...] + _dot(pa.astype(BF16), w["a2"][...]))))
    store("a", a)
    store("g", _dot((1.0 / (1.0 + jnp.exp(-pg))).astype(BF16), w["g2"][...]))
    yield
    kk = k * w["k_k"][...]
    nrm = jnp.sqrt(_seg_sum(kk * kk, w["bd"][...]))
    store("kk", kk / jnp.maximum(nrm, NORM_EPS))
    store("k", k * (1.0 + (a - 1.0) * w["k_a"][...]))
    yield


FRONT_OUTPUTS = ("yconv", "r", "k", "v", "kk", "a", "lw", "g", "new_cu", "new_cp")


def _front_kernel(s, x_ref, conv0_ref, shift0_ref, *refs):
    nw = len(FRONT_WEIGHTS)
    w = dict(zip(FRONT_WEIGHTS, refs[:nw]))
    outs = dict(zip(FRONT_OUTPUTS, refs[nw:nw + len(FRONT_OUTPUTS)]))
    cu_ref, cp_ref = refs[nw + len(FRONT_OUTPUTS):]

    @pl.when(pl.program_id(1) == 0)
    def _():
        cu_ref[...] = conv0_ref[...]
        cp_ref[...] = shift0_ref[...]

    carry = {}

    def store(name, value):
        outs[name][...] = value
        carry[name] = value

    for _ in _front_stages(x_ref[...], cu_ref[...], cp_ref[...], s, w, store):
        pass
    cu_ref[...] = carry["new_cu"]
    cp_ref[...] = carry["new_cp"]


def _const_spec(shape):
    nd = len(shape)
    return pl.BlockSpec(shape, lambda *_: (0,) * nd, pipeline_mode=pl.Buffered(1))


def _front(x, conv0, shift0, s, tb, wts):
    ns, tt, d = x.shape
    cw_dim = conv0.shape[-1]
    rw_dim = wts["w0"].shape[-1]
    rcols = shift0.shape[-1]
    nblk = tt // tb
    row_spec = lambda w: pl.BlockSpec((None, tb, w), lambda b, i: (b, i, 0))
    st_spec = lambda r, w: pl.BlockSpec((None, r, w), lambda b, i: (b, 0, 0))
    w_in = [wts[n] for n in FRONT_WEIGHTS]
    out_shapes = ([jax.ShapeDtypeStruct((ns, tt, cw_dim), BF16)]
                  + [jax.ShapeDtypeStruct((ns, tt, rw_dim), F32)] * 7
                  + [jax.ShapeDtypeStruct((ns, 2 * s, cw_dim), F32),
                     jax.ShapeDtypeStruct((ns, s, rcols), F32)])
    out_specs = ([row_spec(cw_dim)] + [row_spec(rw_dim)] * 7
                 + [st_spec(2 * s, cw_dim), st_spec(s, rcols)])
    return pl.pallas_call(
        functools.partial(_front_kernel, s),
        grid=(ns, nblk),
        in_specs=[row_spec(d), st_spec(2 * s, cw_dim), st_spec(s, rcols)]
        + [_const_spec(w.shape) for w in w_in],
        out_specs=out_specs,
        out_shape=out_shapes,
        scratch_shapes=[pltpu.VMEM((2 * s, cw_dim), F32), pltpu.VMEM((s, rcols), F32)],
        compiler_params=pltpu.CompilerParams(
            dimension_semantics=("arbitrary", "arbitrary"), vmem_limit_bytes=VMEM_LIMIT),
        name="front",
    )(x, conv0, shift0, *w_in)


def _back_kernel(x_ref, yconv_ref, y_ref, r_ref, k_ref, v_ref, g_ref,
                 rk_ref, lg_ref, lb_ref, wo_ref, n2_ref, f1_ref, f2_ref, nf_ref, bd_ref,
                 out_ref):
    bd = bd_ref[...]
    y = y_ref[...]
    inv_n = 1.0 / HEAD
    mean = _seg_sum(y, bd) * inv_n
    dlt = y - mean
    var = _seg_sum(dlt * dlt, bd) * inv_n
    yn = dlt * lax.rsqrt(var + GN_EPS) * lg_ref[...] + lb_ref[...]
    v = v_ref[...]
    bonus = _seg_sum(r_ref[...] * k_ref[...] * rk_ref[...], bd) * v
    y_rwkv = ((yn + bonus) * g_ref[...]).astype(BF16)
    cw_dim = yconv_ref.shape[-1]
    x = x_ref[...]
    x1 = x + _dot(yconv_ref[...], wo_ref[:cw_dim, :]) + _dot(y_rwkv, wo_ref[cw_dim:, :])
    h2 = _rmsnorm(x1, n2_ref[...]).astype(BF16)
    f = jnp.maximum(_dot(h2, f1_ref[...]), 0.0)
    x2 = x1 + _dot((f * f).astype(BF16), f2_ref[...])
    out_ref[...] = _rmsnorm(x2, nf_ref[...])


def _back(x, yconv, y, r, k, v, g, tb, wts):
    ns, tt, d = x.shape
    nblk = tt // tb
    row_spec = lambda w: pl.BlockSpec((None, tb, w), lambda b, i: (b, i, 0))
    names = ["r_k", "lnx_g", "lnx_b", "w_out", "norm2_g", "w_ff1", "w_ff2", "normf_g", "bd"]
    w_in = [wts[n] for n in names]
    acts = [x, yconv, y, r, k, v, g]
    return pl.pallas_call(
        _back_kernel,
        grid=(ns, nblk),
        in_specs=[row_spec(a.shape[-1]) for a in acts] + [_const_spec(w.shape) for w in w_in],
        out_specs=row_spec(d),
        out_shape=jax.ShapeDtypeStruct((ns, tt, d), F32),
        compiler_params=pltpu.CompilerParams(
            dimension_semantics=("arbitrary", "arbitrary"), vmem_limit_bytes=VMEM_LIMIT),
        name="back",
    )(*acts, *w_in)


def _mm(a, b):
    return jnp.dot(a.astype(BF16), b.astype(BF16), preferred_element_type=F32)


def _mm_nt(a, b):
    return lax.dot_general(a.astype(BF16), b.astype(BF16), (((1,), (1,)), ((), ())),
                           preferred_element_type=F32)


def _cumsum_rows(x):
    rows = x.shape[0]
    row = lax.broadcasted_iota(jnp.int32, x.shape, 0)
    sh = 1
    while sh < rows:
        x = x + jnp.where(row >= sh, pltpu.roll(x, sh, 0), 0.0)
        sh *= 2
    return x


def _scan_masks(c):
    lanes = GROUP * HEAD
    gc = GROUP * c
    i0 = lambda shape: lax.broadcasted_iota(jnp.int32, shape, 0)
    i1 = lambda shape: lax.broadcasted_iota(jnp.int32, shape, 1)
    col_t = i1((c, gc)) % c
    row_t = i0((c, gc))
    level = []
    s = 1
    while s < c:
        level.append((row_t // (2 * s) == col_t // (2 * s)) & (row_t % (2 * s) >= s)
                     & (col_t % (2 * s) < s))
        s *= 2
    return dict(
        level=level,
        lane_head=(i0((gc, lanes)) // c) == (i1((gc, lanes)) // HEAD),
        strict=col_t < i0((c, gc)),
        incl=col_t <= i0((c, gc)),
        bd_cc=(i0((gc, gc)) // c) == (i1((gc, gc)) // c),
        bd_ll=(i0((lanes, lanes)) // HEAD) == (i1((lanes, lanes)) // HEAD),
    )


def _chunk_precompute(load, masks):
    lw, kk, a, r, k, v = load()
    c = lw.shape[0]
    assert c == HEAD, "side-by-side forms share lane offsets only when CHUNK == HEAD"

    def bd_rows(m):
        return jnp.where(masks["lane_head"], jnp.concatenate([m] * GROUP, axis=0), 0.0)

    def bd_mat(m):
        return jnp.where(masks["bd_cc"], jnp.concatenate([m] * GROUP, axis=0), 0.0)

    def heads_t(m):
        mt = jnp.transpose(m)
        return jnp.concatenate([mt[h * HEAD:(h + 1) * HEAD, :] for h in range(GROUP)], axis=1)

    cum = _cumsum_rows(lw)
    cum_last = cum[c - 1:c, :]
    beta = kk * a
    alpha_t = -kk * jnp.exp(cum - lw)
    r_t = r * jnp.exp(cum)
    inv = jnp.exp(-cum)
    rem = jnp.exp(cum_last - cum)
    ar = jnp.concatenate([alpha_t, r_t], axis=0)
    g_b = _mm_nt(ar, bd_rows(beta * inv))
    g_k = _mm_nt(ar, bd_rows(k * inv))
    yield
    a_ab = jnp.where(masks["strict"], g_b[:c], 0.0)
    a_rb = jnp.where(masks["incl"], g_b[c:], 0.0)
    a_ak = jnp.where(masks["strict"], g_k[:c], 0.0)
    a_rk = jnp.where(masks["incl"], g_k[c:], 0.0)

    b_t = heads_t(beta * rem)
    k_t = heads_t(k * rem)
    res = _mm(jnp.concatenate([a_ak, a_rk, k_t], axis=0), bd_rows(v))
    yield
    akv, rkv, nh_k = res[:c], res[c:2 * c], res[2 * c:]

    tm = jnp.where(masks["level"][0], a_ab, 0.0)
    for lvl in range(1, len(masks["level"])):
        e_s = jnp.where(masks["level"][lvl], a_ab, 0.0)
        y_s = e_s + _mm(tm, bd_mat(e_s))
        yield
        upd = _mm(y_s, bd_mat(tm))
        yield
        tm = tm + y_s + upd

    wa = alpha_t + _mm(tm, bd_rows(alpha_t))
    uv = akv + _mm(tm, bd_rows(akv))
    yield
    resid = akv - uv + _mm(a_ab, bd_rows(uv))
    yield
    uv = uv + resid + _mm(tm, bd_rows(resid))
    yield
    lhs = jnp.concatenate([a_rb, b_t], axis=0)
    res_w = _mm(lhs, bd_rows(wa))
    res_u = _mm(lhs, bd_rows(uv))
    p_col = jnp.transpose(jnp.broadcast_to(jnp.exp(cum_last), (8, cum.shape[1])))[:, 0:1]
    ph = jnp.concatenate([jnp.broadcast_to(p_col[h * HEAD:(h + 1) * HEAD, :], (HEAD, c))
                          for h in range(GROUP)], axis=1)
    yield
    rq = r_t + res_w[:c]
    yv = rkv + res_u[:c]
    mh = res_w[c:]
    nh = res_u[c:] + nh_k
    return rq, yv, mh, nh, ph


def _run_staggered(gens_by_chunk, skew, on_chunk_done, side=()):
    n = len(gens_by_chunk)
    results = [[None] * len(gens) for gens in gens_by_chunk]
    live = {(ci, j) for ci, gens in enumerate(gens_by_chunk) for j in range(len(gens))}
    side = list(side)
    n_done = 0
    rnd = 0
    while live or side:
        for gen in list(side):
            if next(gen, side) is side:
                side.remove(gen)
        for ci in range(min(n, rnd // skew + 1)):
            for j, gen in enumerate(gens_by_chunk[ci]):
                if (ci, j) in live:
                    try:
                        next(gen)
                    except StopIteration as stop:
                        results[ci][j] = stop.value
                        live.discard((ci, j))
        while n_done < n and not any(ci == n_done for ci, _ in live):
            on_chunk_done(n_done, results[n_done])
            n_done += 1
        rnd += 1


SCAN_OPERANDS = ("lw", "kk", "a", "r", "k", "v")


def _front_scan_kernel(blocks_per_seq, x_ref, *refs):
    nw = len(FRONT_WEIGHTS)
    w = dict(zip(FRONT_WEIGHTS, refs[:nw]))
    (yconv_ref, r_ref, k_ref, v_ref, g_ref, y_ref, nconv_ref, nshift_ref, zout_ref,
     stage_ref, z_ref, cu_ref, cp_ref) = refs[nw:]
    j = pl.program_id(0)
    last_front = pl.num_programs(0) - 2

    @pl.when(j == 0)
    def _():
        stage_ref[...] = jnp.zeros_like(stage_ref)
        z_ref[...] = jnp.zeros_like(z_ref)
        cu_ref[...] = jnp.zeros_like(cu_ref)
        cp_ref[...] = jnp.zeros_like(cp_ref)

    wpar = j % 2
    rpar = 1 - wpar
    keep_f = jnp.where(jnp.minimum(j, last_front) % blocks_per_seq == 0, 0.0, 1.0)
    keep_s = jnp.where((j + blocks_per_seq - 1) % blocks_per_seq == 0, 0.0, 1.0)

    outs = dict(yconv=yconv_ref, r=r_ref, k=k_ref, v=v_ref, g=g_ref)
    slot = {name: i for i, name in enumerate(SCAN_OPERANDS)}
    carry = {}

    def store(name, value):
        if name in outs:
            outs[name][...] = value
        if name in slot:
            stage_ref[wpar, slot[name]] = value
        carry[name] = value

    cu_old = cu_ref[...]
    cp_old = cp_ref[...]
    front = _front_stages(x_ref[...], cu_old * keep_f, cp_old * keep_f, 1, w, store)

    lanes = GROUP * HEAD
    n_groups = stage_ref.shape[3] // lanes
    n_chunks = stage_ref.shape[2] // CHUNK
    masks = _scan_masks(CHUNK)

    def loader(ci, g):
        rows = slice(ci * CHUNK, (ci + 1) * CHUNK)
        sl = slice(g * lanes, (g + 1) * lanes)
        return lambda: tuple(stage_ref[rpar, i, rows, sl] for i in range(len(SCAN_OPERANDS)))

    gens = [[_chunk_precompute(loader(ci, g), masks) for g in range(n_groups)]
            for ci in range(n_chunks)]
    zs = [z_ref[g] * keep_s for g in range(n_groups)]

    def apply_chunk(ci, pre):
        res = [_mm(jnp.concatenate([pre[g][0], pre[g][2]], axis=0),
                   jnp.where(masks["bd_ll"], jnp.concatenate([zs[g]] * GROUP, axis=0), 0.0))
               for g in range(n_groups)]
        for g in range(n_groups):
            _, yv, _, nh, ph = pre[g]
            y_ref[ci * CHUNK:(ci + 1) * CHUNK, g * lanes:(g + 1) * lanes] = res[g][:CHUNK] + yv
            zs[g] = ph * zs[g] + res[g][CHUNK:] + nh

    _run_staggered(gens, SCAN_SKEW, apply_chunk, side=[front])
    for g in range(n_groups):
        z_ref[g] = zs[g]
        zout_ref[g] = zs[g]
    nconv_ref[...] = carry["new_cu"]
    nshift_ref[...] = carry["new_cp"]
    hold = j >= last_front
    cu_ref[...] = jnp.where(hold, cu_old, carry["new_cu"])
    cp_ref[...] = jnp.where(hold, cp_old, carry["new_cp"])


def _front_scan(x, tb, wts):
    bp, seq, d = x.shape
    n = bp * seq
    nb = n // tb
    bps = seq // tb
    cw_dim = wts["conv_w"].shape[-1]
    rw_dim = wts["w0"].shape[-1]
    rcols = wts["mu"].shape[-1]
    lanes = GROUP * HEAD
    ng = rw_dim // lanes
    cur = lambda j: jnp.minimum(j, nb - 1)
    prv = lambda j: jnp.maximum(j - 1, 0)
    row_cur = lambda width: pl.BlockSpec((tb, width), lambda j: (cur(j), 0))
    w_in = [wts[name] for name in FRONT_WEIGHTS]
    out_shape = ([jax.ShapeDtypeStruct((n, cw_dim), BF16)]
                 + [jax.ShapeDtypeStruct((n, rw_dim), F32)] * 5
                 + [jax.ShapeDtypeStruct((bp, 2, cw_dim), F32),
                    jax.ShapeDtypeStruct((bp, 1, rcols), F32),
                    jax.ShapeDtypeStruct((bp, ng, HEAD, lanes), F32)])
    out_specs = ([row_cur(cw_dim)] + [row_cur(rw_dim)] * 4
                 + [pl.BlockSpec((tb, rw_dim), lambda j: (prv(j), 0)),
                    pl.BlockSpec((None, 2, cw_dim), lambda j: (cur(j) // bps, 0, 0)),
                    pl.BlockSpec((None, 1, rcols), lambda j: (cur(j) // bps, 0, 0)),
                    pl.BlockSpec((None, ng, HEAD, lanes), lambda j: (prv(j) // bps, 0, 0, 0))])
    outs = pl.pallas_call(
        functools.partial(_front_scan_kernel, bps),
        grid=(nb + 1,),
        in_specs=[row_cur(d)] + [_const_spec(wt.shape) for wt in w_in],
        out_specs=out_specs,
        out_shape=out_shape,
        scratch_shapes=[pltpu.VMEM((2, len(SCAN_OPERANDS), tb, rw_dim), F32),
                        pltpu.VMEM((ng, HEAD, lanes), F32),
                        pltpu.VMEM((2, cw_dim), F32), pltpu.VMEM((1, rcols), F32)],
        compiler_params=pltpu.CompilerParams(
            dimension_semantics=("arbitrary",), vmem_limit_bytes=VMEM_LIMIT),
        name="front_scan",
    )(x.reshape(n, d), *w_in)
    yconv, r, k, v, g, y, nconv, nshift, z = outs
    per_seq = lambda t: t.reshape(bp, seq, t.shape[-1])
    return [per_seq(t) for t in (yconv, r, k, v, g, y)] + [nconv, nshift, z]


LANE_SLAB = 128


def _sscan_kernel(kk_ref, a_ref, lw_ref, k_ref, r_ref, v_ref, s_in_ref, y_ref, s_out_ref,
                  fm_ref, yt_ref):
    n_t = kk_ref.shape[0]
    heads, n_k = s_in_ref.shape[:2]
    shape = s_in_ref.shape[2:]
    i_kk, i_kka, i_w, i_k, i_r, i_v = range(6)
    for t in range(n_t):
        kk_t = jnp.transpose(kk_ref[t])
        fm_ref[i_kk, t] = kk_t
        fm_ref[i_kka, t] = kk_t * jnp.transpose(a_ref[t])
        fm_ref[i_w, t] = jnp.exp(jnp.transpose(lw_ref[t]))
        fm_ref[i_k, t] = jnp.transpose(k_ref[t])
        fm_ref[i_r, t] = jnp.transpose(r_ref[t])
        fm_ref[i_v, t] = jnp.transpose(v_ref[t])

    zeros = jnp.zeros(shape, F32)
    for hd in range(heads):
        off = hd * HEAD

        def rows8(i, t, k8, off=off):
            return fm_ref[i, t, pl.ds(pl.multiple_of(off + k8 * 8, 8), 8), :]

        def first_sk(k8, acc, hd=hd, rows8=rows8):
            kk8 = rows8(i_kk, 0, k8)
            for j in range(8):
                acc = acc + s_in_ref[hd, k8 * 8 + j] * kk8[j:j + 1, :]
            return acc

        sk = lax.fori_loop(0, n_k // 8, first_sk, zeros)
        for t in range(n_t):
            src = s_in_ref if t == 0 else s_out_ref
            vt = fm_ref[i_v, t, off:off + HEAD, :]
            last = t == n_t - 1

            def step(k8, carry, hd=hd, rows8=rows8, t=t, src=src, vt=vt, sk=sk, last=last):
                acc, sk_next = carry
                kka8 = rows8(i_kka, t, k8)
                w8 = rows8(i_w, t, k8)
                k8v = rows8(i_k, t, k8)
                r8 = rows8(i_r, t, k8)
                kkn8 = None if last else rows8(i_kk, t + 1, k8)
                for j in range(8):
                    snew = (src[hd, k8 * 8 + j] * w8[j:j + 1, :] - sk * kka8[j:j + 1, :]
                            + vt * k8v[j:j + 1, :])
                    s_out_ref[hd, k8 * 8 + j] = snew
                    acc = acc + snew * r8[j:j + 1, :]
                    if not last:
                        sk_next = sk_next + snew * kkn8[j:j + 1, :]
                return acc, sk_next

            y, sk = lax.fori_loop(0, n_k // 8, step, (zeros, zeros))
            yt_ref[t, off:off + HEAD, :] = y
    for t in range(n_t):
        y_ref[t] = jnp.transpose(yt_ref[t])


def _sample_scan(kk, a, lw, k, r, v, state):
    n_t, nb, w = kk.shape
    heads = LANE_SLAB // HEAD
    spec = pl.BlockSpec((n_t, nb, LANE_SLAB), lambda i: (0, 0, i))
    sspec = pl.BlockSpec((heads, HEAD, HEAD, nb), lambda i: (i, 0, 0, 0))
    return pl.pallas_call(
        _sscan_kernel,
        grid=(w // LANE_SLAB,),
        in_specs=[spec] * 6 + [sspec],
        out_specs=[spec, sspec],
        out_shape=[jax.ShapeDtypeStruct((n_t, nb, w), F32),
                   jax.ShapeDtypeStruct(state.shape, F32)],
        scratch_shapes=[pltpu.VMEM((6, n_t, LANE_SLAB, nb), F32),
                        pltpu.VMEM((n_t, LANE_SLAB, nb), F32)],
        compiler_params=pltpu.CompilerParams(
            dimension_semantics=("arbitrary",), vmem_limit_bytes=VMEM_LIMIT),
        name="sample_scan",
    )(kk, a, lw, k, r, v, state)


def kernel(x_prompt, x_sample, state_conv, state_shift, state_wkv, norm1_g, w_in, conv_w, mu, w0,
           w2, a0, a2, g2, k_k, k_a, r_k, lnx_g, lnx_b, w_out, norm2_g, w_ff1, w_ff2, normf_g):
    depth = w_in.shape[0]
    assert depth == 1, "single-layer step only"
    bp, seq, d = x_prompt.shape
    bs, dseq, _ = x_sample.shape
    cw_dim = conv_w.shape[-1]
    rw_dim = w0.shape[-1]
    rcols = mu.shape[-1]
    nh = rw_dim // HEAD
    row = lambda t: t[0].reshape(1, -1)
    seg = jnp.arange(GROUP * HEAD) // HEAD
    wts = {
        "norm1_g": row(norm1_g), "w_in": w_in[0].astype(BF16), "conv_w": conv_w[0],
        "mu": row(mu), "w0": row(w0), "w2": w2[0].astype(BF16), "a0": row(a0),
        "a2": a2[0].astype(BF16), "g2": g2[0].astype(BF16), "k_k": row(k_k), "k_a": row(k_a),
        "r_k": row(r_k), "lnx_g": row(lnx_g), "lnx_b": row(lnx_b),
        "w_out": w_out[0].astype(BF16), "norm2_g": row(norm2_g),
        "w_ff1": w_ff1[0].astype(BF16), "w_ff2": w_ff2[0].astype(BF16),
        "normf_g": normf_g.reshape(1, -1),
        "bd": (seg[:, None] == seg[None, :]).astype(BF16),
    }

    yconv, r, k, v, g, y, nconv_p, nshift_p, z = _front_scan(x_prompt, FRONT_SCAN_ROWS, wts)
    y_prompt = _back(x_prompt, yconv, y, r, k, v, g, BACK_ROWS, wts)
    ng = rw_dim // (GROUP * HEAD)
    zb = z.reshape(bp, ng, HEAD, GROUP, HEAD)
    wkv_p = jnp.transpose(zb, (0, 1, 3, 4, 2)).reshape(bp, nh, HEAD, HEAD)

    xs = jnp.transpose(x_sample, (1, 0, 2)).reshape(1, dseq * bs, d)
    c0 = jnp.transpose(state_conv[0], (1, 0, 2)).reshape(1, 2 * bs, cw_dim)
    s0 = state_shift[0].reshape(1, bs, rcols)
    tbs = 2 * bs
    yconv, r, k, v, kk, a, lw, g, nconv_s, nshift_s = _front(xs, c0, s0, bs, tbs, wts)
    tm = lambda t: t.reshape(dseq, bs, rw_dim)
    st = jnp.transpose(state_wkv[0], (1, 3, 2, 0))
    y_tm, st_new = _sample_scan(tm(kk), tm(a), tm(lw), tm(k), tm(r), tm(v), st)
    y = y_tm.reshape(1, dseq * bs, rw_dim)
    ys = _back(xs, yconv, y, r, k, v, g, tbs, wts)
    y_sample = jnp.transpose(ys.reshape(dseq, bs, d), (1, 0, 2))
    wkv_s = jnp.transpose(st_new, (3, 0, 2, 1))
    nconv_s = jnp.transpose(nconv_s.reshape(2, bs, cw_dim), (1, 0, 2))

    return (y_prompt, y_sample,
            nconv_p[None], nshift_p.reshape(1, bp, rcols), wkv_p[None],
            nconv_s[None], nshift_s.reshape(1, bs, rcols), wkv_s[None])
```

```python
import functools

import jax
import jax.numpy as jnp
from jax import lax
from jax.experimental import pallas as pl
from jax.experimental.pallas import tpu as pltpu

F32 = jnp.float32
BF16 = jnp.bfloat16

HEAD = 64
RMS_EPS = 1e-6
GN_EPS = 64e-5
NORM_EPS = 1e-12
VMEM_LIMIT = 56 * 1024 * 1024
CHUNK = 64
GROUP = 4
SCAN_SKEW = 1
FRONT_SCAN_ROWS = 256
BACK_ROWS = 512


def _dot(a, b):
    return jnp.dot(a, b, preferred_element_type=F32)


def _rmsnorm(x, g):
    return x * lax.rsqrt(jnp.mean(x * x, axis=-1, keepdims=True) + RMS_EPS) * g


def _seg_sum(x, bd):
    lanes = bd.shape[0]
    xb = x.astype(BF16)
    return jnp.concatenate([_dot(xb[:, i:i + lanes], bd) for i in range(0, x.shape[1], lanes)],
                           axis=1)


def _shift_rows(x, k, head):
    rows = x.shape[0]
    if k == rows:
        return head
    if k % 8 == 0:
        return jnp.concatenate([head, x[:rows - k]], axis=0)
    rolled = pltpu.roll(x, k, 0)
    row = lax.broadcasted_iota(jnp.int32, x.shape, 0)
    out = rolled
    for j in range(k):
        out = jnp.where(row == j, head[j:j + 1, :], out)
    return out


FRONT_WEIGHTS = ("norm1_g", "w_in", "conv_w", "mu", "w0", "w2", "a0", "a2", "g2", "k_k", "k_a",
                 "bd")
PROJ_PIECE = 512


def _front_stages(x, cu, cp, s, w, store):
    rows = x.shape[0]
    cw_dim = cu.shape[1]
    rw_dim = w["w0"].shape[1]
    win = w["w_in"]
    ncols = win.shape[1]
    h = _rmsnorm(x, w["norm1_g"][...]).astype(BF16)
    pieces = []
    for c0 in range(0, ncols, PROJ_PIECE):
        pieces.append(_dot(h, win[:, c0:min(c0 + PROJ_PIECE, ncols)]))
        yield
    proj = jnp.concatenate(pieces, axis=1)
    c_b = proj[:, :cw_dim]
    c_c = proj[:, cw_dim:2 * cw_dim]
    c_x = proj[:, 2 * cw_dim:3 * cw_dim]
    p = proj[:, 3 * cw_dim:]

    u = c_c * c_x
    u1 = _shift_rows(u, s, cu[s:2 * s])
    u2 = _shift_rows(u, 2 * s, cu)
    cw = w["conv_w"][...]
    conv = cw[0:1] * u2 + cw[1:2] * u1 + cw[2:3] * u
    store("yconv", (c_b * conv).astype(BF16))
    store("new_cu", u[rows - 2 * s:])
    yield

    prev = _shift_rows(p, s, cp)
    store("new_cp", p[rows - s:])
    xm = p + (prev - p) * w["mu"][...]
    r = xm[:, :rw_dim]
    k = xm[:, rw_dim:2 * rw_dim]
    v = xm[:, 2 * rw_dim:3 * rw_dim]
    store("r", r)
    store("v", v)
    o = 3 * rw_dim
    d_w = w["w2"].shape[0]
    d_a = w["a2"].shape[0]
    pw = xm[:, o:o + d_w]
    pa = xm[:, o + d_w:o + d_w + d_a]
    pg = xm[:, o + d_w + d_a:]
    yield

    wl = w["w0"][...] + _dot(jnp.tanh(pw).astype(BF16), w["w2"][...])
    z = -wl
    softplus = jnp.maximum(z, 0.0) + jnp.log(1.0 + jnp.exp(-jnp.abs(z)))
    store("lw", -jnp.exp(-softplus - 0.5))
    yield
    a = 1.0 / (1.0 + jnp.exp(-(w["a0"][...] + _dot(pa.astype(BF16), w["a2"][...]))))
    store("a", a)
    store("g", _dot((1.0 / (1.0 + jnp.exp(-pg))).astype(BF16), w["g2"][...]))
    yield
    kk = k * w["k_k"][...]
    nrm = jnp.sqrt(_seg_sum(kk * kk, w["bd"][...]))
    store("kk", kk / jnp.maximum(nrm, NORM_EPS))
    store("k", k * (1.0 + (a - 1.0) * w["k_a"][...]))
    yield


FRONT_OUTPUTS = ("yconv", "r", "k", "v", "kk", "a", "lw", "g", "new_cu", "new_cp")


def _front_kernel(s, x_ref, conv0_ref, shift0_ref, *refs):
    nw = len(FRONT_WEIGHTS)
    w = dict(zip(FRONT_WEIGHTS, refs[:nw]))
    outs = dict(zip(FRONT_OUTPUTS, refs[nw:nw + len(FRONT_OUTPUTS)]))
    cu_ref, cp_ref = refs[nw + len(FRONT_OUTPUTS):]

    @pl.when(pl.program_id(1) == 0)
    def _():
        cu_ref[...] = conv0_ref[...]
        cp_ref[...] = shift0_ref[...]

    carry = {}

    def store(name, value):
        outs[name][...] = value
        carry[name] = value

    for _ in _front_stages(x_ref[...], cu_ref[...], cp_ref[...], s, w, store):
        pass
    cu_ref[...] = carry["new_cu"]
    cp_ref[...] = carry["new_cp"]


def _const_spec(shape):
    nd = len(shape)
    return pl.BlockSpec(shape, lambda *_: (0,) * nd, pipeline_mode=pl.Buffered(1))


def _front(x, conv0, shift0, s, tb, wts):
    ns, tt, d = x.shape
    cw_dim = conv0.shape[-1]
    rw_dim = wts["w0"].shape[-1]
    rcols = shift0.shape[-1]
    nblk = tt // tb
    row_spec = lambda w: pl.BlockSpec((None, tb, w), lambda b, i: (b, i, 0))
    st_spec = lambda r, w: pl.BlockSpec((None, r, w), lambda b, i: (b, 0, 0))
    w_in = [wts[n] for n in FRONT_WEIGHTS]
    out_shapes = ([jax.ShapeDtypeStruct((ns, tt, cw_dim), BF16)]
                  + [jax.ShapeDtypeStruct((ns, tt, rw_dim), F32)] * 7
                  + [jax.ShapeDtypeStruct((ns, 2 * s, cw_dim), F32),
                     jax.ShapeDtypeStruct((ns, s, rcols), F32)])
    out_specs = ([row_spec(cw_dim)] + [row_spec(rw_dim)] * 7
                 + [st_spec(2 * s, cw_dim), st_spec(s, rcols)])
    return pl.pallas_call(
        functools.partial(_front_kernel, s),
        grid=(ns, nblk),
        in_specs=[row_spec(d), st_spec(2 * s, cw_dim), st_spec(s, rcols)]
        + [_const_spec(w.shape) for w in w_in],
        out_specs=out_specs,
        out_shape=out_shapes,
        scratch_shapes=[pltpu.VMEM((2 * s, cw_dim), F32), pltpu.VMEM((s, rcols), F32)],
        compiler_params=pltpu.CompilerParams(
            dimension_semantics=("arbitrary", "arbitrary"), vmem_limit_bytes=VMEM_LIMIT),
        name="front",
    )(x, conv0, shift0, *w_in)


def _back_kernel(x_ref, yconv_ref, y_ref, r_ref, k_ref, v_ref, g_ref,
                 rk_ref, lg_ref, lb_ref, wo_ref, n2_ref, f1_ref, f2_ref, nf_ref, bd_ref,
                 out_ref):
    bd = bd_ref[...]
    y = y_ref[...]
    inv_n = 1.0 / HEAD
    mean = _seg_sum(y, bd) * inv_n
    dlt = y - mean
    var = _seg_sum(dlt * dlt, bd) * inv_n
    yn = dlt * lax.rsqrt(var + GN_EPS) * lg_ref[...] + lb_ref[...]
    v = v_ref[...]
    bonus = _seg_sum(r_ref[...] * k_ref[...] * rk_ref[...], bd) * v
    y_rwkv = ((yn + bonus) * g_ref[...]).astype(BF16)
    cw_dim = yconv_ref.shape[-1]
    x = x_ref[...]
    x1 = x + _dot(yconv_ref[...], wo_ref[:cw_dim, :]) + _dot(y_rwkv, wo_ref[cw_dim:, :])
    h2 = _rmsnorm(x1, n2_ref[...]).astype(BF16)
    f = jnp.maximum(_dot(h2, f1_ref[...]), 0.0)
    x2 = x1 + _dot((f * f).astype(BF16), f2_ref[...])
    out_ref[...] = _rmsnorm(x2, nf_ref[...])


def _back(x, yconv, y, r, k, v, g, tb, wts):
    ns, tt, d = x.shape
    nblk = tt // tb
    row_spec = lambda w: pl.BlockSpec((None, tb, w), lambda b, i: (b, i, 0))
    names = ["r_k", "lnx_g", "lnx_b", "w_out", "norm2_g", "w_ff1", "w_ff2", "normf_g", "bd"]
    w_in = [wts[n] for n in names]
    acts = [x, yconv, y, r, k, v, g]
    return pl.pallas_call(
        _back_kernel,
        grid=(ns, nblk),
        in_specs=[row_spec(a.shape[-1]) for a in acts] + [_const_spec(w.shape) for w in w_in],
        out_specs=row_spec(d),
        out_shape=jax.ShapeDtypeStruct((ns, tt, d), F32),
        compiler_params=pltpu.CompilerParams(
            dimension_semantics=("arbitrary", "arbitrary"), vmem_limit_bytes=VMEM_LIMIT),
        name="back",
    )(*acts, *w_in)


def _mm(a, b):
    return jnp.dot(a.astype(BF16), b.astype(BF16), preferred_element_type=F32)


def _mm_nt(a, b):
    return lax.dot_general(a.astype(BF16), b.astype(BF16), (((1,), (1,)), ((), ())),
                           preferred_element_type=F32)


def _cumsum_rows(x):
    rows = x.shape[0]
    row = lax.broadcasted_iota(jnp.int32, x.shape, 0)
    sh = 1
    while sh < rows:
        x = x + jnp.where(row >= sh, pltpu.roll(x, sh, 0), 0.0)
        sh *= 2
    return x


def _scan_masks(c):
    lanes = GROUP * HEAD
    gc = GROUP * c
    i0 = lambda shape: lax.broadcasted_iota(jnp.int32, shape, 0)
    i1 = lambda shape: lax.broadcasted_iota(jnp.int32, shape, 1)
    col_t = i1((c, gc)) % c
    row_t = i0((c, gc))
    level = []
    s = 1
    while s < c:
        level.append((row_t // (2 * s) == col_t // (2 * s)) & (row_t % (2 * s) >= s)
                     & (col_t % (2 * s) < s))
        s *= 2
    return dict(
        level=level,
        lane_head=(i0((gc, lanes)) // c) == (i1((gc, lanes)) // HEAD),
        strict=col_t < i0((c, gc)),
        incl=col_t <= i0((c, gc)),
        bd_cc=(i0((gc, gc)) // c) == (i1((gc, gc)) // c),
        bd_ll=(i0((lanes, lanes)) // HEAD) == (i1((lanes, lanes)) // HEAD),
    )


def _chunk_precompute(load, masks):
    lw, kk, a, r, k, v = load()
    c = lw.shape[0]
    assert c == HEAD, "side-by-side forms share lane offsets only when CHUNK == HEAD"

    def bd_rows(m):
        return jnp.where(masks["lane_head"], jnp.concatenate([m] * GROUP, axis=0), 0.0)

    def bd_mat(m):
        return jnp.where(masks["bd_cc"], jnp.concatenate([m] * GROUP, axis=0), 0.0)

    def heads_t(m):
        mt = jnp.transpose(m)
        return jnp.concatenate([mt[h * HEAD:(h + 1) * HEAD, :] for h in range(GROUP)], axis=1)

    cum = _cumsum_rows(lw)
    cum_last = cum[c - 1:c, :]
    beta = kk * a
    alpha_t = -kk * jnp.exp(cum - lw)
    r_t = r * jnp.exp(cum)
    inv = jnp.exp(-cum)
    rem = jnp.exp(cum_last - cum)
    ar = jnp.concatenate([alpha_t, r_t], axis=0)
    g_b = _mm_nt(ar, bd_rows(beta * inv))
    g_k = _mm_nt(ar, bd_rows(k * inv))
    yield
    a_ab = jnp.where(masks["strict"], g_b[:c], 0.0)
    a_rb = jnp.where(masks["incl"], g_b[c:], 0.0)
    a_ak = jnp.where(masks["strict"], g_k[:c], 0.0)
    a_rk = jnp.where(masks["incl"], g_k[c:], 0.0)

    b_t = heads_t(beta * rem)
    k_t = heads_t(k * rem)
    res = _mm(jnp.concatenate([a_ak, a_rk, k_t], axis=0), bd_rows(v))
    yield
    akv, rkv, nh_k = res[:c], res[c:2 * c], res[2 * c:]

    tm = jnp.where(masks["level"][0], a_ab, 0.0)
    for lvl in range(1, len(masks["level"])):
        e_s = jnp.where(masks["level"][lvl], a_ab, 0.0)
        y_s = e_s + _mm(tm, bd_mat(e_s))
        yield
        upd = _mm(y_s, bd_mat(tm))
        yield
        tm = tm + y_s + upd

    wa = alpha_t + _mm(tm, bd_rows(alpha_t))
    uv = akv + _mm(tm, bd_rows(akv))
    yield
    resid = akv - uv + _mm(a_ab, bd_rows(uv))
    yield
    uv = uv + resid + _mm(tm, bd_rows(resid))
    yield
    lhs = jnp.concatenate([a_rb, b_t], axis=0)
    res_w = _mm(lhs, bd_rows(wa))
    res_u = _mm(lhs, bd_rows(uv))
    p_col = jnp.transpose(jnp.broadcast_to(jnp.exp(cum_last), (8, cum.shape[1])))[:, 0:1]
    ph = jnp.concatenate([jnp.broadcast_to(p_col[h * HEAD:(h + 1) * HEAD, :], (HEAD, c))
                          for h in range(GROUP)], axis=1)
    yield
    rq = r_t + res_w[:c]
    yv = rkv + res_u[:c]
    mh = res_w[c:]
    nh = res_u[c:] + nh_k
    return rq, yv, mh, nh, ph


def _run_staggered(gens_by_chunk, skew, on_chunk_done, side=()):
    n = len(gens_by_chunk)
    results = [[None] * len(gens) for gens in gens_by_chunk]
    live = {(ci, j) for ci, gens in enumerate(gens_by_chunk) for j in range(len(gens))}
    side = list(side)
    n_done = 0
    rnd = 0
    while live or side:
        for gen in list(side):
            if next(gen, side) is side:
                side.remove(gen)
        for ci in range(min(n, rnd // skew + 1)):
            for j, gen in enumerate(gens_by_chunk[ci]):
                if (ci, j) in live:
                    try:
                        next(gen)
                    except StopIteration as stop:
                        results[ci][j] = stop.value
                        live.discard((ci, j))
        while n_done < n and not any(ci == n_done for ci, _ in live):
            on_chunk_done(n_done, results[n_done])
            n_done += 1
        rnd += 1


SCAN_OPERANDS = ("lw", "kk", "a", "r", "k", "v")


def _front_scan_kernel(blocks_per_seq, x_ref, *refs):
    nw = len(FRONT_WEIGHTS)
    w = dict(zip(FRONT_WEIGHTS, refs[:nw]))
    (yconv_ref, r_ref, k_ref, v_ref, g_ref, y_ref, nconv_ref, nshift_ref, zout_ref,
     stage_ref, z_ref, cu_ref, cp_ref) = refs[nw:]
    j = pl.program_id(0)
    last_front = pl.num_programs(0) - 2

    @pl.when(j == 0)
    def _():
        stage_ref[...] = jnp.zeros_like(stage_ref)
        z_ref[...] = jnp.zeros_like(z_ref)
        cu_ref[...] = jnp.zeros_like(cu_ref)
        cp_ref[...] = jnp.zeros_like(cp_ref)

    wpar = j % 2
    rpar = 1 - wpar
    keep_f = jnp.where(jnp.minimum(j, last_front) % blocks_per_seq == 0, 0.0, 1.0)
    keep_s = jnp.where((j + blocks_per_seq - 1) % blocks_per_seq == 0, 0.0, 1.0)

    outs = dict(yconv=yconv_ref, r=r_ref, k=k_ref, v=v_ref, g=g_ref)
    slot = {name: i for i, name in enumerate(SCAN_OPERANDS)}
    carry = {}

    def store(name, value):
        if name in outs:
            outs[name][...] = value
        if name in slot:
            stage_ref[wpar, slot[name]] = value
        carry[name] = value

    cu_old = cu_ref[...]
    cp_old = cp_ref[...]
    front = _front_stages(x_ref[...], cu_old * keep_f, cp_old * keep_f, 1, w, store)

    lanes = GROUP * HEAD
    n_groups = stage_ref.shape[3] // lanes
    n_chunks = stage_ref.shape[2] // CHUNK
    masks = _scan_masks(CHUNK)

    def loader(ci, g):
        rows = slice(ci * CHUNK, (ci + 1) * CHUNK)
        sl = slice(g * lanes, (g + 1) * lanes)
        return lambda: tuple(stage_ref[rpar, i, rows, sl] for i in range(len(SCAN_OPERANDS)))

    gens = [[_chunk_precompute(loader(ci, g), masks) for g in range(n_groups)]
            for ci in range(n_chunks)]
    zs = [z_ref[g] * keep_s for g in range(n_groups)]

    def apply_chunk(ci, pre):
        res = [_mm(jnp.concatenate([pre[g][0], pre[g][2]], axis=0),
                   jnp.where(masks["bd_ll"], jnp.concatenate([zs[g]] * GROUP, axis=0), 0.0))
               for g in range(n_groups)]
        for g in range(n_groups):
            _, yv, _, nh, ph = pre[g]
            y_ref[ci * CHUNK:(ci + 1) * CHUNK, g * lanes:(g + 1) * lanes] = res[g][:CHUNK] + yv
            zs[g] = ph * zs[g] + res[g][CHUNK:] + nh

    _run_staggered(gens, SCAN_SKEW, apply_chunk, side=[front])
    for g in range(n_groups):
        z_ref[g] = zs[g]
        zout_ref[g] = zs[g]
    nconv_ref[...] = carry["new_cu"]
    nshift_ref[...] = carry["new_cp"]
    hold = j >= last_front
    cu_ref[...] = jnp.where(hold, cu_old, carry["new_cu"])
    cp_ref[...] = jnp.where(hold, cp_old, carry["new_cp"])


def _front_scan(x, tb, wts):
    bp, seq, d = x.shape
    n = bp * seq
    nb = n // tb
    bps = seq // tb
    cw_dim = wts["conv_w"].shape[-1]
    rw_dim = wts["w0"].shape[-1]
    rcols = wts["mu"].shape[-1]
    lanes = GROUP * HEAD
    ng = rw_dim // lanes
    cur = lambda j: jnp.minimum(j, nb - 1)
    prv = lambda j: jnp.maximum(j - 1, 0)
    row_cur = lambda width: pl.BlockSpec((tb, width), lambda j: (cur(j), 0))
    w_in = [wts[name] for name in FRONT_WEIGHTS]
    out_shape = ([jax.ShapeDtypeStruct((n, cw_dim), BF16)]
                 + [jax.ShapeDtypeStruct((n, rw_dim), F32)] * 5
                 + [jax.ShapeDtypeStruct((bp, 2, cw_dim), F32),
                    jax.ShapeDtypeStruct((bp, 1, rcols), F32),
                    jax.ShapeDtypeStruct((bp, ng, HEAD, lanes), F32)])
    out_specs = ([row_cur(cw_dim)] + [row_cur(rw_dim)] * 4
                 + [pl.BlockSpec((tb, rw_dim), lambda j: (prv(j), 0)),
                    pl.BlockSpec((None, 2, cw_dim), lambda j: (cur(j) // bps, 0, 0)),
                    pl.BlockSpec((None, 1, rcols), lambda j: (cur(j) // bps, 0, 0)),
                    pl.BlockSpec((None, ng, HEAD, lanes), lambda j: (prv(j) // bps, 0, 0, 0))])
    outs = pl.pallas_call(
        functools.partial(_front_scan_kernel, bps),
        grid=(nb + 1,),
        in_specs=[row_cur(d)] + [_const_spec(wt.shape) for wt in w_in],
        out_specs=out_specs,
        out_shape=out_shape,
        scratch_shapes=[pltpu.VMEM((2, len(SCAN_OPERANDS), tb, rw_dim), F32),
                        pltpu.VMEM((ng, HEAD, lanes), F32),
                        pltpu.VMEM((2, cw_dim), F32), pltpu.VMEM((1, rcols), F32)],
        compiler_params=pltpu.CompilerParams(
            dimension_semantics=("arbitrary",), vmem_limit_bytes=VMEM_LIMIT),
        name="front_scan",
    )(x.reshape(n, d), *w_in)
    yconv, r, k, v, g, y, nconv, nshift, z = outs
    per_seq = lambda t: t.reshape(bp, seq, t.shape[-1])
    return [per_seq(t) for t in (yconv, r, k, v, g, y)] + [nconv, nshift, z]


def _pscan_kernel(lw_ref, kk_ref, a_ref, r_ref, k_ref, v_ref, y_ref, zout_ref, z_ref):
    @pl.when(pl.program_id(1) == 0)
    def _():
        z_ref[...] = jnp.zeros_like(z_ref)

    refs = (lw_ref, kk_ref, a_ref, r_ref, k_ref, v_ref)
    lanes = GROUP * HEAD
    n_groups = kk_ref.shape[1] // lanes
    n_chunks = kk_ref.shape[0] // CHUNK
    masks = _scan_masks(CHUNK)

    def loader(ci, g):
        rows = slice(ci * CHUNK, (ci + 1) * CHUNK)
        sl = slice(g * lanes, (g + 1) * lanes)
        return lambda: tuple(ref[rows, sl] for ref in refs)

    gens = [[_chunk_precompute(loader(ci, g), masks) for g in range(n_groups)]
            for ci in range(n_chunks)]
    zs = [z_ref[g] for g in range(n_groups)]

    def apply_chunk(ci, pre):
        res = [_mm(jnp.concatenate([pre[g][0], pre[g][2]], axis=0),
                   jnp.where(masks["bd_ll"], jnp.concatenate([zs[g]] * GROUP, axis=0), 0.0))
               for g in range(n_groups)]
        for g in range(n_groups):
            _, yv, _, nh, ph = pre[g]
            y_ref[ci * CHUNK:(ci + 1) * CHUNK, g * lanes:(g + 1) * lanes] = res[g][:CHUNK] + yv
            zs[g] = ph * zs[g] + res[g][CHUNK:] + nh

    _run_staggered(gens, SCAN_SKEW, apply_chunk)
    for g in range(n_groups):
        z_ref[g] = zs[g]
        zout_ref[g] = zs[g]


def _prompt_scan(lw, kk, a, r, k, v, tb):
    nb, t, w = kk.shape
    lanes = GROUP * HEAD
    ng = w // lanes
    spec = pl.BlockSpec((None, tb, w), lambda b, i: (b, i, 0))
    return pl.pallas_call(
        _pscan_kernel,
        grid=(nb, t // tb),
        in_specs=[spec] * 6,
        out_specs=[spec, pl.BlockSpec((None, ng, HEAD, lanes), lambda b, i: (b, 0, 0, 0))],
        out_shape=[jax.ShapeDtypeStruct((nb, t, w), F32),
                   jax.ShapeDtypeStruct((nb, ng, HEAD, lanes), F32)],
        scratch_shapes=[pltpu.VMEM((ng, HEAD, lanes), F32)],
        compiler_params=pltpu.CompilerParams(
            dimension_semantics=("arbitrary", "arbitrary"), vmem_limit_bytes=VMEM_LIMIT),
        name="prompt_scan",
    )(lw, kk, a, r, k, v)


LANE_SLAB = 128


def _sscan_kernel(kk_ref, a_ref, lw_ref, k_ref, r_ref, v_ref, s_in_ref, y_ref, s_out_ref,
                  fm_ref, yt_ref):
    n_t = kk_ref.shape[0]
    heads, n_k = s_in_ref.shape[:2]
    shape = s_in_ref.shape[2:]
    i_kk, i_kka, i_w, i_k, i_r, i_v = range(6)
    for t in range(n_t):
        kk_t = jnp.transpose(kk_ref[t])
        fm_ref[i_kk, t] = kk_t
        fm_ref[i_kka, t] = kk_t * jnp.transpose(a_ref[t])
        fm_ref[i_w, t] = jnp.exp(jnp.transpose(lw_ref[t]))
        fm_ref[i_k, t] = jnp.transpose(k_ref[t])
        fm_ref[i_r, t] = jnp.transpose(r_ref[t])
        fm_ref[i_v, t] = jnp.transpose(v_ref[t])

    zeros = jnp.zeros(shape, F32)
    for hd in range(heads):
        off = hd * HEAD

        def rows8(i, t, k8, off=off):
            return fm_ref[i, t, pl.ds(pl.multiple_of(off + k8 * 8, 8), 8), :]

        def first_sk(k8, acc, hd=hd, rows8=rows8):
            kk8 = rows8(i_kk, 0, k8)
            for j in range(8):
                acc = acc + s_in_ref[hd, k8 * 8 + j] * kk8[j:j + 1, :]
            return acc

        sk = lax.fori_loop(0, n_k // 8, first_sk, zeros)
        for t in range(n_t):
            src = s_in_ref if t == 0 else s_out_ref
            vt = fm_ref[i_v, t, off:off + HEAD, :]
            last = t == n_t - 1

            def step(k8, carry, hd=hd, rows8=rows8, t=t, src=src, vt=vt, sk=sk, last=last):
                acc, sk_next = carry
                kka8 = rows8(i_kka, t, k8)
                w8 = rows8(i_w, t, k8)
                k8v = rows8(i_k, t, k8)
                r8 = rows8(i_r, t, k8)
                kkn8 = None if last else rows8(i_kk, t + 1, k8)
                for j in range(8):
                    snew = (src[hd, k8 * 8 + j] * w8[j:j + 1, :] - sk * kka8[j:j + 1, :]
                            + vt * k8v[j:j + 1, :])
                    s_out_ref[hd, k8 * 8 + j] = snew
                    acc = acc + snew * r8[j:j + 1, :]
                    if not last:
                        sk_next = sk_next + snew * kkn8[j:j + 1, :]
                return acc, sk_next

            y, sk = lax.fori_loop(0, n_k // 8, step, (zeros, zeros))
            yt_ref[t, off:off + HEAD, :] = y
    for t in range(n_t):
        y_ref[t] = jnp.transpose(yt_ref[t])


def _sample_scan(kk, a, lw, k, r, v, state):
    n_t, nb, w = kk.shape
    heads = LANE_SLAB // HEAD
    spec = pl.BlockSpec((n_t, nb, LANE_SLAB), lambda i: (0, 0, i))
    sspec = pl.BlockSpec((heads, HEAD, HEAD, nb), lambda i: (i, 0, 0, 0))
    return pl.pallas_call(
        _sscan_kernel,
        grid=(w // LANE_SLAB,),
        in_specs=[spec] * 6 + [sspec],
        out_specs=[spec, sspec],
        out_shape=[jax.ShapeDtypeStruct((n_t, nb, w), F32),
                   jax.ShapeDtypeStruct(state.shape, F32)],
        scratch_shapes=[pltpu.VMEM((6, n_t, LANE_SLAB, nb), F32),
                        pltpu.VMEM((n_t, LANE_SLAB, nb), F32)],
        compiler_params=pltpu.CompilerParams(
            dimension_semantics=("arbitrary",), vmem_limit_bytes=VMEM_LIMIT),
        name="sample_scan",
    )(kk, a, lw, k, r, v, state)


def kernel(x_prompt, x_sample, state_conv, state_shift, state_wkv, norm1_g, w_in, conv_w, mu, w0,
           w2, a0, a2, g2, k_k, k_a, r_k, lnx_g, lnx_b, w_out, norm2_g, w_ff1, w_ff2, normf_g):
    depth = w_in.shape[0]
    assert depth == 1, "single-layer step only"
    bp, seq, d = x_prompt.shape
    bs, dseq, _ = x_sample.shape
    cw_dim = conv_w.shape[-1]
    rw_dim = w0.shape[-1]
    rcols = mu.shape[-1]
    nh = rw_dim // HEAD
    row = lambda t: t[0].reshape(1, -1)
    seg = jnp.arange(GROUP * HEAD) // HEAD
    wts = {
        "norm1_g": row(norm1_g), "w_in": w_in[0].astype(BF16), "conv_w": conv_w[0],
        "mu": row(mu), "w0": row(w0), "w2": w2[0].astype(BF16), "a0": row(a0),
        "a2": a2[0].astype(BF16), "g2": g2[0].astype(BF16), "k_k": row(k_k), "k_a": row(k_a),
        "r_k": row(r_k), "lnx_g": row(lnx_g), "lnx_b": row(lnx_b),
        "w_out": w_out[0].astype(BF16), "norm2_g": row(norm2_g),
        "w_ff1": w_ff1[0].astype(BF16), "w_ff2": w_ff2[0].astype(BF16),
        "normf_g": normf_g.reshape(1, -1),
        "bd": (seg[:, None] == seg[None, :]).astype(BF16),
    }

    zc = jnp.zeros((bp, 2, cw_dim), F32)
    zs = jnp.zeros((bp, 1, rcols), F32)
    yconv, r, k, v, kk, a, lw, g, nconv_p, nshift_p = _front(x_prompt, zc, zs, 1, 256, wts)
    y, z = _prompt_scan(lw, kk, a, r, k, v, 512)
    y_prompt = _back(x_prompt, yconv, y, r, k, v, g, BACK_ROWS, wts)
    ng = rw_dim // (GROUP * HEAD)
    zb = z.reshape(bp, ng, HEAD, GROUP, HEAD)
    wkv_p = jnp.transpose(zb, (0, 1, 3, 4, 2)).reshape(bp, nh, HEAD, HEAD)

    xs = jnp.transpose(x_sample, (1, 0, 2)).reshape(1, dseq * bs, d)
    c0 = jnp.transpose(state_conv[0], (1, 0, 2)).reshape(1, 2 * bs, cw_dim)
    s0 = state_shift[0].reshape(1, bs, rcols)
    tbs = 2 * bs
    yconv, r, k, v, kk, a, lw, g, nconv_s, nshift_s = _front(xs, c0, s0, bs, tbs, wts)
    tm = lambda t: t.reshape(dseq, bs, rw_dim)
    st = jnp.transpose(state_wkv[0], (1, 3, 2, 0))
    y_tm, st_new = _sample_scan(tm(kk), tm(a), tm(lw), tm(k), tm(r), tm(v), st)
    y = y_tm.reshape(1, dseq * bs, rw_dim)
    ys = _back(xs, yconv, y, r, k, v, g, tbs, wts)
    y_sample = jnp.transpose(ys.reshape(dseq, bs, d), (1, 0, 2))
    wkv_s = jnp.transpose(st_new, (3, 0, 2, 1))
    nconv_s = jnp.transpose(nconv_s.reshape(2, bs, cw_dim), (1, 0, 2))

    return (y_prompt, y_sample,
            nconv_p[None], nshift_p.reshape(1, bp, rcols), wkv_p[None],
            nconv_s[None], nshift_s.reshape(1, bs, rcols), wkv_s[None])
```

```python
import functools

import jax
import jax.numpy as jnp
from jax import lax
from jax.experimental import pallas as pl
from jax.experimental.pallas import tpu as pltpu

F32 = jnp.float32
BF16 = jnp.bfloat16

HEAD = 64
RMS_EPS = 1e-6
GN_EPS = 64e-5
NORM_EPS = 1e-12
VMEM_LIMIT = 56 * 1024 * 1024
CHUNK = 64
GROUP = 4
SCAN_SKEW = 1
FRONT_ROWS = 256
SCAN_ROWS = 512
BACK_ROWS = 512


def _dot(a, b):
    return jnp.dot(a, b, preferred_element_type=F32)


def _sigmoid(x):
    return 0.5 * jnp.tanh(0.5 * x) + 0.5


def _rmsnorm(x, g):
    return x * lax.rsqrt(jnp.mean(x * x, axis=-1, keepdims=True) + RMS_EPS) * g


def _seg_sum(x, bd):
    lanes = bd.shape[0]
    xb = x.astype(BF16)
    return jnp.concatenate([_dot(xb[:, i:i + lanes], bd) for i in range(0, x.shape[1], lanes)],
                           axis=1)


def _shift_rows(x, k, head):
    rows = x.shape[0]
    if k == rows:
        return head
    if k % 8 == 0:
        return jnp.concatenate([head, x[:rows - k]], axis=0)
    rolled = pltpu.roll(x, k, 0)
    row = lax.broadcasted_iota(jnp.int32, x.shape, 0)
    out = rolled
    for j in range(k):
        out = jnp.where(row == j, head[j:j + 1, :], out)
    return out


FRONT_WEIGHTS = ("norm1_g", "w_in", "conv_w", "mu", "w0", "w2", "a0", "a2", "g2", "k_k", "k_a",
                 "bd")
def _front_block(x, cu, cp, s, w, store):
    rows = x.shape[0]
    cw_dim = cu.shape[1]
    rw_dim = w["w0"].shape[1]
    rcols = cp.shape[1]
    win = w["w_in"]
    h = _rmsnorm(x, w["norm1_g"][...]).astype(BF16)

    def proj(c0, c1):
        return _dot(h, win[:, c0:c1])

    def mixed(c0, c1):
        p = proj(3 * cw_dim + c0, 3 * cw_dim + c1)
        prev = _shift_rows(p, s, cp[:, c0:c1])
        return p + (prev - p) * w["mu"][:, c0:c1], p[rows - s:]

    lora, last_l = mixed(3 * rw_dim, rcols)
    d_w = w["w2"].shape[0]
    d_a = w["a2"].shape[0]
    pw = lora[:, :d_w]
    pa = lora[:, d_w:d_w + d_a]
    pg = lora[:, d_w + d_a:]
    wl = w["w0"][...] + _dot(jnp.tanh(pw).astype(BF16), w["w2"][...])
    z = -wl
    softplus = jnp.maximum(z, 0.0) + jnp.log(1.0 + jnp.exp(-jnp.abs(z)))
    store("lw", -jnp.exp(-softplus - 0.5))
    a = _sigmoid(w["a0"][...] + _dot(pa.astype(BF16), w["a2"][...]))
    store("a", a)
    store("g", _dot(_sigmoid(pg).astype(BF16), w["g2"][...]))

    k, last_k = mixed(rw_dim, 2 * rw_dim)
    kk = k * w["k_k"][...]
    store("kk", kk * lax.rsqrt(jnp.maximum(_seg_sum(kk * kk, w["bd"][...]), NORM_EPS * NORM_EPS)))
    store("k", k * (1.0 + (a - 1.0) * w["k_a"][...]))

    r, last_r = mixed(0, rw_dim)
    store("r", r)
    v, last_v = mixed(2 * rw_dim, 3 * rw_dim)
    store("v", v)
    store("new_cp", jnp.concatenate([last_r, last_k, last_v, last_l], axis=1))

    u = proj(cw_dim, 2 * cw_dim) * proj(2 * cw_dim, 3 * cw_dim)
    u1 = _shift_rows(u, s, cu[s:2 * s])
    u2 = _shift_rows(u, 2 * s, cu)
    cw = w["conv_w"][...]
    conv = cw[0:1] * u2 + cw[1:2] * u1 + cw[2:3] * u
    store("new_cu", u[rows - 2 * s:])
    store("yconv", (proj(0, cw_dim) * conv).astype(BF16))


FRONT_OUTPUTS = ("yconv", "r", "k", "v", "kk", "a", "lw", "g", "new_cu", "new_cp")


def _front_kernel(s, x_ref, conv0_ref, shift0_ref, *refs):
    nw = len(FRONT_WEIGHTS)
    w = dict(zip(FRONT_WEIGHTS, refs[:nw]))
    outs = dict(zip(FRONT_OUTPUTS, refs[nw:nw + len(FRONT_OUTPUTS)]))
    cu_ref, cp_ref = refs[nw + len(FRONT_OUTPUTS):]

    @pl.when(pl.program_id(1) == 0)
    def _():
        cu_ref[...] = conv0_ref[...]
        cp_ref[...] = shift0_ref[...]

    carry = {}

    def store(name, value):
        outs[name][...] = value
        carry[name] = value

    _front_block(x_ref[...], cu_ref[...], cp_ref[...], s, w, store)
    cu_ref[...] = carry["new_cu"]
    cp_ref[...] = carry["new_cp"]


def _const_spec(shape):
    nd = len(shape)
    return pl.BlockSpec(shape, lambda *_: (0,) * nd, pipeline_mode=pl.Buffered(1))


def _front(x, conv0, shift0, s, tb, wts):
    ns, tt, d = x.shape
    cw_dim = conv0.shape[-1]
    rw_dim = wts["w0"].shape[-1]
    rcols = shift0.shape[-1]
    nblk = tt // tb
    row_spec = lambda w: pl.BlockSpec((None, tb, w), lambda b, i: (b, i, 0))
    st_spec = lambda r, w: pl.BlockSpec((None, r, w), lambda b, i: (b, 0, 0))
    w_in = [wts[n] for n in FRONT_WEIGHTS]
    out_shapes = ([jax.ShapeDtypeStruct((ns, tt, cw_dim), BF16)]
                  + [jax.ShapeDtypeStruct((ns, tt, rw_dim), F32)] * 7
                  + [jax.ShapeDtypeStruct((ns, 2 * s, cw_dim), F32),
                     jax.ShapeDtypeStruct((ns, s, rcols), F32)])
    out_specs = ([row_spec(cw_dim)] + [row_spec(rw_dim)] * 7
                 + [st_spec(2 * s, cw_dim), st_spec(s, rcols)])
    return pl.pallas_call(
        functools.partial(_front_kernel, s),
        grid=(ns, nblk),
        in_specs=[row_spec(d), st_spec(2 * s, cw_dim), st_spec(s, rcols)]
        + [_const_spec(w.shape) for w in w_in],
        out_specs=out_specs,
        out_shape=out_shapes,
        scratch_shapes=[pltpu.VMEM((2 * s, cw_dim), F32), pltpu.VMEM((s, rcols), F32)],
        compiler_params=pltpu.CompilerParams(
            dimension_semantics=("arbitrary", "arbitrary"), vmem_limit_bytes=VMEM_LIMIT),
        name="front",
    )(x, conv0, shift0, *w_in)


def _back_kernel(x_ref, yconv_ref, y_ref, r_ref, k_ref, v_ref, g_ref,
                 rk_ref, lg_ref, lb_ref, wo_ref, n2_ref, f1_ref, f2_ref, nf_ref, bd_ref,
                 out_ref):
    bd = bd_ref[...]
    y = y_ref[...]
    inv_n = 1.0 / HEAD
    mean = _seg_sum(y, bd) * inv_n
    dlt = y - mean
    var = _seg_sum(dlt * dlt, bd) * inv_n
    yn = dlt * lax.rsqrt(var + GN_EPS) * lg_ref[...] + lb_ref[...]
    v = v_ref[...]
    bonus = _seg_sum(r_ref[...] * k_ref[...] * rk_ref[...], bd) * v
    y_rwkv = ((yn + bonus) * g_ref[...]).astype(BF16)
    cw_dim = yconv_ref.shape[-1]
    x = x_ref[...]
    x1 = x + _dot(yconv_ref[...], wo_ref[:cw_dim, :]) + _dot(y_rwkv, wo_ref[cw_dim:, :])
    h2 = _rmsnorm(x1, n2_ref[...]).astype(BF16)
    f = jnp.maximum(_dot(h2, f1_ref[...]), 0.0)
    x2 = x1 + _dot((f * f).astype(BF16), f2_ref[...])
    out_ref[...] = _rmsnorm(x2, nf_ref[...])


def _back(x, yconv, y, r, k, v, g, tb, wts):
    ns, tt, d = x.shape
    nblk = tt // tb
    row_spec = lambda w: pl.BlockSpec((None, tb, w), lambda b, i: (b, i, 0))
    names = ["r_k", "lnx_g", "lnx_b", "w_out", "norm2_g", "w_ff1", "w_ff2", "normf_g", "bd"]
    w_in = [wts[n] for n in names]
    acts = [x, yconv, y, r, k, v, g]
    return pl.pallas_call(
        _back_kernel,
        grid=(ns, nblk),
        in_specs=[row_spec(a.shape[-1]) for a in acts] + [_const_spec(w.shape) for w in w_in],
        out_specs=row_spec(d),
        out_shape=jax.ShapeDtypeStruct((ns, tt, d), F32),
        compiler_params=pltpu.CompilerParams(
            dimension_semantics=("arbitrary", "arbitrary"), vmem_limit_bytes=VMEM_LIMIT),
        name="back",
    )(*acts, *w_in)


def _mm(a, b):
    return jnp.dot(a.astype(BF16), b.astype(BF16), preferred_element_type=F32)


def _mm_nt(a, b):
    return lax.dot_general(a.astype(BF16), b.astype(BF16), (((1,), (1,)), ((), ())),
                           preferred_element_type=F32)


def _cumsum_rows(x):
    rows = x.shape[0]
    row = lax.broadcasted_iota(jnp.int32, x.shape, 0)
    sh = 1
    while sh < rows:
        x = x + jnp.where(row >= sh, pltpu.roll(x, sh, 0), 0.0)
        sh *= 2
    return x


def _scan_masks(c):
    lanes = GROUP * HEAD
    gc = GROUP * c
    i0 = lambda shape: lax.broadcasted_iota(jnp.int32, shape, 0)
    i1 = lambda shape: lax.broadcasted_iota(jnp.int32, shape, 1)
    col_t = i1((c, gc)) % c
    row_t = i0((c, gc))
    level = []
    s = 1
    while s < c:
        level.append((row_t // (2 * s) == col_t // (2 * s)) & (row_t % (2 * s) >= s)
                     & (col_t % (2 * s) < s))
        s *= 2
    return dict(
        level=level,
        lane_head=(i0((gc, lanes)) // c) == (i1((gc, lanes)) // HEAD),
        strict=col_t < i0((c, gc)),
        incl=col_t <= i0((c, gc)),
        bd_cc=(i0((gc, gc)) // c) == (i1((gc, gc)) // c),
        bd_ll=(i0((lanes, lanes)) // HEAD) == (i1((lanes, lanes)) // HEAD),
    )


def _chunk_precompute(load, masks):
    lw, kk, a, r, k, v = load()
    c = lw.shape[0]
    assert c == HEAD, "side-by-side forms share lane offsets only when CHUNK == HEAD"

    def bd_rows(m):
        return jnp.where(masks["lane_head"], jnp.concatenate([m] * GROUP, axis=0), 0.0)

    def bd_mat(m):
        return jnp.where(masks["bd_cc"], jnp.concatenate([m] * GROUP, axis=0), 0.0)

    def heads_t(m):
        mt = jnp.transpose(m)
        return jnp.concatenate([mt[h * HEAD:(h + 1) * HEAD, :] for h in range(GROUP)], axis=1)

    cum = _cumsum_rows(lw)
    cum_last = cum[c - 1:c, :]
    beta = kk * a
    alpha_t = -kk * jnp.exp(cum - lw)
    r_t = r * jnp.exp(cum)
    inv = jnp.exp(-cum)
    rem = jnp.exp(cum_last - cum)
    ar = jnp.concatenate([alpha_t, r_t], axis=0)
    g_b = _mm_nt(ar, bd_rows(beta * inv))
    g_k = _mm_nt(ar, bd_rows(k * inv))
    yield
    a_ab = jnp.where(masks["strict"], g_b[:c], 0.0)
    a_rb = jnp.where(masks["incl"], g_b[c:], 0.0)
    a_ak = jnp.where(masks["strict"], g_k[:c], 0.0)
    a_rk = jnp.where(masks["incl"], g_k[c:], 0.0)

    b_t = heads_t(beta * rem)
    k_t = heads_t(k * rem)
    res = _mm(jnp.concatenate([a_ak, a_rk, k_t], axis=0), bd_rows(v))
    yield
    akv, rkv, nh_k = res[:c], res[c:2 * c], res[2 * c:]

    tm, cm = None, a_ab
    n_lvl = len(masks["level"])
    for lvl, corner in enumerate(masks["level"]):
        m_s = jnp.where(corner, cm, 0.0)
        rhs = bd_mat(m_s)
        if lvl == 0:
            res = _mm(cm, rhs)
            yield
            tm, cm = m_s, cm + res
        elif lvl < n_lvl - 1:
            res = _mm(jnp.concatenate([tm, cm], axis=0), rhs)
            yield
            tm, cm = tm + m_s + res[:c], cm + res[c:]
        else:
            res = _mm(tm, rhs)
            yield
            tm = tm + m_s + res

    wa = alpha_t + _mm(tm, bd_rows(alpha_t))
    uv = akv + _mm(tm, bd_rows(akv))
    yield
    resid = akv - uv + _mm(a_ab, bd_rows(uv))
    yield
    uv = uv + resid + _mm(tm, bd_rows(resid))
    yield
    lhs = jnp.concatenate([a_rb, b_t], axis=0)
    res_w = _mm(lhs, bd_rows(wa))
    res_u = _mm(lhs, bd_rows(uv))
    p_col = jnp.transpose(jnp.broadcast_to(jnp.exp(cum_last), (8, cum.shape[1])))[:, 0:1]
    ph = jnp.concatenate([jnp.broadcast_to(p_col[h * HEAD:(h + 1) * HEAD, :], (HEAD, c))
                          for h in range(GROUP)], axis=1)
    yield
    rq = r_t + res_w[:c]
    yv = rkv + res_u[:c]
    mh = res_w[c:]
    nh = res_u[c:] + nh_k
    return rq, yv, mh, nh, ph


def _run_staggered(gens_by_chunk, skew, on_chunk_done):
    n = len(gens_by_chunk)
    results = [[None] * len(gens) for gens in gens_by_chunk]
    live = {(ci, j) for ci, gens in enumerate(gens_by_chunk) for j in range(len(gens))}
    n_done = 0
    rnd = 0
    while live:
        for ci in range(min(n, rnd // skew + 1)):
            for j, gen in enumerate(gens_by_chunk[ci]):
                if (ci, j) in live:
                    try:
                        next(gen)
                    except StopIteration as stop:
                        results[ci][j] = stop.value
                        live.discard((ci, j))
        while n_done < n and not any(ci == n_done for ci, _ in live):
            on_chunk_done(n_done, results[n_done])
            n_done += 1
        rnd += 1


SCAN_OPERANDS = ("lw", "kk", "a", "r", "k", "v")


def _pscan_kernel(lw_ref, kk_ref, a_ref, r_ref, k_ref, v_ref, y_ref, zout_ref, z_ref):
    @pl.when(pl.program_id(1) == 0)
    def _():
        z_ref[...] = jnp.zeros_like(z_ref)

    refs = (lw_ref, kk_ref, a_ref, r_ref, k_ref, v_ref)
    lanes = GROUP * HEAD
    n_groups = kk_ref.shape[1] // lanes
    n_chunks = kk_ref.shape[0] // CHUNK
    masks = _scan_masks(CHUNK)

    def loader(ci, g):
        rows = slice(ci * CHUNK, (ci + 1) * CHUNK)
        sl = slice(g * lanes, (g + 1) * lanes)
        return lambda: tuple(ref[rows, sl] for ref in refs)

    gens = [[_chunk_precompute(loader(ci, g), masks) for g in range(n_groups)]
            for ci in range(n_chunks)]
    zs = [z_ref[g] for g in range(n_groups)]

    def apply_chunk(ci, pre):
        res = [_mm(jnp.concatenate([pre[g][0], pre[g][2]], axis=0),
                   jnp.where(masks["bd_ll"], jnp.concatenate([zs[g]] * GROUP, axis=0), 0.0))
               for g in range(n_groups)]
        for g in range(n_groups):
            _, yv, _, nh, ph = pre[g]
            y_ref[ci * CHUNK:(ci + 1) * CHUNK, g * lanes:(g + 1) * lanes] = res[g][:CHUNK] + yv
            zs[g] = ph * zs[g] + res[g][CHUNK:] + nh

    _run_staggered(gens, SCAN_SKEW, apply_chunk)
    for g in range(n_groups):
        z_ref[g] = zs[g]
        zout_ref[g] = zs[g]


def _prompt_scan(lw, kk, a, r, k, v, tb):
    nb, t, w = kk.shape
    lanes = GROUP * HEAD
    ng = w // lanes
    spec = pl.BlockSpec((None, tb, w), lambda b, i: (b, i, 0))
    return pl.pallas_call(
        _pscan_kernel,
        grid=(nb, t // tb),
        in_specs=[spec] * 6,
        out_specs=[spec, pl.BlockSpec((None, ng, HEAD, lanes), lambda b, i: (b, 0, 0, 0))],
        out_shape=[jax.ShapeDtypeStruct((nb, t, w), F32),
                   jax.ShapeDtypeStruct((nb, ng, HEAD, lanes), F32)],
        scratch_shapes=[pltpu.VMEM((ng, HEAD, lanes), F32)],
        compiler_params=pltpu.CompilerParams(
            dimension_semantics=("arbitrary", "arbitrary"), vmem_limit_bytes=VMEM_LIMIT),
        name="prompt_scan",
    )(lw, kk, a, r, k, v)


LANE_SLAB = 128


def _sscan_kernel(kk_ref, a_ref, lw_ref, k_ref, r_ref, v_ref, s_in_ref, y_ref, s_out_ref,
                  fm_ref, yt_ref):
    n_t = kk_ref.shape[0]
    heads, n_k = s_in_ref.shape[:2]
    shape = s_in_ref.shape[2:]
    i_kk, i_kka, i_w, i_k, i_r, i_v = range(6)
    for t in range(n_t):
        kk_t = jnp.transpose(kk_ref[t])
        fm_ref[i_kk, t] = kk_t
        fm_ref[i_kka, t] = kk_t * jnp.transpose(a_ref[t])
        fm_ref[i_w, t] = jnp.exp(jnp.transpose(lw_ref[t]))
        fm_ref[i_k, t] = jnp.transpose(k_ref[t])
        fm_ref[i_r, t] = jnp.transpose(r_ref[t])
        fm_ref[i_v, t] = jnp.transpose(v_ref[t])

    zeros = jnp.zeros(shape, F32)
    for hd in range(heads):
        off = hd * HEAD

        def rows8(i, t, k8, off=off):
            return fm_ref[i, t, pl.ds(pl.multiple_of(off + k8 * 8, 8), 8), :]

        def first_sk(k8, acc, hd=hd, rows8=rows8):
            kk8 = rows8(i_kk, 0, k8)
            for j in range(8):
                acc = acc + s_in_ref[hd, k8 * 8 + j] * kk8[j:j + 1, :]
            return acc

        sk = lax.fori_loop(0, n_k // 8, first_sk, zeros)
        for t in range(n_t):
            src = s_in_ref if t == 0 else s_out_ref
            vt = fm_ref[i_v, t, off:off + HEAD, :]
            last = t == n_t - 1

            def step(k8, carry, hd=hd, rows8=rows8, t=t, src=src, vt=vt, sk=sk, last=last):
                acc, sk_next = carry
                kka8 = rows8(i_kka, t, k8)
                w8 = rows8(i_w, t, k8)
                k8v = rows8(i_k, t, k8)
                r8 = rows8(i_r, t, k8)
                kkn8 = None if last else rows8(i_kk, t + 1, k8)
                for j in range(8):
                    snew = (src[hd, k8 * 8 + j] * w8[j:j + 1, :] - sk * kka8[j:j + 1, :]
                            + vt * k8v[j:j + 1, :])
                    s_out_ref[hd, k8 * 8 + j] = snew
                    acc = acc + snew * r8[j:j + 1, :]
                    if not last:
                        sk_next = sk_next + snew * kkn8[j:j + 1, :]
                return acc, sk_next

            y, sk = lax.fori_loop(0, n_k // 8, step, (zeros, zeros))
            yt_ref[t, off:off + HEAD, :] = y
    for t in range(n_t):
        y_ref[t] = jnp.transpose(yt_ref[t])


def _sample_scan(kk, a, lw, k, r, v, state):
    n_t, nb, w = kk.shape
    heads = LANE_SLAB // HEAD
    spec = pl.BlockSpec((n_t, nb, LANE_SLAB), lambda i: (0, 0, i))
    sspec = pl.BlockSpec((heads, HEAD, HEAD, nb), lambda i: (i, 0, 0, 0))
    return pl.pallas_call(
        _sscan_kernel,
        grid=(w // LANE_SLAB,),
        in_specs=[spec] * 6 + [sspec],
        out_specs=[spec, sspec],
        out_shape=[jax.ShapeDtypeStruct((n_t, nb, w), F32),
                   jax.ShapeDtypeStruct(state.shape, F32)],
        scratch_shapes=[pltpu.VMEM((6, n_t, LANE_SLAB, nb), F32),
                        pltpu.VMEM((n_t, LANE_SLAB, nb), F32)],
        compiler_params=pltpu.CompilerParams(
            dimension_semantics=("arbitrary",), vmem_limit_bytes=VMEM_LIMIT),
        name="sample_scan",
    )(kk, a, lw, k, r, v, state)


def kernel(x_prompt, x_sample, state_conv, state_shift, state_wkv, norm1_g, w_in, conv_w, mu, w0,
           w2, a0, a2, g2, k_k, k_a, r_k, lnx_g, lnx_b, w_out, norm2_g, w_ff1, w_ff2, normf_g):
    depth = w_in.shape[0]
    assert depth == 1, "single-layer step only"
    bp, seq, d = x_prompt.shape
    bs, dseq, _ = x_sample.shape
    cw_dim = conv_w.shape[-1]
    rw_dim = w0.shape[-1]
    rcols = mu.shape[-1]
    nh = rw_dim // HEAD
    row = lambda t: t[0].reshape(1, -1)
    seg = jnp.arange(GROUP * HEAD) // HEAD
    wts = {
        "norm1_g": row(norm1_g), "w_in": w_in[0].astype(BF16), "conv_w": conv_w[0],
        "mu": row(mu), "w0": row(w0), "w2": w2[0].astype(BF16), "a0": row(a0),
        "a2": a2[0].astype(BF16), "g2": g2[0].astype(BF16), "k_k": row(k_k), "k_a": row(k_a),
        "r_k": row(r_k), "lnx_g": row(lnx_g), "lnx_b": row(lnx_b),
        "w_out": w_out[0].astype(BF16), "norm2_g": row(norm2_g),
        "w_ff1": w_ff1[0].astype(BF16), "w_ff2": w_ff2[0].astype(BF16),
        "normf_g": normf_g.reshape(1, -1),
        "bd": (seg[:, None] == seg[None, :]).astype(BF16),
    }

    zc = jnp.zeros((bp, 2, cw_dim), F32)
    zs = jnp.zeros((bp, 1, rcols), F32)
    yconv, r, k, v, kk, a, lw, g, nconv_p, nshift_p = _front(x_prompt, zc, zs, 1, FRONT_ROWS, wts)
    y, z = _prompt_scan(lw, kk, a, r, k, v, SCAN_ROWS)
    y_prompt = _back(x_prompt, yconv, y, r, k, v, g, BACK_ROWS, wts)
    ng = rw_dim // (GROUP * HEAD)
    zb = z.reshape(bp, ng, HEAD, GROUP, HEAD)
    wkv_p = jnp.transpose(zb, (0, 1, 3, 4, 2)).reshape(bp, nh, HEAD, HEAD)

    xs = jnp.transpose(x_sample, (1, 0, 2)).reshape(1, dseq * bs, d)
    c0 = jnp.transpose(state_conv[0], (1, 0, 2)).reshape(1, 2 * bs, cw_dim)
    s0 = state_shift[0].reshape(1, bs, rcols)
    tbs = 2 * bs
    yconv, r, k, v, kk, a, lw, g, nconv_s, nshift_s = _front(xs, c0, s0, bs, tbs, wts)
    tm = lambda t: t.reshape(dseq, bs, rw_dim)
    st = jnp.transpose(state_wkv[0], (1, 3, 2, 0))
    y_tm, st_new = _sample_scan(tm(kk), tm(a), tm(lw), tm(k), tm(r), tm(v), st)
    y = y_tm.reshape(1, dseq * bs, rw_dim)
    ys = _back(xs, yconv, y, r, k, v, g, tbs, wts)
    y_sample = jnp.transpose(ys.reshape(dseq, bs, d), (1, 0, 2))
    wkv_s = jnp.transpose(st_new, (3, 0, 2, 1))
    nconv_s = jnp.transpose(nconv_s.reshape(2, bs, cw_dim), (1, 0, 2))

    return (y_prompt, y_sample,
            nconv_p[None], nshift_p.reshape(1, bp, rcols), wkv_p[None],
            nconv_s[None], nshift_s.reshape(1, bs, rcols), wkv_s[None])
```

```python
import functools

import jax
import jax.numpy as jnp
from jax import lax
from jax.experimental import pallas as pl
from jax.experimental.pallas import tpu as pltpu

F32 = jnp.float32
BF16 = jnp.bfloat16

HEAD = 64
RMS_EPS = 1e-6
GN_EPS = 64e-5
NORM_EPS = 1e-12
VMEM_LIMIT = 56 * 1024 * 1024
CHUNK = 64
GROUP = 4
SCAN_SKEW = 1
FRONT_ROWS = 256
SCAN_ROWS = 512
BACK_ROWS = 512


def _dot(a, b):
    return jnp.dot(a, b, preferred_element_type=F32)


def _sigmoid(x):
    return 0.5 * jnp.tanh(0.5 * x) + 0.5


def _rmsnorm(x, g):
    return x * lax.rsqrt(jnp.mean(x * x, axis=-1, keepdims=True) + RMS_EPS) * g


def _seg_sum(x, bd):
    lanes = bd.shape[0]
    xb = x.astype(BF16)
    return jnp.concatenate([_dot(xb[:, i:i + lanes], bd) for i in range(0, x.shape[1], lanes)],
                           axis=1)


def _shift_rows(x, k, head):
    rows = x.shape[0]
    if k == rows:
        return head
    if k % 8 == 0:
        return jnp.concatenate([head, x[:rows - k]], axis=0)
    rolled = pltpu.roll(x, k, 0)
    row = lax.broadcasted_iota(jnp.int32, x.shape, 0)
    out = rolled
    for j in range(k):
        out = jnp.where(row == j, head[j:j + 1, :], out)
    return out


FRONT_WEIGHTS = ("norm1_g", "w_in", "conv_w", "mu", "w0", "w2", "a0", "a2", "g2", "k_k", "k_a",
                 "bd")
def _front_block(x, cu, cp, s, w, store):
    rows = x.shape[0]
    cw_dim = cu.shape[1]
    rw_dim = w["w0"].shape[1]
    rcols = cp.shape[1]
    win = w["w_in"]
    h = _rmsnorm(x, w["norm1_g"][...]).astype(BF16)

    def proj(c0, c1):
        return _dot(h, win[:, c0:c1])

    def rwkv_piece(c0, c1):
        return proj(3 * cw_dim + c0, 3 * cw_dim + c1)

    def mixed(p, c0, c1):
        prev = _shift_rows(p, s, cp[:, c0:c1])
        return p + (prev - p) * w["mu"][:, c0:c1], p[rows - s:]

    p_lora = rwkv_piece(3 * rw_dim, rcols)
    p_k = rwkv_piece(rw_dim, 2 * rw_dim)
    p_r = rwkv_piece(0, rw_dim)
    lora, last_l = mixed(p_lora, 3 * rw_dim, rcols)
    d_w = w["w2"].shape[0]
    d_a = w["a2"].shape[0]
    pw = lora[:, :d_w]
    pa = lora[:, d_w:d_w + d_a]
    pg = lora[:, d_w + d_a:]
    wl = w["w0"][...] + _dot(jnp.tanh(pw).astype(BF16), w["w2"][...])
    z = -wl
    softplus = jnp.maximum(z, 0.0) + jnp.log(1.0 + jnp.exp(-jnp.abs(z)))
    store("lw", -jnp.exp(-softplus - 0.5))
    a = _sigmoid(w["a0"][...] + _dot(pa.astype(BF16), w["a2"][...]))
    store("a", a)
    store("g", _dot(_sigmoid(pg).astype(BF16), w["g2"][...]))

    p_v = rwkv_piece(2 * rw_dim, 3 * rw_dim)
    k, last_k = mixed(p_k, rw_dim, 2 * rw_dim)
    kk = k * w["k_k"][...]
    store("kk", kk * lax.rsqrt(jnp.maximum(_seg_sum(kk * kk, w["bd"][...]), NORM_EPS * NORM_EPS)))
    store("k", k * (1.0 + (a - 1.0) * w["k_a"][...]))

    c_c = proj(cw_dim, 2 * cw_dim)
    r, last_r = mixed(p_r, 0, rw_dim)
    store("r", r)
    c_x = proj(2 * cw_dim, 3 * cw_dim)
    v, last_v = mixed(p_v, 2 * rw_dim, 3 * rw_dim)
    store("v", v)
    store("new_cp", jnp.concatenate([last_r, last_k, last_v, last_l], axis=1))

    c_b = proj(0, cw_dim)
    u = c_c * c_x
    u1 = _shift_rows(u, s, cu[s:2 * s])
    u2 = _shift_rows(u, 2 * s, cu)
    cw = w["conv_w"][...]
    conv = cw[0:1] * u2 + cw[1:2] * u1 + cw[2:3] * u
    store("new_cu", u[rows - 2 * s:])
    store("yconv", (c_b * conv).astype(BF16))


FRONT_OUTPUTS = ("yconv", "r", "k", "v", "kk", "a", "lw", "g", "new_cu", "new_cp")


def _front_kernel(s, x_ref, conv0_ref, shift0_ref, *refs):
    nw = len(FRONT_WEIGHTS)
    w = dict(zip(FRONT_WEIGHTS, refs[:nw]))
    outs = dict(zip(FRONT_OUTPUTS, refs[nw:nw + len(FRONT_OUTPUTS)]))
    cu_ref, cp_ref = refs[nw + len(FRONT_OUTPUTS):]

    @pl.when(pl.program_id(1) == 0)
    def _():
        cu_ref[...] = conv0_ref[...]
        cp_ref[...] = shift0_ref[...]

    carry = {}

    def store(name, value):
        outs[name][...] = value
        carry[name] = value

    _front_block(x_ref[...], cu_ref[...], cp_ref[...], s, w, store)
    cu_ref[...] = carry["new_cu"]
    cp_ref[...] = carry["new_cp"]


def _const_spec(shape):
    nd = len(shape)
    return pl.BlockSpec(shape, lambda *_: (0,) * nd, pipeline_mode=pl.Buffered(1))


def _front(x, conv0, shift0, s, tb, wts):
    ns, tt, d = x.shape
    cw_dim = conv0.shape[-1]
    rw_dim = wts["w0"].shape[-1]
    rcols = shift0.shape[-1]
    nblk = tt // tb
    row_spec = lambda w: pl.BlockSpec((None, tb, w), lambda b, i: (b, i, 0))
    st_spec = lambda r, w: pl.BlockSpec((None, r, w), lambda b, i: (b, 0, 0))
    w_in = [wts[n] for n in FRONT_WEIGHTS]
    out_shapes = ([jax.ShapeDtypeStruct((ns, tt, cw_dim), BF16)]
                  + [jax.ShapeDtypeStruct((ns, tt, rw_dim), F32)] * 7
                  + [jax.ShapeDtypeStruct((ns, 2 * s, cw_dim), F32),
                     jax.ShapeDtypeStruct((ns, s, rcols), F32)])
    out_specs = ([row_spec(cw_dim)] + [row_spec(rw_dim)] * 7
                 + [st_spec(2 * s, cw_dim), st_spec(s, rcols)])
    return pl.pallas_call(
        functools.partial(_front_kernel, s),
        grid=(ns, nblk),
        in_specs=[row_spec(d), st_spec(2 * s, cw_dim), st_spec(s, rcols)]
        + [_const_spec(w.shape) for w in w_in],
        out_specs=out_specs,
        out_shape=out_shapes,
        scratch_shapes=[pltpu.VMEM((2 * s, cw_dim), F32), pltpu.VMEM((s, rcols), F32)],
        compiler_params=pltpu.CompilerParams(
            dimension_semantics=("arbitrary", "arbitrary"), vmem_limit_bytes=VMEM_LIMIT),
        name="front",
    )(x, conv0, shift0, *w_in)


def _back_kernel(x_ref, yconv_ref, y_ref, r_ref, k_ref, v_ref, g_ref,
                 rk_ref, lg_ref, lb_ref, wo_ref, n2_ref, f1_ref, f2_ref, nf_ref, bd_ref,
                 out_ref):
    bd = bd_ref[...]
    y = y_ref[...]
    inv_n = 1.0 / HEAD
    mean = _seg_sum(y, bd) * inv_n
    dlt = y - mean
    var = _seg_sum(dlt * dlt, bd) * inv_n
    yn = dlt * lax.rsqrt(var + GN_EPS) * lg_ref[...] + lb_ref[...]
    v = v_ref[...]
    bonus = _seg_sum(r_ref[...] * k_ref[...] * rk_ref[...], bd) * v
    y_rwkv = ((yn + bonus) * g_ref[...]).astype(BF16)
    cw_dim = yconv_ref.shape[-1]
    x = x_ref[...]
    x1 = x + _dot(yconv_ref[...], wo_ref[:cw_dim, :]) + _dot(y_rwkv, wo_ref[cw_dim:, :])
    h2 = _rmsnorm(x1, n2_ref[...]).astype(BF16)
    f = jnp.maximum(_dot(h2, f1_ref[...]), 0.0)
    x2 = x1 + _dot((f * f).astype(BF16), f2_ref[...])
    out_ref[...] = _rmsnorm(x2, nf_ref[...])


def _back(x, yconv, y, r, k, v, g, tb, wts):
    ns, tt, d = x.shape
    nblk = tt // tb
    row_spec = lambda w: pl.BlockSpec((None, tb, w), lambda b, i: (b, i, 0))
    names = ["r_k", "lnx_g", "lnx_b", "w_out", "norm2_g", "w_ff1", "w_ff2", "normf_g", "bd"]
    w_in = [wts[n] for n in names]
    acts = [x, yconv, y, r, k, v, g]
    return pl.pallas_call(
        _back_kernel,
        grid=(ns, nblk),
        in_specs=[row_spec(a.shape[-1]) for a in acts] + [_const_spec(w.shape) for w in w_in],
        out_specs=row_spec(d),
        out_shape=jax.ShapeDtypeStruct((ns, tt, d), F32),
        compiler_params=pltpu.CompilerParams(
            dimension_semantics=("arbitrary", "arbitrary"), vmem_limit_bytes=VMEM_LIMIT),
        name="back",
    )(*acts, *w_in)


def _mm(a, b):
    return jnp.dot(a.astype(BF16), b.astype(BF16), preferred_element_type=F32)


def _mm_nt(a, b):
    return lax.dot_general(a.astype(BF16), b.astype(BF16), (((1,), (1,)), ((), ())),
                           preferred_element_type=F32)


def _cumsum_rows(x):
    rows = x.shape[0]
    row = lax.broadcasted_iota(jnp.int32, x.shape, 0)
    sh = 1
    while sh < rows:
        x = x + jnp.where(row >= sh, pltpu.roll(x, sh, 0), 0.0)
        sh *= 2
    return x


def _scan_masks(c):
    lanes = GROUP * HEAD
    gc = GROUP * c
    i0 = lambda shape: lax.broadcasted_iota(jnp.int32, shape, 0)
    i1 = lambda shape: lax.broadcasted_iota(jnp.int32, shape, 1)
    col_t = i1((c, gc)) % c
    row_t = i0((c, gc))
    level = []
    s = 1
    while s < c:
        level.append((row_t // (2 * s) == col_t // (2 * s)) & (row_t % (2 * s) >= s)
                     & (col_t % (2 * s) < s))
        s *= 2
    return dict(
        level=level,
        lane_head=(i0((gc, lanes)) // c) == (i1((gc, lanes)) // HEAD),
        strict=col_t < i0((c, gc)),
        incl=col_t <= i0((c, gc)),
        bd_cc=(i0((gc, gc)) // c) == (i1((gc, gc)) // c),
        bd_ll=(i0((lanes, lanes)) // HEAD) == (i1((lanes, lanes)) // HEAD),
    )


def _chunk_precompute(load, masks):
    lw, kk, a, r, k, v = load()
    c = lw.shape[0]
    assert c == HEAD, "side-by-side forms share lane offsets only when CHUNK == HEAD"

    def bd_rows(m):
        return jnp.where(masks["lane_head"], jnp.concatenate([m] * GROUP, axis=0), 0.0)

    def bd_mat(m):
        return jnp.where(masks["bd_cc"], jnp.concatenate([m] * GROUP, axis=0), 0.0)

    def heads_t(m):
        mt = jnp.transpose(m)
        return jnp.concatenate([mt[h * HEAD:(h + 1) * HEAD, :] for h in range(GROUP)], axis=1)

    cum = _cumsum_rows(lw)
    cum_last = cum[c - 1:c, :]
    beta = kk * a
    alpha_t = -kk * jnp.exp(cum - lw)
    r_t = r * jnp.exp(cum)
    inv = jnp.exp(-cum)
    rem = jnp.exp(cum_last - cum)
    ar = jnp.concatenate([alpha_t, r_t], axis=0)
    g_b = _mm_nt(ar, bd_rows(beta * inv))
    g_k = _mm_nt(ar, bd_rows(k * inv))
    yield
    a_ab = jnp.where(masks["strict"], g_b[:c], 0.0)
    a_rb = jnp.where(masks["incl"], g_b[c:], 0.0)
    a_ak = jnp.where(masks["strict"], g_k[:c], 0.0)
    a_rk = jnp.where(masks["incl"], g_k[c:], 0.0)

    b_t = heads_t(beta * rem)
    k_t = heads_t(k * rem)
    res = _mm(jnp.concatenate([a_ak, a_rk, k_t], axis=0), bd_rows(v))
    yield
    akv, rkv, nh_k = res[:c], res[c:2 * c], res[2 * c:]

    tm, cm = None, a_ab
    n_lvl = len(masks["level"])
    for lvl, corner in enumerate(masks["level"]):
        m_s = jnp.where(corner, cm, 0.0)
        rhs = bd_mat(m_s)
        if lvl == 0:
            res = _mm(cm, rhs)
            yield
            tm, cm = m_s, cm + res
        elif lvl < n_lvl - 1:
            res = _mm(jnp.concatenate([tm, cm], axis=0), rhs)
            yield
            tm, cm = tm + m_s + res[:c], cm + res[c:]
        else:
            res = _mm(tm, rhs)
            yield
            tm = tm + m_s + res

    wa = alpha_t + _mm(tm, bd_rows(alpha_t))
    uv = akv + _mm(tm, bd_rows(akv))
    yield
    resid = akv - uv + _mm(a_ab, bd_rows(uv))
    yield
    uv = uv + resid + _mm(tm, bd_rows(resid))
    yield
    lhs = jnp.concatenate([a_rb, b_t], axis=0)
    res_w = _mm(lhs, bd_rows(wa))
    res_u = _mm(lhs, bd_rows(uv))
    p_col = jnp.transpose(jnp.broadcast_to(jnp.exp(cum_last), (8, cum.shape[1])))[:, 0:1]
    ph = jnp.concatenate([jnp.broadcast_to(p_col[h * HEAD:(h + 1) * HEAD, :], (HEAD, c))
                          for h in range(GROUP)], axis=1)
    yield
    rq = r_t + res_w[:c]
    yv = rkv + res_u[:c]
    mh = res_w[c:]
    nh = res_u[c:] + nh_k
    return rq, yv, mh, nh, ph


def _run_staggered(gens_by_chunk, skew, on_chunk_done):
    n = len(gens_by_chunk)
    results = [[None] * len(gens) for gens in gens_by_chunk]
    live = {(ci, j) for ci, gens in enumerate(gens_by_chunk) for j in range(len(gens))}
    n_done = 0
    rnd = 0
    while live:
        for ci in range(min(n, rnd // skew + 1)):
            for j, gen in enumerate(gens_by_chunk[ci]):
                if (ci, j) in live:
                    try:
                        next(gen)
                    except StopIteration as stop:
                        results[ci][j] = stop.value
                        live.discard((ci, j))
        while n_done < n and not any(ci == n_done for ci, _ in live):
            on_chunk_done(n_done, results[n_done])
            n_done += 1
        rnd += 1


SCAN_OPERANDS = ("lw", "kk", "a", "r", "k", "v")


def _pscan_kernel(lw_ref, kk_ref, a_ref, r_ref, k_ref, v_ref, y_ref, zout_ref, z_ref):
    @pl.when(pl.program_id(1) == 0)
    def _():
        z_ref[...] = jnp.zeros_like(z_ref)

    refs = (lw_ref, kk_ref, a_ref, r_ref, k_ref, v_ref)
    lanes = GROUP * HEAD
    n_groups = kk_ref.shape[1] // lanes
    n_chunks = kk_ref.shape[0] // CHUNK
    masks = _scan_masks(CHUNK)

    def loader(ci, g):
        rows = slice(ci * CHUNK, (ci + 1) * CHUNK)
        sl = slice(g * lanes, (g + 1) * lanes)
        return lambda: tuple(ref[rows, sl] for ref in refs)

    gens = [[_chunk_precompute(loader(ci, g), masks) for g in range(n_groups)]
            for ci in range(n_chunks)]
    zs = [z_ref[g] for g in range(n_groups)]

    def apply_chunk(ci, pre):
        res = [_mm(jnp.concatenate([pre[g][0], pre[g][2]], axis=0),
                   jnp.where(masks["bd_ll"], jnp.concatenate([zs[g]] * GROUP, axis=0), 0.0))
               for g in range(n_groups)]
        for g in range(n_groups):
            _, yv, _, nh, ph = pre[g]
            y_ref[ci * CHUNK:(ci + 1) * CHUNK, g * lanes:(g + 1) * lanes] = res[g][:CHUNK] + yv
            zs[g] = ph * zs[g] + res[g][CHUNK:] + nh

    _run_staggered(gens, SCAN_SKEW, apply_chunk)
    for g in range(n_groups):
        z_ref[g] = zs[g]
        zout_ref[g] = zs[g]


def _prompt_scan(lw, kk, a, r, k, v, tb):
    nb, t, w = kk.shape
    lanes = GROUP * HEAD
    ng = w // lanes
    spec = pl.BlockSpec((None, tb, w), lambda b, i: (b, i, 0))
    return pl.pallas_call(
        _pscan_kernel,
        grid=(nb, t // tb),
        in_specs=[spec] * 6,
        out_specs=[spec, pl.BlockSpec((None, ng, HEAD, lanes), lambda b, i: (b, 0, 0, 0))],
        out_shape=[jax.ShapeDtypeStruct((nb, t, w), F32),
                   jax.ShapeDtypeStruct((nb, ng, HEAD, lanes), F32)],
        scratch_shapes=[pltpu.VMEM((ng, HEAD, lanes), F32)],
        compiler_params=pltpu.CompilerParams(
            dimension_semantics=("arbitrary", "arbitrary"), vmem_limit_bytes=VMEM_LIMIT),
        name="prompt_scan",
    )(lw, kk, a, r, k, v)


LANE_SLAB = 128


def _sscan_kernel(kk_ref, a_ref, lw_ref, k_ref, r_ref, v_ref, s_in_ref, y_ref, s_out_ref,
                  fm_ref, yt_ref):
    n_t = kk_ref.shape[0]
    heads, n_k = s_in_ref.shape[:2]
    shape = s_in_ref.shape[2:]
    i_kk, i_kka, i_w, i_k, i_r, i_v = range(6)
    for t in range(n_t):
        kk_t = jnp.transpose(kk_ref[t])
        fm_ref[i_kk, t] = kk_t
        fm_ref[i_kka, t] = kk_t * jnp.transpose(a_ref[t])
        fm_ref[i_w, t] = jnp.exp(jnp.transpose(lw_ref[t]))
        fm_ref[i_k, t] = jnp.transpose(k_ref[t])
        fm_ref[i_r, t] = jnp.transpose(r_ref[t])
        fm_ref[i_v, t] = jnp.transpose(v_ref[t])

    zeros = jnp.zeros(shape, F32)
    for hd in range(heads):
        off = hd * HEAD

        def rows8(i, t, k8, off=off):
            return fm_ref[i, t, pl.ds(pl.multiple_of(off + k8 * 8, 8), 8), :]

        def first_sk(k8, acc, hd=hd, rows8=rows8):
            kk8 = rows8(i_kk, 0, k8)
            for j in range(8):
                acc = acc + s_in_ref[hd, k8 * 8 + j] * kk8[j:j + 1, :]
            return acc

        sk = lax.fori_loop(0, n_k // 8, first_sk, zeros)
        for t in range(n_t):
            src = s_in_ref if t == 0 else s_out_ref
            vt = fm_ref[i_v, t, off:off + HEAD, :]
            last = t == n_t - 1

            def step(k8, carry, hd=hd, rows8=rows8, t=t, src=src, vt=vt, sk=sk, last=last):
                acc, sk_next = carry
                kka8 = rows8(i_kka, t, k8)
                w8 = rows8(i_w, t, k8)
                k8v = rows8(i_k, t, k8)
                r8 = rows8(i_r, t, k8)
                kkn8 = None if last else rows8(i_kk, t + 1, k8)
                for j in range(8):
                    snew = (src[hd, k8 * 8 + j] * w8[j:j + 1, :] - sk * kka8[j:j + 1, :]
                            + vt * k8v[j:j + 1, :])
                    s_out_ref[hd, k8 * 8 + j] = snew
                    acc = acc + snew * r8[j:j + 1, :]
                    if not last:
                        sk_next = sk_next + snew * kkn8[j:j + 1, :]
                return acc, sk_next

            y, sk = lax.fori_loop(0, n_k // 8, step, (zeros, zeros))
            yt_ref[t, off:off + HEAD, :] = y
    for t in range(n_t):
        y_ref[t] = jnp.transpose(yt_ref[t])


def _sample_scan(kk, a, lw, k, r, v, state):
    n_t, nb, w = kk.shape
    heads = LANE_SLAB // HEAD
    spec = pl.BlockSpec((n_t, nb, LANE_SLAB), lambda i: (0, 0, i))
    sspec = pl.BlockSpec((heads, HEAD, HEAD, nb), lambda i: (i, 0, 0, 0))
    return pl.pallas_call(
        _sscan_kernel,
        grid=(w // LANE_SLAB,),
        in_specs=[spec] * 6 + [sspec],
        out_specs=[spec, sspec],
        out_shape=[jax.ShapeDtypeStruct((n_t, nb, w), F32),
                   jax.ShapeDtypeStruct(state.shape, F32)],
        scratch_shapes=[pltpu.VMEM((6, n_t, LANE_SLAB, nb), F32),
                        pltpu.VMEM((n_t, LANE_SLAB, nb), F32)],
        compiler_params=pltpu.CompilerParams(
            dimension_semantics=("arbitrary",), vmem_limit_bytes=VMEM_LIMIT),
        name="sample_scan",
    )(kk, a, lw, k, r, v, state)


def kernel(x_prompt, x_sample, state_conv, state_shift, state_wkv, norm1_g, w_in, conv_w, mu, w0,
           w2, a0, a2, g2, k_k, k_a, r_k, lnx_g, lnx_b, w_out, norm2_g, w_ff1, w_ff2, normf_g):
    depth = w_in.shape[0]
    assert depth == 1, "single-layer step only"
    bp, seq, d = x_prompt.shape
    bs, dseq, _ = x_sample.shape
    cw_dim = conv_w.shape[-1]
    rw_dim = w0.shape[-1]
    rcols = mu.shape[-1]
    nh = rw_dim // HEAD
    row = lambda t: t[0].reshape(1, -1)
    seg = jnp.arange(GROUP * HEAD) // HEAD
    wts = {
        "norm1_g": row(norm1_g), "w_in": w_in[0].astype(BF16), "conv_w": conv_w[0],
        "mu": row(mu), "w0": row(w0), "w2": w2[0].astype(BF16), "a0": row(a0),
        "a2": a2[0].astype(BF16), "g2": g2[0].astype(BF16), "k_k": row(k_k), "k_a": row(k_a),
        "r_k": row(r_k), "lnx_g": row(lnx_g), "lnx_b": row(lnx_b),
        "w_out": w_out[0].astype(BF16), "norm2_g": row(norm2_g),
        "w_ff1": w_ff1[0].astype(BF16), "w_ff2": w_ff2[0].astype(BF16),
        "normf_g": normf_g.reshape(1, -1),
        "bd": (seg[:, None] == seg[None, :]).astype(BF16),
    }

    zc = jnp.zeros((bp, 2, cw_dim), F32)
    zs = jnp.zeros((bp, 1, rcols), F32)
    yconv, r, k, v, kk, a, lw, g, nconv_p, nshift_p = _front(x_prompt, zc, zs, 1, FRONT_ROWS, wts)
    y, z = _prompt_scan(lw, kk, a, r, k, v, SCAN_ROWS)
    y_prompt = _back(x_prompt, yconv, y, r, k, v, g, BACK_ROWS, wts)
    ng = rw_dim // (GROUP * HEAD)
    zb = z.reshape(bp, ng, HEAD, GROUP, HEAD)
    wkv_p = jnp.transpose(zb, (0, 1, 3, 4, 2)).reshape(bp, nh, HEAD, HEAD)

    xs = jnp.transpose(x_sample, (1, 0, 2)).reshape(1, dseq * bs, d)
    c0 = jnp.transpose(state_conv[0], (1, 0, 2)).reshape(1, 2 * bs, cw_dim)
    s0 = state_shift[0].reshape(1, bs, rcols)
    tbs = 2 * bs
    yconv, r, k, v, kk, a, lw, g, nconv_s, nshift_s = _front(xs, c0, s0, bs, tbs, wts)
    tm = lambda t: t.reshape(dseq, bs, rw_dim)
    st = jnp.transpose(state_wkv[0], (1, 3, 2, 0))
    y_tm, st_new = _sample_scan(tm(kk), tm(a), tm(lw), tm(k), tm(r), tm(v), st)
    y = y_tm.reshape(1, dseq * bs, rw_dim)
    ys = _back(xs, yconv, y, r, k, v, g, tbs, wts)
    y_sample = jnp.transpose(ys.reshape(dseq, bs, d), (1, 0, 2))
    wkv_s = jnp.transpose(st_new, (3, 0, 2, 1))
    nconv_s = jnp.transpose(nconv_s.reshape(2, bs, cw_dim), (1, 0, 2))

    return (y_prompt, y_sample,
            nconv_p[None], nshift_p.reshape(1, bp, rcols), wkv_p[None],
            nconv_s[None], nshift_s.reshape(1, bs, rcols), wkv_s[None])
```

```python
import functools

import jax
import jax.numpy as jnp
from jax import lax
from jax.experimental import pallas as pl
from jax.experimental.pallas import tpu as pltpu

F32 = jnp.float32
BF16 = jnp.bfloat16

HEAD = 64
RMS_EPS = 1e-6
GN_EPS = 64e-5
NORM_EPS = 1e-12
VMEM_LIMIT = 56 * 1024 * 1024
CHUNK = 64
GROUP = 4
SCAN_SKEW = 1
FRONT_ROWS = 512
SCAN_ROWS = 512
BACK_ROWS = 512


def _dot(a, b):
    return jnp.dot(a, b, preferred_element_type=F32)


def _sigmoid(x):
    return 0.5 * jnp.tanh(0.5 * x) + 0.5


def _rmsnorm(x, g):
    return x * lax.rsqrt(jnp.mean(x * x, axis=-1, keepdims=True) + RMS_EPS) * g


def _seg_sum(x, bd):
    lanes = bd.shape[0]
    xb = x.astype(BF16)
    return jnp.concatenate([_dot(xb[:, i:i + lanes], bd) for i in range(0, x.shape[1], lanes)],
                           axis=1)


def _shift_rows(x, k, head):
    rows = x.shape[0]
    if k == rows:
        return head
    if k % 8 == 0:
        return jnp.concatenate([head, x[:rows - k]], axis=0)
    rolled = pltpu.roll(x, k, 0)
    row = lax.broadcasted_iota(jnp.int32, x.shape, 0)
    out = rolled
    for j in range(k):
        out = jnp.where(row == j, head[j:j + 1, :], out)
    return out


FRONT_WEIGHTS = ("norm1_g", "w_in", "conv_w", "mu", "w0", "w2", "a0", "a2", "g2", "k_k", "k_a",
                 "bd")
def _front_block(x, cu, cp, s, w, store):
    rows = x.shape[0]
    cw_dim = cu.shape[1]
    rw_dim = w["w0"].shape[1]
    rcols = cp.shape[1]
    win = w["w_in"]
    h = _rmsnorm(x, w["norm1_g"][...]).astype(BF16)

    def proj(c0, c1):
        return _dot(h, win[:, c0:c1])

    def rwkv_piece(c0, c1):
        return proj(3 * cw_dim + c0, 3 * cw_dim + c1)

    def mixed(p, c0, c1):
        prev = _shift_rows(p, s, cp[:, c0:c1])
        return p + (prev - p) * w["mu"][:, c0:c1], p[rows - s:]

    p_lora = rwkv_piece(3 * rw_dim, rcols)
    p_k = rwkv_piece(rw_dim, 2 * rw_dim)
    p_r = rwkv_piece(0, rw_dim)
    lora, last_l = mixed(p_lora, 3 * rw_dim, rcols)
    d_w = w["w2"].shape[0]
    d_a = w["a2"].shape[0]
    pw = lora[:, :d_w]
    pa = lora[:, d_w:d_w + d_a]
    pg = lora[:, d_w + d_a:]
    wl = w["w0"][...] + _dot(jnp.tanh(pw).astype(BF16), w["w2"][...])
    z = -wl
    softplus = jnp.maximum(z, 0.0) + jnp.log(1.0 + jnp.exp(-jnp.abs(z)))
    store("lw", -jnp.exp(-softplus - 0.5))
    a = _sigmoid(w["a0"][...] + _dot(pa.astype(BF16), w["a2"][...]))
    store("a", a)
    store("g", _dot(_sigmoid(pg).astype(BF16), w["g2"][...]))

    p_v = rwkv_piece(2 * rw_dim, 3 * rw_dim)
    k, last_k = mixed(p_k, rw_dim, 2 * rw_dim)
    kk = k * w["k_k"][...]
    store("kk", kk * lax.rsqrt(jnp.maximum(_seg_sum(kk * kk, w["bd"][...]), NORM_EPS * NORM_EPS)))
    store("k", k * (1.0 + (a - 1.0) * w["k_a"][...]))

    c_c = proj(cw_dim, 2 * cw_dim)
    r, last_r = mixed(p_r, 0, rw_dim)
    store("r", r)
    c_x = proj(2 * cw_dim, 3 * cw_dim)
    v, last_v = mixed(p_v, 2 * rw_dim, 3 * rw_dim)
    store("v", v)
    store("new_cp", jnp.concatenate([last_r, last_k, last_v, last_l], axis=1))

    c_b = proj(0, cw_dim)
    u = c_c * c_x
    u1 = _shift_rows(u, s, cu[s:2 * s])
    u2 = _shift_rows(u, 2 * s, cu)
    cw = w["conv_w"][...]
    conv = cw[0:1] * u2 + cw[1:2] * u1 + cw[2:3] * u
    store("new_cu", u[rows - 2 * s:])
    store("yconv", (c_b * conv).astype(BF16))


FRONT_OUTPUTS = ("yconv", "r", "k", "v", "kk", "a", "lw", "g", "new_cu", "new_cp")


def _front_kernel(s, x_ref, conv0_ref, shift0_ref, *refs):
    nw = len(FRONT_WEIGHTS)
    w = dict(zip(FRONT_WEIGHTS, refs[:nw]))
    outs = dict(zip(FRONT_OUTPUTS, refs[nw:nw + len(FRONT_OUTPUTS)]))
    cu_ref, cp_ref = refs[nw + len(FRONT_OUTPUTS):]

    @pl.when(pl.program_id(1) == 0)
    def _():
        cu_ref[...] = conv0_ref[...]
        cp_ref[...] = shift0_ref[...]

    carry = {}

    def store(name, value):
        outs[name][...] = value
        carry[name] = value

    _front_block(x_ref[...], cu_ref[...], cp_ref[...], s, w, store)
    cu_ref[...] = carry["new_cu"]
    cp_ref[...] = carry["new_cp"]


def _const_spec(shape):
    nd = len(shape)
    return pl.BlockSpec(shape, lambda *_: (0,) * nd, pipeline_mode=pl.Buffered(1))


def _front(x, conv0, shift0, s, tb, wts):
    ns, tt, d = x.shape
    cw_dim = conv0.shape[-1]
    rw_dim = wts["w0"].shape[-1]
    rcols = shift0.shape[-1]
    nblk = tt // tb
    row_spec = lambda w: pl.BlockSpec((None, tb, w), lambda b, i: (b, i, 0))
    st_spec = lambda r, w: pl.BlockSpec((None, r, w), lambda b, i: (b, 0, 0))
    w_in = [wts[n] for n in FRONT_WEIGHTS]
    out_shapes = ([jax.ShapeDtypeStruct((ns, tt, cw_dim), BF16)]
                  + [jax.ShapeDtypeStruct((ns, tt, rw_dim), F32)] * 7
                  + [jax.ShapeDtypeStruct((ns, 2 * s, cw_dim), F32),
                     jax.ShapeDtypeStruct((ns, s, rcols), F32)])
    out_specs = ([row_spec(cw_dim)] + [row_spec(rw_dim)] * 7
                 + [st_spec(2 * s, cw_dim), st_spec(s, rcols)])
    return pl.pallas_call(
        functools.partial(_front_kernel, s),
        grid=(ns, nblk),
        in_specs=[row_spec(d), st_spec(2 * s, cw_dim), st_spec(s, rcols)]
        + [_const_spec(w.shape) for w in w_in],
        out_specs=out_specs,
        out_shape=out_shapes,
        scratch_shapes=[pltpu.VMEM((2 * s, cw_dim), F32), pltpu.VMEM((s, rcols), F32)],
        compiler_params=pltpu.CompilerParams(
            dimension_semantics=("arbitrary", "arbitrary"), vmem_limit_bytes=VMEM_LIMIT),
        name="front",
    )(x, conv0, shift0, *w_in)


def _back_kernel(x_ref, yconv_ref, y_ref, r_ref, k_ref, v_ref, g_ref,
                 rk_ref, lg_ref, lb_ref, wo_ref, n2_ref, f1_ref, f2_ref, nf_ref, bd_ref,
                 out_ref):
    bd = bd_ref[...]
    y = y_ref[...]
    inv_n = 1.0 / HEAD
    mean = _seg_sum(y, bd) * inv_n
    dlt = y - mean
    var = _seg_sum(dlt * dlt, bd) * inv_n
    yn = dlt * lax.rsqrt(var + GN_EPS) * lg_ref[...] + lb_ref[...]
    v = v_ref[...]
    bonus = _seg_sum(r_ref[...] * k_ref[...] * rk_ref[...], bd) * v
    y_rwkv = ((yn + bonus) * g_ref[...]).astype(BF16)
    cw_dim = yconv_ref.shape[-1]
    x = x_ref[...]
    x1 = x + _dot(yconv_ref[...], wo_ref[:cw_dim, :]) + _dot(y_rwkv, wo_ref[cw_dim:, :])
    h2 = _rmsnorm(x1, n2_ref[...]).astype(BF16)
    f = jnp.maximum(_dot(h2, f1_ref[...]), 0.0)
    x2 = x1 + _dot((f * f).astype(BF16), f2_ref[...])
    out_ref[...] = _rmsnorm(x2, nf_ref[...])


def _back(x, yconv, y, r, k, v, g, tb, wts):
    ns, tt, d = x.shape
    nblk = tt // tb
    row_spec = lambda w: pl.BlockSpec((None, tb, w), lambda b, i: (b, i, 0))
    names = ["r_k", "lnx_g", "lnx_b", "w_out", "norm2_g", "w_ff1", "w_ff2", "normf_g", "bd"]
    w_in = [wts[n] for n in names]
    acts = [x, yconv, y, r, k, v, g]
    return pl.pallas_call(
        _back_kernel,
        grid=(ns, nblk),
        in_specs=[row_spec(a.shape[-1]) for a in acts] + [_const_spec(w.shape) for w in w_in],
        out_specs=row_spec(d),
        out_shape=jax.ShapeDtypeStruct((ns, tt, d), F32),
        compiler_params=pltpu.CompilerParams(
            dimension_semantics=("arbitrary", "arbitrary"), vmem_limit_bytes=VMEM_LIMIT),
        name="back",
    )(*acts, *w_in)


def _mm(a, b):
    return jnp.dot(a.astype(BF16), b.astype(BF16), preferred_element_type=F32)


def _mm_nt(a, b):
    return lax.dot_general(a.astype(BF16), b.astype(BF16), (((1,), (1,)), ((), ())),
                           preferred_element_type=F32)


def _cumsum_rows(x):
    rows = x.shape[0]
    row = lax.broadcasted_iota(jnp.int32, x.shape, 0)
    sh = 1
    while sh < rows:
        x = x + jnp.where(row >= sh, pltpu.roll(x, sh, 0), 0.0)
        sh *= 2
    return x


def _scan_masks(c):
    lanes = GROUP * HEAD
    gc = GROUP * c
    i0 = lambda shape: lax.broadcasted_iota(jnp.int32, shape, 0)
    i1 = lambda shape: lax.broadcasted_iota(jnp.int32, shape, 1)
    col_t = i1((c, gc)) % c
    row_t = i0((c, gc))
    level = []
    s = 1
    while s < c:
        level.append((row_t // (2 * s) == col_t // (2 * s)) & (row_t % (2 * s) >= s)
                     & (col_t % (2 * s) < s))
        s *= 2
    return dict(
        level=level,
        lane_head=(i0((gc, lanes)) // c) == (i1((gc, lanes)) // HEAD),
        strict=col_t < i0((c, gc)),
        incl=col_t <= i0((c, gc)),
        bd_cc=(i0((gc, gc)) // c) == (i1((gc, gc)) // c),
        bd_ll=(i0((lanes, lanes)) // HEAD) == (i1((lanes, lanes)) // HEAD),
    )


def _chunk_precompute(load, masks):
    lw, kk, a, r, k, v = load()
    c = lw.shape[0]
    assert c == HEAD, "side-by-side forms share lane offsets only when CHUNK == HEAD"

    def bd_rows(m):
        return jnp.where(masks["lane_head"], jnp.concatenate([m] * GROUP, axis=0), 0.0)

    def bd_mat(m):
        return jnp.where(masks["bd_cc"], jnp.concatenate([m] * GROUP, axis=0), 0.0)

    def heads_t(m):
        mt = jnp.transpose(m)
        return jnp.concatenate([mt[h * HEAD:(h + 1) * HEAD, :] for h in range(GROUP)], axis=1)

    cum = _cumsum_rows(lw)
    cum_last = cum[c - 1:c, :]
    beta = kk * a
    alpha_t = -kk * jnp.exp(cum - lw)
    r_t = r * jnp.exp(cum)
    inv = jnp.exp(-cum)
    rem = jnp.exp(cum_last - cum)
    ar = jnp.concatenate([alpha_t, r_t], axis=0)
    g_b = _mm_nt(ar, bd_rows(beta * inv))
    g_k = _mm_nt(ar, bd_rows(k * inv))
    yield
    a_ab = jnp.where(masks["strict"], g_b[:c], 0.0)
    a_rb = jnp.where(masks["incl"], g_b[c:], 0.0)
    a_ak = jnp.where(masks["strict"], g_k[:c], 0.0)
    a_rk = jnp.where(masks["incl"], g_k[c:], 0.0)

    b_t = heads_t(beta * rem)
    k_t = heads_t(k * rem)
    res = _mm(jnp.concatenate([a_ak, a_rk, k_t], axis=0), bd_rows(v))
    yield
    akv, rkv, nh_k = res[:c], res[c:2 * c], res[2 * c:]

    tm, cm = None, a_ab
    n_lvl = len(masks["level"])
    for lvl, corner in enumerate(masks["level"]):
        m_s = jnp.where(corner, cm, 0.0)
        rhs = bd_mat(m_s)
        if lvl == 0:
            res = _mm(cm, rhs)
            yield
            tm, cm = m_s, cm + res
        elif lvl < n_lvl - 1:
            res = _mm(jnp.concatenate([tm, cm], axis=0), rhs)
            yield
            tm, cm = tm + m_s + res[:c], cm + res[c:]
        else:
            res = _mm(tm, rhs)
            yield
            tm = tm + m_s + res

    wa = alpha_t + _mm(tm, bd_rows(alpha_t))
    uv = akv + _mm(tm, bd_rows(akv))
    yield
    resid = akv - uv + _mm(a_ab, bd_rows(uv))
    yield
    uv = uv + resid + _mm(tm, bd_rows(resid))
    yield
    lhs = jnp.concatenate([a_rb, b_t], axis=0)
    res_w = _mm(lhs, bd_rows(wa))
    res_u = _mm(lhs, bd_rows(uv))
    p_col = jnp.transpose(jnp.broadcast_to(jnp.exp(cum_last), (8, cum.shape[1])))[:, 0:1]
    ph = jnp.concatenate([jnp.broadcast_to(p_col[h * HEAD:(h + 1) * HEAD, :], (HEAD, c))
                          for h in range(GROUP)], axis=1)
    yield
    rq = r_t + res_w[:c]
    yv = rkv + res_u[:c]
    mh = res_w[c:]
    nh = res_u[c:] + nh_k
    return rq, yv, mh, nh, ph


def _run_staggered(gens_by_chunk, skew, on_chunk_done):
    n = len(gens_by_chunk)
    results = [[None] * len(gens) for gens in gens_by_chunk]
    live = {(ci, j) for ci, gens in enumerate(gens_by_chunk) for j in range(len(gens))}
    n_done = 0
    rnd = 0
    while live:
        for ci in range(min(n, rnd // skew + 1)):
            for j, gen in enumerate(gens_by_chunk[ci]):
                if (ci, j) in live:
                    try:
                        next(gen)
                    except StopIteration as stop:
                        results[ci][j] = stop.value
                        live.discard((ci, j))
        while n_done < n and not any(ci == n_done for ci, _ in live):
            on_chunk_done(n_done, results[n_done])
            n_done += 1
        rnd += 1


SCAN_OPERANDS = ("lw", "kk", "a", "r", "k", "v")


def _pscan_kernel(lw_ref, kk_ref, a_ref, r_ref, k_ref, v_ref, y_ref, zout_ref, z_ref):
    @pl.when(pl.program_id(1) == 0)
    def _():
        z_ref[...] = jnp.zeros_like(z_ref)

    refs = (lw_ref, kk_ref, a_ref, r_ref, k_ref, v_ref)
    lanes = GROUP * HEAD
    n_groups = kk_ref.shape[1] // lanes
    n_chunks = kk_ref.shape[0] // CHUNK
    masks = _scan_masks(CHUNK)

    def loader(ci, g):
        rows = slice(ci * CHUNK, (ci + 1) * CHUNK)
        sl = slice(g * lanes, (g + 1) * lanes)
        return lambda: tuple(ref[rows, sl] for ref in refs)

    gens = [[_chunk_precompute(loader(ci, g), masks) for g in range(n_groups)]
            for ci in range(n_chunks)]
    zs = [z_ref[g] for g in range(n_groups)]

    def apply_chunk(ci, pre):
        res = [_mm(jnp.concatenate([pre[g][0], pre[g][2]], axis=0),
                   jnp.where(masks["bd_ll"], jnp.concatenate([zs[g]] * GROUP, axis=0), 0.0))
               for g in range(n_groups)]
        for g in range(n_groups):
            _, yv, _, nh, ph = pre[g]
            y_ref[ci * CHUNK:(ci + 1) * CHUNK, g * lanes:(g + 1) * lanes] = res[g][:CHUNK] + yv
            zs[g] = ph * zs[g] + res[g][CHUNK:] + nh

    _run_staggered(gens, SCAN_SKEW, apply_chunk)
    for g in range(n_groups):
        z_ref[g] = zs[g]
        zout_ref[g] = zs[g]


def _prompt_scan(lw, kk, a, r, k, v, tb):
    nb, t, w = kk.shape
    lanes = GROUP * HEAD
    ng = w // lanes
    spec = pl.BlockSpec((None, tb, w), lambda b, i: (b, i, 0))
    return pl.pallas_call(
        _pscan_kernel,
        grid=(nb, t // tb),
        in_specs=[spec] * 6,
        out_specs=[spec, pl.BlockSpec((None, ng, HEAD, lanes), lambda b, i: (b, 0, 0, 0))],
        out_shape=[jax.ShapeDtypeStruct((nb, t, w), F32),
                   jax.ShapeDtypeStruct((nb, ng, HEAD, lanes), F32)],
        scratch_shapes=[pltpu.VMEM((ng, HEAD, lanes), F32)],
        compiler_params=pltpu.CompilerParams(
            dimension_semantics=("arbitrary", "arbitrary"), vmem_limit_bytes=VMEM_LIMIT),
        name="prompt_scan",
    )(lw, kk, a, r, k, v)


LANE_SLAB = 128


def _sscan_kernel(kk_ref, a_ref, lw_ref, k_ref, r_ref, v_ref, s_in_ref, y_ref, s_out_ref,
                  fm_ref, yt_ref):
    n_t = kk_ref.shape[0]
    heads, n_k = s_in_ref.shape[:2]
    shape = s_in_ref.shape[2:]
    i_kk, i_kka, i_w, i_k, i_r, i_v = range(6)
    for t in range(n_t):
        kk_t = jnp.transpose(kk_ref[t])
        fm_ref[i_kk, t] = kk_t
        fm_ref[i_kka, t] = kk_t * jnp.transpose(a_ref[t])
        fm_ref[i_w, t] = jnp.exp(jnp.transpose(lw_ref[t]))
        fm_ref[i_k, t] = jnp.transpose(k_ref[t])
        fm_ref[i_r, t] = jnp.transpose(r_ref[t])
        fm_ref[i_v, t] = jnp.transpose(v_ref[t])

    zeros = jnp.zeros(shape, F32)
    for hd in range(heads):
        off = hd * HEAD

        def rows8(i, t, k8, off=off):
            return fm_ref[i, t, pl.ds(pl.multiple_of(off + k8 * 8, 8), 8), :]

        def first_sk(k8, acc, hd=hd, rows8=rows8):
            kk8 = rows8(i_kk, 0, k8)
            for j in range(8):
                acc = acc + s_in_ref[hd, k8 * 8 + j] * kk8[j:j + 1, :]
            return acc

        sk = lax.fori_loop(0, n_k // 8, first_sk, zeros)
        for t in range(n_t):
            src = s_in_ref if t == 0 else s_out_ref
            vt = fm_ref[i_v, t, off:off + HEAD, :]
            last = t == n_t - 1

            def step(k8, carry, hd=hd, rows8=rows8, t=t, src=src, vt=vt, sk=sk, last=last):
                acc, sk_next = carry
                kka8 = rows8(i_kka, t, k8)
                w8 = rows8(i_w, t, k8)
                k8v = rows8(i_k, t, k8)
                r8 = rows8(i_r, t, k8)
                kkn8 = None if last else rows8(i_kk, t + 1, k8)
                for j in range(8):
                    snew = (src[hd, k8 * 8 + j] * w8[j:j + 1, :] - sk * kka8[j:j + 1, :]
                            + vt * k8v[j:j + 1, :])
                    s_out_ref[hd, k8 * 8 + j] = snew
                    acc = acc + snew * r8[j:j + 1, :]
                    if not last:
                        sk_next = sk_next + snew * kkn8[j:j + 1, :]
                return acc, sk_next

            y, sk = lax.fori_loop(0, n_k // 8, step, (zeros, zeros))
            yt_ref[t, off:off + HEAD, :] = y
    for t in range(n_t):
        y_ref[t] = jnp.transpose(yt_ref[t])


def _sample_scan(kk, a, lw, k, r, v, state):
    n_t, nb, w = kk.shape
    heads = LANE_SLAB // HEAD
    spec = pl.BlockSpec((n_t, nb, LANE_SLAB), lambda i: (0, 0, i))
    sspec = pl.BlockSpec((heads, HEAD, HEAD, nb), lambda i: (i, 0, 0, 0))
    return pl.pallas_call(
        _sscan_kernel,
        grid=(w // LANE_SLAB,),
        in_specs=[spec] * 6 + [sspec],
        out_specs=[spec, sspec],
        out_shape=[jax.ShapeDtypeStruct((n_t, nb, w), F32),
                   jax.ShapeDtypeStruct(state.shape, F32)],
        scratch_shapes=[pltpu.VMEM((6, n_t, LANE_SLAB, nb), F32),
                        pltpu.VMEM((n_t, LANE_SLAB, nb), F32)],
        compiler_params=pltpu.CompilerParams(
            dimension_semantics=("arbitrary",), vmem_limit_bytes=VMEM_LIMIT),
        name="sample_scan",
    )(kk, a, lw, k, r, v, state)


def kernel(x_prompt, x_sample, state_conv, state_shift, state_wkv, norm1_g, w_in, conv_w, mu, w0,
           w2, a0, a2, g2, k_k, k_a, r_k, lnx_g, lnx_b, w_out, norm2_g, w_ff1, w_ff2, normf_g):
    depth = w_in.shape[0]
    assert depth == 1, "single-layer step only"
    bp, seq, d = x_prompt.shape
    bs, dseq, _ = x_sample.shape
    cw_dim = conv_w.shape[-1]
    rw_dim = w0.shape[-1]
    rcols = mu.shape[-1]
    nh = rw_dim // HEAD
    row = lambda t: t[0].reshape(1, -1)
    seg = jnp.arange(GROUP * HEAD) // HEAD
    wts = {
        "norm1_g": row(norm1_g), "w_in": w_in[0].astype(BF16), "conv_w": conv_w[0],
        "mu": row(mu), "w0": row(w0), "w2": w2[0].astype(BF16), "a0": row(a0),
        "a2": a2[0].astype(BF16), "g2": g2[0].astype(BF16), "k_k": row(k_k), "k_a": row(k_a),
        "r_k": row(r_k), "lnx_g": row(lnx_g), "lnx_b": row(lnx_b),
        "w_out": w_out[0].astype(BF16), "norm2_g": row(norm2_g),
        "w_ff1": w_ff1[0].astype(BF16), "w_ff2": w_ff2[0].astype(BF16),
        "normf_g": normf_g.reshape(1, -1),
        "bd": (seg[:, None] == seg[None, :]).astype(BF16),
    }

    zc = jnp.zeros((bp, 2, cw_dim), F32)
    zs = jnp.zeros((bp, 1, rcols), F32)
    yconv, r, k, v, kk, a, lw, g, nconv_p, nshift_p = _front(x_prompt, zc, zs, 1, FRONT_ROWS, wts)
    y, z = _prompt_scan(lw, kk, a, r, k, v, SCAN_ROWS)
    y_prompt = _back(x_prompt, yconv, y, r, k, v, g, BACK_ROWS, wts)
    ng = rw_dim // (GROUP * HEAD)
    zb = z.reshape(bp, ng, HEAD, GROUP, HEAD)
    wkv_p = jnp.transpose(zb, (0, 1, 3, 4, 2)).reshape(bp, nh, HEAD, HEAD)

    xs = jnp.transpose(x_sample, (1, 0, 2)).reshape(1, dseq * bs, d)
    c0 = jnp.transpose(state_conv[0], (1, 0, 2)).reshape(1, 2 * bs, cw_dim)
    s0 = state_shift[0].reshape(1, bs, rcols)
    tbs = 4 * bs
    yconv, r, k, v, kk, a, lw, g, nconv_s, nshift_s = _front(xs, c0, s0, bs, tbs, wts)
    tm = lambda t: t.reshape(dseq, bs, rw_dim)
    st = jnp.transpose(state_wkv[0], (1, 3, 2, 0))
    y_tm, st_new = _sample_scan(tm(kk), tm(a), tm(lw), tm(k), tm(r), tm(v), st)
    y = y_tm.reshape(1, dseq * bs, rw_dim)
    ys = _back(xs, yconv, y, r, k, v, g, tbs, wts)
    y_sample = jnp.transpose(ys.reshape(dseq, bs, d), (1, 0, 2))
    wkv_s = jnp.transpose(st_new, (3, 0, 2, 1))
    nconv_s = jnp.transpose(nconv_s.reshape(2, bs, cw_dim), (1, 0, 2))

    return (y_prompt, y_sample,
            nconv_p[None], nshift_p.reshape(1, bp, rcols), wkv_p[None],
            nconv_s[None], nshift_s.reshape(1, bs, rcols), wkv_s[None])
```

```python
import functools

import jax
import jax.numpy as jnp
from jax import lax
from jax.experimental import pallas as pl
from jax.experimental.pallas import tpu as pltpu

F32 = jnp.float32
BF16 = jnp.bfloat16

HEAD = 64
RMS_EPS = 1e-6
GN_EPS = 64e-5
NORM_EPS = 1e-12
VMEM_LIMIT = 56 * 1024 * 1024
CHUNK = 64
GROUP = 4
SCAN_SKEW = 1
FRONT_ROWS = 512
SCAN_ROWS = 256
BACK_ROWS = 512


def _dot(a, b):
    return jnp.dot(a, b, preferred_element_type=F32)


def _sigmoid(x):
    return 0.5 * jnp.tanh(0.5 * x) + 0.5


def _rmsnorm(x, g):
    return x * lax.rsqrt(jnp.mean(x * x, axis=-1, keepdims=True) + RMS_EPS) * g


def _seg_sum(x, bd):
    lanes = bd.shape[0]
    xb = x.astype(BF16)
    return jnp.concatenate([_dot(xb[:, i:i + lanes], bd) for i in range(0, x.shape[1], lanes)],
                           axis=1)


def _shift_rows(x, k, head):
    rows = x.shape[0]
    if k == rows:
        return head
    if k % 8 == 0:
        return jnp.concatenate([head, x[:rows - k]], axis=0)
    rolled = pltpu.roll(x, k, 0)
    row = lax.broadcasted_iota(jnp.int32, x.shape, 0)
    out = rolled
    for j in range(k):
        out = jnp.where(row == j, head[j:j + 1, :], out)
    return out


FRONT_WEIGHTS = ("norm1_g", "w_in", "conv_w", "mu", "w0", "w2", "a0", "a2", "g2", "k_k", "k_a",
                 "bd")
def _front_block(x, cu, cp, s, w, store):
    rows = x.shape[0]
    cw_dim = cu.shape[1]
    rw_dim = w["w0"].shape[1]
    rcols = cp.shape[1]
    win = w["w_in"]
    h = _rmsnorm(x, w["norm1_g"][...]).astype(BF16)

    def proj(c0, c1):
        return _dot(h, win[:, c0:c1])

    def rwkv_piece(c0, c1):
        return proj(3 * cw_dim + c0, 3 * cw_dim + c1)

    def mixed(p, c0, c1):
        prev = _shift_rows(p, s, cp[:, c0:c1])
        return p + (prev - p) * w["mu"][:, c0:c1], p[rows - s:]

    p_lora = rwkv_piece(3 * rw_dim, rcols)
    p_k = rwkv_piece(rw_dim, 2 * rw_dim)
    p_r = rwkv_piece(0, rw_dim)
    p_v = rwkv_piece(2 * rw_dim, 3 * rw_dim)
    lora, last_l = mixed(p_lora, 3 * rw_dim, rcols)
    d_w = w["w2"].shape[0]
    d_a = w["a2"].shape[0]
    pw = lora[:, :d_w]
    pa = lora[:, d_w:d_w + d_a]
    pg = lora[:, d_w + d_a:]
    wl = w["w0"][...] + _dot(jnp.tanh(pw).astype(BF16), w["w2"][...])
    z = -wl
    softplus = jnp.maximum(z, 0.0) + jnp.log(1.0 + jnp.exp(-jnp.abs(z)))
    store("lw", -jnp.exp(-softplus - 0.5))
    a = _sigmoid(w["a0"][...] + _dot(pa.astype(BF16), w["a2"][...]))
    store("a", a)
    store("g", _dot(_sigmoid(pg).astype(BF16), w["g2"][...]))

    c_c = proj(cw_dim, 2 * cw_dim)
    k, last_k = mixed(p_k, rw_dim, 2 * rw_dim)
    kk = k * w["k_k"][...]
    store("kk", kk * lax.rsqrt(jnp.maximum(_seg_sum(kk * kk, w["bd"][...]), NORM_EPS * NORM_EPS)))
    store("k", k * (1.0 + (a - 1.0) * w["k_a"][...]))

    c_x = proj(2 * cw_dim, 3 * cw_dim)
    r, last_r = mixed(p_r, 0, rw_dim)
    store("r", r)
    c_b = proj(0, cw_dim)
    v, last_v = mixed(p_v, 2 * rw_dim, 3 * rw_dim)
    store("v", v)
    store("new_cp", jnp.concatenate([last_r, last_k, last_v, last_l], axis=1))

    u = c_c * c_x
    u1 = _shift_rows(u, s, cu[s:2 * s])
    u2 = _shift_rows(u, 2 * s, cu)
    cw = w["conv_w"][...]
    conv = cw[0:1] * u2 + cw[1:2] * u1 + cw[2:3] * u
    store("new_cu", u[rows - 2 * s:])
    store("yconv", (c_b * conv).astype(BF16))


FRONT_OUTPUTS = ("yconv", "r", "k", "v", "kk", "a", "lw", "g", "new_cu", "new_cp")


def _front_kernel(s, x_ref, conv0_ref, shift0_ref, *refs):
    nw = len(FRONT_WEIGHTS)
    w = dict(zip(FRONT_WEIGHTS, refs[:nw]))
    outs = dict(zip(FRONT_OUTPUTS, refs[nw:nw + len(FRONT_OUTPUTS)]))
    cu_ref, cp_ref = refs[nw + len(FRONT_OUTPUTS):]

    @pl.when(pl.program_id(1) == 0)
    def _():
        cu_ref[...] = conv0_ref[...]
        cp_ref[...] = shift0_ref[...]

    carry = {}

    def store(name, value):
        outs[name][...] = value
        carry[name] = value

    _front_block(x_ref[...], cu_ref[...], cp_ref[...], s, w, store)
    cu_ref[...] = carry["new_cu"]
    cp_ref[...] = carry["new_cp"]


def _const_spec(shape):
    nd = len(shape)
    return pl.BlockSpec(shape, lambda *_: (0,) * nd, pipeline_mode=pl.Buffered(1))


def _front(x, conv0, shift0, s, tb, wts):
    ns, tt, d = x.shape
    cw_dim = conv0.shape[-1]
    rw_dim = wts["w0"].shape[-1]
    rcols = shift0.shape[-1]
    nblk = tt // tb
    row_spec = lambda w: pl.BlockSpec((None, tb, w), lambda b, i: (b, i, 0))
    st_spec = lambda r, w: pl.BlockSpec((None, r, w), lambda b, i: (b, 0, 0))
    w_in = [wts[n] for n in FRONT_WEIGHTS]
    out_shapes = ([jax.ShapeDtypeStruct((ns, tt, cw_dim), BF16)]
                  + [jax.ShapeDtypeStruct((ns, tt, rw_dim), F32)] * 7
                  + [jax.ShapeDtypeStruct((ns, 2 * s, cw_dim), F32),
                     jax.ShapeDtypeStruct((ns, s, rcols), F32)])
    out_specs = ([row_spec(cw_dim)] + [row_spec(rw_dim)] * 7
                 + [st_spec(2 * s, cw_dim), st_spec(s, rcols)])
    return pl.pallas_call(
        functools.partial(_front_kernel, s),
        grid=(ns, nblk),
        in_specs=[row_spec(d), st_spec(2 * s, cw_dim), st_spec(s, rcols)]
        + [_const_spec(w.shape) for w in w_in],
        out_specs=out_specs,
        out_shape=out_shapes,
        scratch_shapes=[pltpu.VMEM((2 * s, cw_dim), F32), pltpu.VMEM((s, rcols), F32)],
        compiler_params=pltpu.CompilerParams(
            dimension_semantics=("arbitrary", "arbitrary"), vmem_limit_bytes=VMEM_LIMIT),
        name="front",
    )(x, conv0, shift0, *w_in)


def _back_kernel(x_ref, yconv_ref, y_ref, r_ref, k_ref, v_ref, g_ref,
                 rk_ref, lg_ref, lb_ref, wo_ref, n2_ref, f1_ref, f2_ref, nf_ref, bd_ref,
                 out_ref):
    bd = bd_ref[...]
    y = y_ref[...]
    inv_n = 1.0 / HEAD
    mean = _seg_sum(y, bd) * inv_n
    dlt = y - mean
    var = _seg_sum(dlt * dlt, bd) * inv_n
    yn = dlt * lax.rsqrt(var + GN_EPS) * lg_ref[...] + lb_ref[...]
    v = v_ref[...]
    bonus = _seg_sum(r_ref[...] * k_ref[...] * rk_ref[...], bd) * v
    y_rwkv = ((yn + bonus) * g_ref[...]).astype(BF16)
    cw_dim = yconv_ref.shape[-1]
    x = x_ref[...]
    x1 = x + _dot(yconv_ref[...], wo_ref[:cw_dim, :]) + _dot(y_rwkv, wo_ref[cw_dim:, :])
    h2 = _rmsnorm(x1, n2_ref[...]).astype(BF16)
    f = jnp.maximum(_dot(h2, f1_ref[...]), 0.0)
    x2 = x1 + _dot((f * f).astype(BF16), f2_ref[...])
    out_ref[...] = _rmsnorm(x2, nf_ref[...])


def _back(x, yconv, y, r, k, v, g, tb, wts):
    ns, tt, d = x.shape
    nblk = tt // tb
    row_spec = lambda w: pl.BlockSpec((None, tb, w), lambda b, i: (b, i, 0))
    names = ["r_k", "lnx_g", "lnx_b", "w_out", "norm2_g", "w_ff1", "w_ff2", "normf_g", "bd"]
    w_in = [wts[n] for n in names]
    acts = [x, yconv, y, r, k, v, g]
    return pl.pallas_call(
        _back_kernel,
        grid=(ns, nblk),
        in_specs=[row_spec(a.shape[-1]) for a in acts] + [_const_spec(w.shape) for w in w_in],
        out_specs=row_spec(d),
        out_shape=jax.ShapeDtypeStruct((ns, tt, d), F32),
        compiler_params=pltpu.CompilerParams(
            dimension_semantics=("arbitrary", "arbitrary"), vmem_limit_bytes=VMEM_LIMIT),
        name="back",
    )(*acts, *w_in)


def _mm(a, b):
    return jnp.dot(a.astype(BF16), b.astype(BF16), preferred_element_type=F32)


def _mm_nt(a, b):
    return lax.dot_general(a.astype(BF16), b.astype(BF16), (((1,), (1,)), ((), ())),
                           preferred_element_type=F32)


def _cumsum_rows(x):
    rows = x.shape[0]
    row = lax.broadcasted_iota(jnp.int32, x.shape, 0)
    sh = 1
    while sh < rows:
        x = x + jnp.where(row >= sh, pltpu.roll(x, sh, 0), 0.0)
        sh *= 2
    return x


def _scan_masks(c):
    lanes = GROUP * HEAD
    gc = GROUP * c
    i0 = lambda shape: lax.broadcasted_iota(jnp.int32, shape, 0)
    i1 = lambda shape: lax.broadcasted_iota(jnp.int32, shape, 1)
    col_t = i1((c, gc)) % c
    row_t = i0((c, gc))
    level = []
    s = 1
    while s < c:
        level.append((row_t // (2 * s) == col_t // (2 * s)) & (row_t % (2 * s) >= s)
                     & (col_t % (2 * s) < s))
        s *= 2
    return dict(
        level=level,
        lane_head=(i0((gc, lanes)) // c) == (i1((gc, lanes)) // HEAD),
        strict=col_t < i0((c, gc)),
        incl=col_t <= i0((c, gc)),
        bd_cc=(i0((gc, gc)) // c) == (i1((gc, gc)) // c),
        bd_ll=(i0((lanes, lanes)) // HEAD) == (i1((lanes, lanes)) // HEAD),
    )


def _chunk_precompute(load, masks):
    lw, kk, a, r, k, v = load()
    c = lw.shape[0]
    assert c == HEAD, "side-by-side forms share lane offsets only when CHUNK == HEAD"

    def bd_rows(m):
        return jnp.where(masks["lane_head"], jnp.concatenate([m] * GROUP, axis=0), 0.0)

    def bd_mat(m):
        return jnp.where(masks["bd_cc"], jnp.concatenate([m] * GROUP, axis=0), 0.0)

    def heads_t(m):
        mt = jnp.transpose(m)
        return jnp.concatenate([mt[h * HEAD:(h + 1) * HEAD, :] for h in range(GROUP)], axis=1)

    cum = _cumsum_rows(lw)
    cum_last = cum[c - 1:c, :]
    beta = kk * a
    alpha_t = -kk * jnp.exp(cum - lw)
    r_t = r * jnp.exp(cum)
    inv = jnp.exp(-cum)
    rem = jnp.exp(cum_last - cum)
    ar = jnp.concatenate([alpha_t, r_t], axis=0)
    g_b = _mm_nt(ar, bd_rows(beta * inv))
    g_k = _mm_nt(ar, bd_rows(k * inv))
    yield
    a_ab = jnp.where(masks["strict"], g_b[:c], 0.0)
    a_rb = jnp.where(masks["incl"], g_b[c:], 0.0)
    a_ak = jnp.where(masks["strict"], g_k[:c], 0.0)
    a_rk = jnp.where(masks["incl"], g_k[c:], 0.0)

    b_t = heads_t(beta * rem)
    k_t = heads_t(k * rem)
    res = _mm(jnp.concatenate([a_ak, a_rk, k_t], axis=0), bd_rows(v))
    yield
    akv, rkv, nh_k = res[:c], res[c:2 * c], res[2 * c:]

    tm, cm = None, a_ab
    n_lvl = len(masks["level"])
    for lvl, corner in enumerate(masks["level"]):
        m_s = jnp.where(corner, cm, 0.0)
        rhs = bd_mat(m_s)
        if lvl == 0:
            res = _mm(cm, rhs)
            yield
            tm, cm = m_s, cm + res
        elif lvl < n_lvl - 1:
            res = _mm(jnp.concatenate([tm, cm], axis=0), rhs)
            yield
            tm, cm = tm + m_s + res[:c], cm + res[c:]
        else:
            res = _mm(tm, rhs)
            yield
            tm = tm + m_s + res

    wa = alpha_t + _mm(tm, bd_rows(alpha_t))
    uv = akv + _mm(tm, bd_rows(akv))
    yield
    resid = akv - uv + _mm(a_ab, bd_rows(uv))
    yield
    uv = uv + resid + _mm(tm, bd_rows(resid))
    yield
    lhs = jnp.concatenate([a_rb, b_t], axis=0)
    res_w = _mm(lhs, bd_rows(wa))
    res_u = _mm(lhs, bd_rows(uv))
    p_col = jnp.transpose(jnp.broadcast_to(jnp.exp(cum_last), (8, cum.shape[1])))[:, 0:1]
    ph = jnp.concatenate([jnp.broadcast_to(p_col[h * HEAD:(h + 1) * HEAD, :], (HEAD, c))
                          for h in range(GROUP)], axis=1)
    yield
    rq = r_t + res_w[:c]
    yv = rkv + res_u[:c]
    mh = res_w[c:]
    nh = res_u[c:] + nh_k
    return rq, yv, mh, nh, ph


def _run_staggered(gens_by_chunk, skew, on_chunk_done):
    n = len(gens_by_chunk)
    results = [[None] * len(gens) for gens in gens_by_chunk]
    live = {(ci, j) for ci, gens in enumerate(gens_by_chunk) for j in range(len(gens))}
    n_done = 0
    rnd = 0
    while live:
        for ci in range(min(n, rnd // skew + 1)):
            for j, gen in enumerate(gens_by_chunk[ci]):
                if (ci, j) in live:
                    try:
                        next(gen)
                    except StopIteration as stop:
                        results[ci][j] = stop.value
                        live.discard((ci, j))
        while n_done < n and not any(ci == n_done for ci, _ in live):
            on_chunk_done(n_done, results[n_done])
            n_done += 1
        rnd += 1


SCAN_OPERANDS = ("lw", "kk", "a", "r", "k", "v")


def _pscan_kernel(lw_ref, kk_ref, a_ref, r_ref, k_ref, v_ref, y_ref, zout_ref, z_ref):
    @pl.when(pl.program_id(1) == 0)
    def _():
        z_ref[...] = jnp.zeros_like(z_ref)

    refs = (lw_ref, kk_ref, a_ref, r_ref, k_ref, v_ref)
    lanes = GROUP * HEAD
    n_groups = kk_ref.shape[1] // lanes
    n_chunks = kk_ref.shape[0] // CHUNK
    masks = _scan_masks(CHUNK)

    def loader(ci, g):
        rows = slice(ci * CHUNK, (ci + 1) * CHUNK)
        sl = slice(g * lanes, (g + 1) * lanes)
        return lambda: tuple(ref[rows, sl] for ref in refs)

    gens = [[_chunk_precompute(loader(ci, g), masks) for g in range(n_groups)]
            for ci in range(n_chunks)]
    zs = [z_ref[g] for g in range(n_groups)]

    def apply_chunk(ci, pre):
        res = [_mm(jnp.concatenate([pre[g][0], pre[g][2]], axis=0),
                   jnp.where(masks["bd_ll"], jnp.concatenate([zs[g]] * GROUP, axis=0), 0.0))
               for g in range(n_groups)]
        for g in range(n_groups):
            _, yv, _, nh, ph = pre[g]
            y_ref[ci * CHUNK:(ci + 1) * CHUNK, g * lanes:(g + 1) * lanes] = res[g][:CHUNK] + yv
            zs[g] = ph * zs[g] + res[g][CHUNK:] + nh

    _run_staggered(gens, SCAN_SKEW, apply_chunk)
    for g in range(n_groups):
        z_ref[g] = zs[g]
        zout_ref[g] = zs[g]


def _prompt_scan(lw, kk, a, r, k, v, tb):
    nb, t, w = kk.shape
    lanes = GROUP * HEAD
    ng = w // lanes
    spec = pl.BlockSpec((None, tb, w), lambda b, i: (b, i, 0))
    return pl.pallas_call(
        _pscan_kernel,
        grid=(nb, t // tb),
        in_specs=[spec] * 6,
        out_specs=[spec, pl.BlockSpec((None, ng, HEAD, lanes), lambda b, i: (b, 0, 0, 0))],
        out_shape=[jax.ShapeDtypeStruct((nb, t, w), F32),
                   jax.ShapeDtypeStruct((nb, ng, HEAD, lanes), F32)],
        scratch_shapes=[pltpu.VMEM((ng, HEAD, lanes), F32)],
        compiler_params=pltpu.CompilerParams(
            dimension_semantics=("arbitrary", "arbitrary"), vmem_limit_bytes=VMEM_LIMIT),
        name="prompt_scan",
    )(lw, kk, a, r, k, v)


LANE_SLAB = 128


def _sscan_kernel(kk_ref, a_ref, lw_ref, k_ref, r_ref, v_ref, s_in_ref, y_ref, s_out_ref,
                  fm_ref, yt_ref):
    n_t = kk_ref.shape[0]
    heads, n_k = s_in_ref.shape[:2]
    shape = s_in_ref.shape[2:]
    i_kk, i_kka, i_w, i_k, i_r, i_v = range(6)
    for t in range(n_t):
        kk_t = jnp.transpose(kk_ref[t])
        fm_ref[i_kk, t] = kk_t
        fm_ref[i_kka, t] = kk_t * jnp.transpose(a_ref[t])
        fm_ref[i_w, t] = jnp.exp(jnp.transpose(lw_ref[t]))
        fm_ref[i_k, t] = jnp.transpose(k_ref[t])
        fm_ref[i_r, t] = jnp.transpose(r_ref[t])
        fm_ref[i_v, t] = jnp.transpose(v_ref[t])

    zeros = jnp.zeros(shape, F32)
    for hd in range(heads):
        off = hd * HEAD

        def rows8(i, t, k8, off=off):
            return fm_ref[i, t, pl.ds(pl.multiple_of(off + k8 * 8, 8), 8), :]

        def first_sk(k8, acc, hd=hd, rows8=rows8):
            kk8 = rows8(i_kk, 0, k8)
            for j in range(8):
                acc = acc + s_in_ref[hd, k8 * 8 + j] * kk8[j:j + 1, :]
            return acc

        sk = lax.fori_loop(0, n_k // 8, first_sk, zeros)
        for t in range(n_t):
            src = s_in_ref if t == 0 else s_out_ref
            vt = fm_ref[i_v, t, off:off + HEAD, :]
            last = t == n_t - 1

            def step(k8, carry, hd=hd, rows8=rows8, t=t, src=src, vt=vt, sk=sk, last=last):
                acc, sk_next = carry
                kka8 = rows8(i_kka, t, k8)
                w8 = rows8(i_w, t, k8)
                k8v = rows8(i_k, t, k8)
                r8 = rows8(i_r, t, k8)
                kkn8 = None if last else rows8(i_kk, t + 1, k8)
                for j in range(8):
                    snew = (src[hd, k8 * 8 + j] * w8[j:j + 1, :] - sk * kka8[j:j + 1, :]
                            + vt * k8v[j:j + 1, :])
                    s_out_ref[hd, k8 * 8 + j] = snew
                    acc = acc + snew * r8[j:j + 1, :]
                    if not last:
                        sk_next = sk_next + snew * kkn8[j:j + 1, :]
                return acc, sk_next

            y, sk = lax.fori_loop(0, n_k // 8, step, (zeros, zeros))
            yt_ref[t, off:off + HEAD, :] = y
    for t in range(n_t):
        y_ref[t] = jnp.transpose(yt_ref[t])


def _sample_scan(kk, a, lw, k, r, v, state):
    n_t, nb, w = kk.shape
    heads = LANE_SLAB // HEAD
    spec = pl.BlockSpec((n_t, nb, LANE_SLAB), lambda i: (0, 0, i))
    sspec = pl.BlockSpec((heads, HEAD, HEAD, nb), lambda i: (i, 0, 0, 0))
    return pl.pallas_call(
        _sscan_kernel,
        grid=(w // LANE_SLAB,),
        in_specs=[spec] * 6 + [sspec],
        out_specs=[spec, sspec],
        out_shape=[jax.ShapeDtypeStruct((n_t, nb, w), F32),
                   jax.ShapeDtypeStruct(state.shape, F32)],
        scratch_shapes=[pltpu.VMEM((6, n_t, LANE_SLAB, nb), F32),
                        pltpu.VMEM((n_t, LANE_SLAB, nb), F32)],
        compiler_params=pltpu.CompilerParams(
            dimension_semantics=("arbitrary",), vmem_limit_bytes=VMEM_LIMIT),
        name="sample_scan",
    )(kk, a, lw, k, r, v, state)


def kernel(x_prompt, x_sample, state_conv, state_shift, state_wkv, norm1_g, w_in, conv_w, mu, w0,
           w2, a0, a2, g2, k_k, k_a, r_k, lnx_g, lnx_b, w_out, norm2_g, w_ff1, w_ff2, normf_g):
    depth = w_in.shape[0]
    assert depth == 1, "single-layer step only"
    bp, seq, d = x_prompt.shape
    bs, dseq, _ = x_sample.shape
    cw_dim = conv_w.shape[-1]
    rw_dim = w0.shape[-1]
    rcols = mu.shape[-1]
    nh = rw_dim // HEAD
    row = lambda t: t[0].reshape(1, -1)
    seg = jnp.arange(GROUP * HEAD) // HEAD
    wts = {
        "norm1_g": row(norm1_g), "w_in": w_in[0].astype(BF16), "conv_w": conv_w[0],
        "mu": row(mu), "w0": row(w0), "w2": w2[0].astype(BF16), "a0": row(a0),
        "a2": a2[0].astype(BF16), "g2": g2[0].astype(BF16), "k_k": row(k_k), "k_a": row(k_a),
        "r_k": row(r_k), "lnx_g": row(lnx_g), "lnx_b": row(lnx_b),
        "w_out": w_out[0].astype(BF16), "norm2_g": row(norm2_g),
        "w_ff1": w_ff1[0].astype(BF16), "w_ff2": w_ff2[0].astype(BF16),
        "normf_g": normf_g.reshape(1, -1),
        "bd": (seg[:, None] == seg[None, :]).astype(BF16),
    }

    zc = jnp.zeros((bp, 2, cw_dim), F32)
    zs = jnp.zeros((bp, 1, rcols), F32)
    yconv, r, k, v, kk, a, lw, g, nconv_p, nshift_p = _front(x_prompt, zc, zs, 1, FRONT_ROWS, wts)
    y, z = _prompt_scan(lw, kk, a, r, k, v, SCAN_ROWS)
    y_prompt = _back(x_prompt, yconv, y, r, k, v, g, BACK_ROWS, wts)
    ng = rw_dim // (GROUP * HEAD)
    zb = z.reshape(bp, ng, HEAD, GROUP, HEAD)
    wkv_p = jnp.transpose(zb, (0, 1, 3, 4, 2)).reshape(bp, nh, HEAD, HEAD)

    xs = jnp.transpose(x_sample, (1, 0, 2)).reshape(1, dseq * bs, d)
    c0 = jnp.transpose(state_conv[0], (1, 0, 2)).reshape(1, 2 * bs, cw_dim)
    s0 = state_shift[0].reshape(1, bs, rcols)
    tbs = 4 * bs
    yconv, r, k, v, kk, a, lw, g, nconv_s, nshift_s = _front(xs, c0, s0, bs, tbs, wts)
    tm = lambda t: t.reshape(dseq, bs, rw_dim)
    st = jnp.transpose(state_wkv[0], (1, 3, 2, 0))
    y_tm, st_new = _sample_scan(tm(kk), tm(a), tm(lw), tm(k), tm(r), tm(v), st)
    y = y_tm.reshape(1, dseq * bs, rw_dim)
    ys = _back(xs, yconv, y, r, k, v, g, tbs, wts)
    y_sample = jnp.transpose(ys.reshape(dseq, bs, d), (1, 0, 2))
    wkv_s = jnp.transpose(st_new, (3, 0, 2, 1))
    nconv_s = jnp.transpose(nconv_s.reshape(2, bs, cw_dim), (1, 0, 2))

    return (y_prompt, y_sample,
            nconv_p[None], nshift_p.reshape(1, bp, rcols), wkv_p[None],
            nconv_s[None], nshift_s.reshape(1, bs, rcols), wkv_s[None])
```

```python
import functools

import jax
import jax.numpy as jnp
from jax import lax
from jax.experimental import pallas as pl
from jax.experimental.pallas import tpu as pltpu

F32 = jnp.float32
BF16 = jnp.bfloat16

HEAD = 64
RMS_EPS = 1e-6
GN_EPS = 64e-5
NORM_EPS = 1e-12
VMEM_LIMIT = 56 * 1024 * 1024
CHUNK = 64
GROUP = 4
SCAN_SKEW = 1
FRONT_ROWS = 512
SCAN_ROWS = 512
BACK_ROWS = 512


def _dot(a, b):
    return jnp.dot(a, b, preferred_element_type=F32)


def _sigmoid(x):
    return 0.5 * jnp.tanh(0.5 * x) + 0.5


def _rmsnorm(x, g):
    return x * lax.rsqrt(jnp.mean(x * x, axis=-1, keepdims=True) + RMS_EPS) * g


def _seg_sum(x, bd):
    lanes = bd.shape[0]
    xb = x.astype(BF16)
    return jnp.concatenate([_dot(xb[:, i:i + lanes], bd) for i in range(0, x.shape[1], lanes)],
                           axis=1)


def _shift_rows(x, k, head):
    rows = x.shape[0]
    if k == rows:
        return head
    if k % 8 == 0:
        return jnp.concatenate([head, x[:rows - k]], axis=0)
    rolled = pltpu.roll(x, k, 0)
    row = lax.broadcasted_iota(jnp.int32, x.shape, 0)
    out = rolled
    for j in range(k):
        out = jnp.where(row == j, head[j:j + 1, :], out)
    return out


FRONT_WEIGHTS = ("norm1_g", "w_in", "conv_w", "mu", "w0", "w2", "a0", "a2", "g2", "k_k", "k_a",
                 "bd")
def _front_block(x, cu, cp, s, w, store):
    rows = x.shape[0]
    cw_dim = cu.shape[1]
    rw_dim = w["w0"].shape[1]
    rcols = cp.shape[1]
    win = w["w_in"]
    h = _rmsnorm(x, w["norm1_g"][...]).astype(BF16)

    def proj(c0, c1):
        return _dot(h, win[:, c0:c1])

    def rwkv_piece(c0, c1):
        return proj(3 * cw_dim + c0, 3 * cw_dim + c1)

    def mixed(p, c0, c1):
        prev = _shift_rows(p, s, cp[:, c0:c1])
        return p + (prev - p) * w["mu"][:, c0:c1], p[rows - s:]

    p_lora = rwkv_piece(3 * rw_dim, rcols)
    p_k = rwkv_piece(rw_dim, 2 * rw_dim)
    p_r = rwkv_piece(0, rw_dim)
    p_v = rwkv_piece(2 * rw_dim, 3 * rw_dim)
    lora, last_l = mixed(p_lora, 3 * rw_dim, rcols)
    d_w = w["w2"].shape[0]
    d_a = w["a2"].shape[0]
    pw = lora[:, :d_w]
    pa = lora[:, d_w:d_w + d_a]
    pg = lora[:, d_w + d_a:]
    wl = w["w0"][...] + _dot(jnp.tanh(pw).astype(BF16), w["w2"][...])
    z = -wl
    softplus = jnp.maximum(z, 0.0) + jnp.log(1.0 + jnp.exp(-jnp.abs(z)))
    store("lw", -jnp.exp(-softplus - 0.5))
    a = _sigmoid(w["a0"][...] + _dot(pa.astype(BF16), w["a2"][...]))
    store("a", a)
    store("g", _dot(_sigmoid(pg).astype(BF16), w["g2"][...]))

    c_c = proj(cw_dim, 2 * cw_dim)
    k, last_k = mixed(p_k, rw_dim, 2 * rw_dim)
    kk = k * w["k_k"][...]
    store("kk", kk * lax.rsqrt(jnp.maximum(_seg_sum(kk * kk, w["bd"][...]), NORM_EPS * NORM_EPS)))
    store("k", k * (1.0 + (a - 1.0) * w["k_a"][...]))

    c_x = proj(2 * cw_dim, 3 * cw_dim)
    r, last_r = mixed(p_r, 0, rw_dim)
    store("r", r)
    c_b = proj(0, cw_dim)
    v, last_v = mixed(p_v, 2 * rw_dim, 3 * rw_dim)
    store("v", v)
    store("new_cp", jnp.concatenate([last_r, last_k, last_v, last_l], axis=1))

    u = c_c * c_x
    u1 = _shift_rows(u, s, cu[s:2 * s])
    u2 = _shift_rows(u, 2 * s, cu)
    cw = w["conv_w"][...]
    conv = cw[0:1] * u2 + cw[1:2] * u1 + cw[2:3] * u
    store("new_cu", u[rows - 2 * s:])
    store("yconv", (c_b * conv).astype(BF16))


FRONT_OUTPUTS = ("yconv", "r", "k", "v", "kk", "a", "lw", "g", "new_cu", "new_cp")


def _front_kernel(s, x_ref, conv0_ref, shift0_ref, *refs):
    nw = len(FRONT_WEIGHTS)
    w = dict(zip(FRONT_WEIGHTS, refs[:nw]))
    outs = dict(zip(FRONT_OUTPUTS, refs[nw:nw + len(FRONT_OUTPUTS)]))
    cu_ref, cp_ref = refs[nw + len(FRONT_OUTPUTS):]

    @pl.when(pl.program_id(1) == 0)
    def _():
        cu_ref[...] = conv0_ref[...]
        cp_ref[...] = shift0_ref[...]

    carry = {}

    def store(name, value):
        outs[name][...] = value
        carry[name] = value

    _front_block(x_ref[...], cu_ref[...], cp_ref[...], s, w, store)
    cu_ref[...] = carry["new_cu"]
    cp_ref[...] = carry["new_cp"]


def _const_spec(shape):
    nd = len(shape)
    return pl.BlockSpec(shape, lambda *_: (0,) * nd, pipeline_mode=pl.Buffered(1))


def _front(x, conv0, shift0, s, tb, wts):
    ns, tt, d = x.shape
    cw_dim = conv0.shape[-1]
    rw_dim = wts["w0"].shape[-1]
    rcols = shift0.shape[-1]
    nblk = tt // tb
    row_spec = lambda w: pl.BlockSpec((None, tb, w), lambda b, i: (b, i, 0))
    st_spec = lambda r, w: pl.BlockSpec((None, r, w), lambda b, i: (b, 0, 0))
    w_in = [wts[n] for n in FRONT_WEIGHTS]
    out_shapes = ([jax.ShapeDtypeStruct((ns, tt, cw_dim), BF16)]
                  + [jax.ShapeDtypeStruct((ns, tt, rw_dim), F32)] * 7
                  + [jax.ShapeDtypeStruct((ns, 2 * s, cw_dim), F32),
                     jax.ShapeDtypeStruct((ns, s, rcols), F32)])
    out_specs = ([row_spec(cw_dim)] + [row_spec(rw_dim)] * 7
                 + [st_spec(2 * s, cw_dim), st_spec(s, rcols)])
    return pl.pallas_call(
        functools.partial(_front_kernel, s),
        grid=(ns, nblk),
        in_specs=[row_spec(d), st_spec(2 * s, cw_dim), st_spec(s, rcols)]
        + [_const_spec(w.shape) for w in w_in],
        out_specs=out_specs,
        out_shape=out_shapes,
        scratch_shapes=[pltpu.VMEM((2 * s, cw_dim), F32), pltpu.VMEM((s, rcols), F32)],
        compiler_params=pltpu.CompilerParams(
            dimension_semantics=("arbitrary", "arbitrary"), vmem_limit_bytes=VMEM_LIMIT),
        name="front",
    )(x, conv0, shift0, *w_in)


def _back_kernel(x_ref, yconv_ref, y_ref, r_ref, k_ref, v_ref, g_ref,
                 rk_ref, lg_ref, lb_ref, wo_ref, n2_ref, f1_ref, f2_ref, nf_ref, bd_ref,
                 out_ref):
    bd = bd_ref[...]
    y = y_ref[...]
    inv_n = 1.0 / HEAD
    mean = _seg_sum(y, bd) * inv_n
    dlt = y - mean
    var = _seg_sum(dlt * dlt, bd) * inv_n
    yn = dlt * lax.rsqrt(var + GN_EPS) * lg_ref[...] + lb_ref[...]
    v = v_ref[...]
    bonus = _seg_sum(r_ref[...] * k_ref[...] * rk_ref[...], bd) * v
    y_rwkv = ((yn + bonus) * g_ref[...]).astype(BF16)
    cw_dim = yconv_ref.shape[-1]
    x = x_ref[...]
    x1 = x + _dot(yconv_ref[...], wo_ref[:cw_dim, :]) + _dot(y_rwkv, wo_ref[cw_dim:, :])
    h2 = _rmsnorm(x1, n2_ref[...]).astype(BF16)
    f = jnp.maximum(_dot(h2, f1_ref[...]), 0.0)
    x2 = x1 + _dot((f * f).astype(BF16), f2_ref[...])
    out_ref[...] = _rmsnorm(x2, nf_ref[...])


def _back(x, yconv, y, r, k, v, g, tb, wts):
    ns, tt, d = x.shape
    nblk = tt // tb
    row_spec = lambda w: pl.BlockSpec((None, tb, w), lambda b, i: (b, i, 0))
    names = ["r_k", "lnx_g", "lnx_b", "w_out", "norm2_g", "w_ff1", "w_ff2", "normf_g", "bd"]
    w_in = [wts[n] for n in names]
    acts = [x, yconv, y, r, k, v, g]
    return pl.pallas_call(
        _back_kernel,
        grid=(ns, nblk),
        in_specs=[row_spec(a.shape[-1]) for a in acts] + [_const_spec(w.shape) for w in w_in],
        out_specs=row_spec(d),
        out_shape=jax.ShapeDtypeStruct((ns, tt, d), F32),
        compiler_params=pltpu.CompilerParams(
            dimension_semantics=("arbitrary", "arbitrary"), vmem_limit_bytes=VMEM_LIMIT),
        name="back",
    )(*acts, *w_in)


def _mm(a, b):
    return jnp.dot(a.astype(BF16), b.astype(BF16), preferred_element_type=F32)


def _mm_nt(a, b):
    return lax.dot_general(a.astype(BF16), b.astype(BF16), (((1,), (1,)), ((), ())),
                           preferred_element_type=F32)


def _cumsum_rows(x):
    rows = x.shape[0]
    row = lax.broadcasted_iota(jnp.int32, x.shape, 0)
    sh = 1
    while sh < rows:
        x = x + jnp.where(row >= sh, pltpu.roll(x, sh, 0), 0.0)
        sh *= 2
    return x


def _scan_masks(c):
    lanes = GROUP * HEAD
    gc = GROUP * c
    i0 = lambda shape: lax.broadcasted_iota(jnp.int32, shape, 0)
    i1 = lambda shape: lax.broadcasted_iota(jnp.int32, shape, 1)
    col_t = i1((c, gc)) % c
    row_t = i0((c, gc))
    level = []
    s = 1
    while s < c:
        level.append((row_t // (2 * s) == col_t // (2 * s)) & (row_t % (2 * s) >= s)
                     & (col_t % (2 * s) < s))
        s *= 2
    return dict(
        level=level,
        lane_head=(i0((gc, lanes)) // c) == (i1((gc, lanes)) // HEAD),
        strict=col_t < i0((c, gc)),
        incl=col_t <= i0((c, gc)),
        bd_cc=(i0((gc, gc)) // c) == (i1((gc, gc)) // c),
        bd_ll=(i0((lanes, lanes)) // HEAD) == (i1((lanes, lanes)) // HEAD),
    )


def _chunk_precompute(load, masks):
    lw, kk, a, r, k, v = load()
    c = lw.shape[0]
    assert c == HEAD, "side-by-side forms share lane offsets only when CHUNK == HEAD"

    def bd_rows(m):
        return jnp.where(masks["lane_head"], jnp.concatenate([m] * GROUP, axis=0), 0.0)

    def bd_mat(m):
        return jnp.where(masks["bd_cc"], jnp.concatenate([m] * GROUP, axis=0), 0.0)

    def heads_t(m):
        mt = jnp.transpose(m)
        return jnp.concatenate([mt[h * HEAD:(h + 1) * HEAD, :] for h in range(GROUP)], axis=1)

    cum = _cumsum_rows(lw)
    cum_last = cum[c - 1:c, :]
    beta = kk * a
    alpha_t = -kk * jnp.exp(cum - lw)
    r_t = r * jnp.exp(cum)
    inv = jnp.exp(-cum)
    rem = jnp.exp(cum_last - cum)
    ar = jnp.concatenate([alpha_t, r_t], axis=0)
    g_b = _mm_nt(ar, bd_rows(beta * inv))
    g_k = _mm_nt(ar, bd_rows(k * inv))
    yield
    a_ab = jnp.where(masks["strict"], g_b[:c], 0.0)
    a_rb = jnp.where(masks["incl"], g_b[c:], 0.0)
    a_ak = jnp.where(masks["strict"], g_k[:c], 0.0)
    a_rk = jnp.where(masks["incl"], g_k[c:], 0.0)

    b_t = heads_t(beta * rem)
    k_t = heads_t(k * rem)
    res = _mm(jnp.concatenate([a_ak, a_rk, k_t], axis=0), bd_rows(v))
    yield
    akv, rkv, nh_k = res[:c], res[c:2 * c], res[2 * c:]

    tm, cm = None, a_ab
    n_lvl = len(masks["level"])
    for lvl, corner in enumerate(masks["level"]):
        m_s = jnp.where(corner, cm, 0.0)
        rhs = bd_mat(m_s)
        if lvl == 0:
            res = _mm(cm, rhs)
            yield
            tm, cm = m_s, cm + res
        elif lvl < n_lvl - 1:
            res = _mm(jnp.concatenate([tm, cm], axis=0), rhs)
            yield
            tm, cm = tm + m_s + res[:c], cm + res[c:]
        else:
            res = _mm(tm, rhs)
            yield
            tm = tm + m_s + res

    wa = alpha_t + _mm(tm, bd_rows(alpha_t))
    uv = akv + _mm(tm, bd_rows(akv))
    yield
    resid = akv - uv + _mm(a_ab, bd_rows(uv))
    yield
    uv = uv + resid + _mm(tm, bd_rows(resid))
    yield
    lhs = jnp.concatenate([a_rb, b_t], axis=0)
    res_w = _mm(lhs, bd_rows(wa))
    res_u = _mm(lhs, bd_rows(uv))
    p_col = jnp.transpose(jnp.broadcast_to(jnp.exp(cum_last), (8, cum.shape[1])))[:, 0:1]
    ph = jnp.concatenate([jnp.broadcast_to(p_col[h * HEAD:(h + 1) * HEAD, :], (HEAD, c))
                          for h in range(GROUP)], axis=1)
    yield
    rq = r_t + res_w[:c]
    yv = rkv + res_u[:c]
    mh = res_w[c:]
    nh = res_u[c:] + nh_k
    return rq, yv, mh, nh, ph


def _run_staggered(gens_by_chunk, skew, on_chunk_done):
    n = len(gens_by_chunk)
    results = [[None] * len(gens) for gens in gens_by_chunk]
    live = {(ci, j) for ci, gens in enumerate(gens_by_chunk) for j in range(len(gens))}
    n_done = 0
    rnd = 0
    while live:
        for ci in range(min(n, rnd // skew + 1)):
            for j, gen in enumerate(gens_by_chunk[ci]):
                if (ci, j) in live:
                    try:
                        next(gen)
                    except StopIteration as stop:
                        results[ci][j] = stop.value
                        live.discard((ci, j))
        while n_done < n and not any(ci == n_done for ci, _ in live):
            on_chunk_done(n_done, results[n_done])
            n_done += 1
        rnd += 1


SCAN_OPERANDS = ("lw", "kk", "a", "r", "k", "v")


def _pscan_kernel(lw_ref, kk_ref, a_ref, r_ref, k_ref, v_ref, y_ref, zout_ref, z_ref):
    @pl.when(pl.program_id(1) == 0)
    def _():
        z_ref[...] = jnp.zeros_like(z_ref)

    refs = (lw_ref, kk_ref, a_ref, r_ref, k_ref, v_ref)
    lanes = GROUP * HEAD
    n_groups = kk_ref.shape[1] // lanes
    n_chunks = kk_ref.shape[0] // CHUNK
    masks = _scan_masks(CHUNK)

    def loader(ci, g):
        rows = slice(ci * CHUNK, (ci + 1) * CHUNK)
        sl = slice(g * lanes, (g + 1) * lanes)
        return lambda: tuple(ref[rows, sl] for ref in refs)

    gens = [[_chunk_precompute(loader(ci, g), masks) for g in range(n_groups)]
            for ci in range(n_chunks)]
    zs = [z_ref[g] for g in range(n_groups)]

    def apply_chunk(ci, pre):
        res = [_mm(jnp.concatenate([pre[g][0], pre[g][2]], axis=0),
                   jnp.where(masks["bd_ll"], jnp.concatenate([zs[g]] * GROUP, axis=0), 0.0))
               for g in range(n_groups)]
        for g in range(n_groups):
            _, yv, _, nh, ph = pre[g]
            y_ref[ci * CHUNK:(ci + 1) * CHUNK, g * lanes:(g + 1) * lanes] = res[g][:CHUNK] + yv
            zs[g] = ph * zs[g] + res[g][CHUNK:] + nh

    _run_staggered(gens, SCAN_SKEW, apply_chunk)
    for g in range(n_groups):
        z_ref[g] = zs[g]
        zout_ref[g] = zs[g]


def _prompt_scan(lw, kk, a, r, k, v, tb):
    nb, t, w = kk.shape
    lanes = GROUP * HEAD
    ng = w // lanes
    spec = pl.BlockSpec((None, tb, w), lambda b, i: (b, i, 0))
    return pl.pallas_call(
        _pscan_kernel,
        grid=(nb, t // tb),
        in_specs=[spec] * 6,
        out_specs=[spec, pl.BlockSpec((None, ng, HEAD, lanes), lambda b, i: (b, 0, 0, 0))],
        out_shape=[jax.ShapeDtypeStruct((nb, t, w), F32),
                   jax.ShapeDtypeStruct((nb, ng, HEAD, lanes), F32)],
        scratch_shapes=[pltpu.VMEM((ng, HEAD, lanes), F32)],
        compiler_params=pltpu.CompilerParams(
            dimension_semantics=("arbitrary", "arbitrary"), vmem_limit_bytes=VMEM_LIMIT),
        name="prompt_scan",
    )(lw, kk, a, r, k, v)


LANE_SLAB = 128


def _sscan_kernel(kk_ref, a_ref, lw_ref, k_ref, r_ref, v_ref, s_in_ref, y_ref, s_out_ref,
                  fm_ref, yt_ref):
    n_t = kk_ref.shape[0]
    heads, n_k = s_in_ref.shape[:2]
    shape = s_in_ref.shape[2:]
    i_kk, i_kka, i_w, i_k, i_r, i_v = range(6)
    for t in range(n_t):
        kk_t = jnp.transpose(kk_ref[t])
        fm_ref[i_kk, t] = kk_t
        fm_ref[i_kka, t] = kk_t * jnp.transpose(a_ref[t])
        fm_ref[i_w, t] = jnp.exp(jnp.transpose(lw_ref[t]))
        fm_ref[i_k, t] = jnp.transpose(k_ref[t])
        fm_ref[i_r, t] = jnp.transpose(r_ref[t])
        fm_ref[i_v, t] = jnp.transpose(v_ref[t])

    zeros = jnp.zeros(shape, F32)
    for hd in range(heads):
        off = hd * HEAD

        def rows8(i, t, k8, off=off):
            return fm_ref[i, t, pl.ds(pl.multiple_of(off + k8 * 8, 8), 8), :]

        def first_sk(k8, acc, hd=hd, rows8=rows8):
            kk8 = rows8(i_kk, 0, k8)
            for j in range(8):
                acc = acc + s_in_ref[hd, k8 * 8 + j] * kk8[j:j + 1, :]
            return acc

        sk = lax.fori_loop(0, n_k // 8, first_sk, zeros)
        for t in range(n_t):
            src = s_in_ref if t == 0 else s_out_ref
            vt = fm_ref[i_v, t, off:off + HEAD, :]
            last = t == n_t - 1

            def step(k8, carry, hd=hd, rows8=rows8, t=t, src=src, vt=vt, sk=sk, last=last):
                acc, sk_next = carry
                kka8 = rows8(i_kka, t, k8)
                w8 = rows8(i_w, t, k8)
                k8v = rows8(i_k, t, k8)
                r8 = rows8(i_r, t, k8)
                kkn8 = None if last else rows8(i_kk, t + 1, k8)
                for j in range(8):
                    snew = (src[hd, k8 * 8 + j] * w8[j:j + 1, :] - sk * kka8[j:j + 1, :]
                            + vt * k8v[j:j + 1, :])
                    s_out_ref[hd, k8 * 8 + j] = snew
                    acc = acc + snew * r8[j:j + 1, :]
                    if not last:
                        sk_next = sk_next + snew * kkn8[j:j + 1, :]
                return acc, sk_next

            y, sk = lax.fori_loop(0, n_k // 8, step, (zeros, zeros))
            yt_ref[t, off:off + HEAD, :] = y
    for t in range(n_t):
        y_ref[t] = jnp.transpose(yt_ref[t])


def _sample_scan(kk, a, lw, k, r, v, state):
    n_t, nb, w = kk.shape
    heads = LANE_SLAB // HEAD
    spec = pl.BlockSpec((n_t, nb, LANE_SLAB), lambda i: (0, 0, i))
    sspec = pl.BlockSpec((heads, HEAD, HEAD, nb), lambda i: (i, 0, 0, 0))
    return pl.pallas_call(
        _sscan_kernel,
        grid=(w // LANE_SLAB,),
        in_specs=[spec] * 6 + [sspec],
        out_specs=[spec, sspec],
        out_shape=[jax.ShapeDtypeStruct((n_t, nb, w), F32),
                   jax.ShapeDtypeStruct(state.shape, F32)],
        scratch_shapes=[pltpu.VMEM((6, n_t, LANE_SLAB, nb), F32),
                        pltpu.VMEM((n_t, LANE_SLAB, nb), F32)],
        compiler_params=pltpu.CompilerParams(
            dimension_semantics=("arbitrary",), vmem_limit_bytes=VMEM_LIMIT),
        name="sample_scan",
    )(kk, a, lw, k, r, v, state)


def kernel(x_prompt, x_sample, state_conv, state_shift, state_wkv, norm1_g, w_in, conv_w, mu, w0,
           w2, a0, a2, g2, k_k, k_a, r_k, lnx_g, lnx_b, w_out, norm2_g, w_ff1, w_ff2, normf_g):
    depth = w_in.shape[0]
    assert depth == 1, "single-layer step only"
    bp, seq, d = x_prompt.shape
    bs, dseq, _ = x_sample.shape
    cw_dim = conv_w.shape[-1]
    rw_dim = w0.shape[-1]
    rcols = mu.shape[-1]
    nh = rw_dim // HEAD
    row = lambda t: t[0].reshape(1, -1)
    seg = jnp.arange(GROUP * HEAD) // HEAD
    wts = {
        "norm1_g": row(norm1_g), "w_in": w_in[0].astype(BF16), "conv_w": conv_w[0],
        "mu": row(mu), "w0": row(w0), "w2": w2[0].astype(BF16), "a0": row(a0),
        "a2": a2[0].astype(BF16), "g2": g2[0].astype(BF16), "k_k": row(k_k), "k_a": row(k_a),
        "r_k": row(r_k), "lnx_g": row(lnx_g), "lnx_b": row(lnx_b),
        "w_out": w_out[0].astype(BF16), "norm2_g": row(norm2_g),
        "w_ff1": w_ff1[0].astype(BF16), "w_ff2": w_ff2[0].astype(BF16),
        "normf_g": normf_g.reshape(1, -1),
        "bd": (seg[:, None] == seg[None, :]).astype(BF16),
    }

    zc = jnp.zeros((bp, 2, cw_dim), F32)
    zs = jnp.zeros((bp, 1, rcols), F32)
    yconv, r, k, v, kk, a, lw, g, nconv_p, nshift_p = _front(x_prompt, zc, zs, 1, FRONT_ROWS, wts)
    y, z = _prompt_scan(lw, kk, a, r, k, v, SCAN_ROWS)
    y_prompt = _back(x_prompt, yconv, y, r, k, v, g, BACK_ROWS, wts)
    ng = rw_dim // (GROUP * HEAD)
    zb = z.reshape(bp, ng, HEAD, GROUP, HEAD)
    wkv_p = jnp.transpose(zb, (0, 1, 3, 4, 2)).reshape(bp, nh, HEAD, HEAD)

    xs = jnp.transpose(x_sample, (1, 0, 2)).reshape(1, dseq * bs, d)
    c0 = jnp.transpose(state_conv[0], (1, 0, 2)).reshape(1, 2 * bs, cw_dim)
    s0 = state_shift[0].reshape(1, bs, rcols)
    tbs = 4 * bs
    yconv, r, k, v, kk, a, lw, g, nconv_s, nshift_s = _front(xs, c0, s0, bs, tbs, wts)
    tm = lambda t: t.reshape(dseq, bs, rw_dim)
    st = jnp.transpose(state_wkv[0], (1, 3, 2, 0))
    y_tm, st_new = _sample_scan(tm(kk), tm(a), tm(lw), tm(k), tm(r), tm(v), st)
    y = y_tm.reshape(1, dseq * bs, rw_dim)
    ys = _back(xs, yconv, y, r, k, v, g, tbs, wts)
    y_sample = jnp.transpose(ys.reshape(dseq, bs, d), (1, 0, 2))
    wkv_s = jnp.transpose(st_new, (3, 0, 2, 1))
    nconv_s = jnp.transpose(nconv_s.reshape(2, bs, cw_dim), (1, 0, 2))

    return (y_prompt, y_sample,
            nconv_p[None], nshift_p.reshape(1, bp, rcols), wkv_p[None],
            nconv_s[None], nshift_s.reshape(1, bs, rcols), wkv_s[None])
```

```python
import functools

import jax
import jax.numpy as jnp
from jax import lax
from jax.experimental import pallas as pl
from jax.experimental.pallas import tpu as pltpu

F32 = jnp.float32
BF16 = jnp.bfloat16

HEAD = 64
RMS_EPS = 1e-6
GN_EPS = 64e-5
NORM_EPS = 1e-12
VMEM_LIMIT = 56 * 1024 * 1024
CHUNK = 64
GROUP = 4
SCAN_SKEW = 1
FRONT_ROWS = 512
SCAN_ROWS = 512
BACK_ROWS = 512


def _dot(a, b):
    return jnp.dot(a, b, preferred_element_type=F32)


def _sigmoid(x):
    return 0.5 * jnp.tanh(0.5 * x) + 0.5


def _rmsnorm(x, g):
    return x * lax.rsqrt(jnp.mean(x * x, axis=-1, keepdims=True) + RMS_EPS) * g


def _seg_sum(x, bd):
    lanes = bd.shape[0]
    xb = x.astype(BF16)
    return jnp.concatenate([_dot(xb[:, i:i + lanes], bd) for i in range(0, x.shape[1], lanes)],
                           axis=1)


def _shift_rows(x, k, head):
    rows = x.shape[0]
    if k == rows:
        return head
    if k % 8 == 0:
        return jnp.concatenate([head, x[:rows - k]], axis=0)
    rolled = pltpu.roll(x, k, 0)
    row = lax.broadcasted_iota(jnp.int32, x.shape, 0)
    out = rolled
    for j in range(k):
        out = jnp.where(row == j, head[j:j + 1, :], out)
    return out


FRONT_WEIGHTS = ("norm1_g", "w_in", "conv_w", "mu", "w0", "w2", "a0", "a2", "g2", "k_k", "k_a",
                 "bd")
def _front_block(x, cu, cp, s, w, store):
    rows = x.shape[0]
    cw_dim = cu.shape[1]
    rw_dim = w["w0"].shape[1]
    rcols = cp.shape[1]
    win = w["w_in"]
    h = _rmsnorm(x, w["norm1_g"][...]).astype(BF16)

    def proj(c0, c1):
        return _dot(h, win[:, c0:c1])

    def rwkv_piece(c0, c1):
        return proj(3 * cw_dim + c0, 3 * cw_dim + c1)

    def mixed(p, c0, c1):
        prev = _shift_rows(p, s, cp[:, c0:c1])
        return p + (prev - p) * w["mu"][:, c0:c1], p[rows - s:]

    p_lora = rwkv_piece(3 * rw_dim, rcols)
    p_k = rwkv_piece(rw_dim, 2 * rw_dim)
    p_r = rwkv_piece(0, rw_dim)
    p_v = rwkv_piece(2 * rw_dim, 3 * rw_dim)
    lora, last_l = mixed(p_lora, 3 * rw_dim, rcols)
    d_w = w["w2"].shape[0]
    d_a = w["a2"].shape[0]
    pw = lora[:, :d_w]
    pa = lora[:, d_w:d_w + d_a]
    pg = lora[:, d_w + d_a:]
    wl = w["w0"][...] + _dot(jnp.tanh(pw).astype(BF16), w["w2"][...])
    z = -wl
    softplus = jnp.maximum(z, 0.0) + jnp.log(1.0 + jnp.exp(-jnp.abs(z)))
    store("lw", -jnp.exp(-softplus - 0.5))
    a = _sigmoid(w["a0"][...] + _dot(pa.astype(BF16), w["a2"][...]))
    store("a", a)
    store("g", _dot(_sigmoid(pg).astype(BF16), w["g2"][...]))

    c_c = proj(cw_dim, 2 * cw_dim)
    k, last_k = mixed(p_k, rw_dim, 2 * rw_dim)
    kk = k * w["k_k"][...]
    store("kk", kk * lax.rsqrt(jnp.maximum(_seg_sum(kk * kk, w["bd"][...]), NORM_EPS * NORM_EPS)))
    store("k", k * (1.0 + (a - 1.0) * w["k_a"][...]))

    c_x = proj(2 * cw_dim, 3 * cw_dim)
    r, last_r = mixed(p_r, 0, rw_dim)
    store("r", r)
    c_b = proj(0, cw_dim)
    v, last_v = mixed(p_v, 2 * rw_dim, 3 * rw_dim)
    store("v", v)
    store("new_cp", jnp.concatenate([last_r, last_k, last_v, last_l], axis=1))

    u = c_c * c_x
    u1 = _shift_rows(u, s, cu[s:2 * s])
    u2 = _shift_rows(u, 2 * s, cu)
    cw = w["conv_w"][...]
    conv = cw[0:1] * u2 + cw[1:2] * u1 + cw[2:3] * u
    store("new_cu", u[rows - 2 * s:])
    store("yconv", (c_b * conv).astype(BF16))


FRONT_OUTPUTS = ("yconv", "r", "k", "v", "kk", "a", "lw", "g", "new_cu", "new_cp")


def _front_kernel(s, x_ref, conv0_ref, shift0_ref, *refs):
    nw = len(FRONT_WEIGHTS)
    w = dict(zip(FRONT_WEIGHTS, refs[:nw]))
    outs = dict(zip(FRONT_OUTPUTS, refs[nw:nw + len(FRONT_OUTPUTS)]))
    cu_ref, cp_ref = refs[nw + len(FRONT_OUTPUTS):]

    @pl.when(pl.program_id(1) == 0)
    def _():
        cu_ref[...] = conv0_ref[...]
        cp_ref[...] = shift0_ref[...]

    carry = {}

    def store(name, value):
        outs[name][...] = value
        carry[name] = value

    _front_block(x_ref[...], cu_ref[...], cp_ref[...], s, w, store)
    cu_ref[...] = carry["new_cu"]
    cp_ref[...] = carry["new_cp"]


def _const_spec(shape):
    nd = len(shape)
    return pl.BlockSpec(shape, lambda *_: (0,) * nd, pipeline_mode=pl.Buffered(1))


def _front(x, conv0, shift0, s, tb, wts):
    ns, tt, d = x.shape
    cw_dim = conv0.shape[-1]
    rw_dim = wts["w0"].shape[-1]
    rcols = shift0.shape[-1]
    nblk = tt // tb
    row_spec = lambda w: pl.BlockSpec((None, tb, w), lambda b, i: (b, i, 0))
    st_spec = lambda r, w: pl.BlockSpec((None, r, w), lambda b, i: (b, 0, 0))
    w_in = [wts[n] for n in FRONT_WEIGHTS]
    out_shapes = ([jax.ShapeDtypeStruct((ns, tt, cw_dim), BF16)]
                  + [jax.ShapeDtypeStruct((ns, tt, rw_dim), F32)] * 7
                  + [jax.ShapeDtypeStruct((ns, 2 * s, cw_dim), F32),
                     jax.ShapeDtypeStruct((ns, s, rcols), F32)])
    out_specs = ([row_spec(cw_dim)] + [row_spec(rw_dim)] * 7
                 + [st_spec(2 * s, cw_dim), st_spec(s, rcols)])
    return pl.pallas_call(
        functools.partial(_front_kernel, s),
        grid=(ns, nblk),
        in_specs=[row_spec(d), st_spec(2 * s, cw_dim), st_spec(s, rcols)]
        + [_const_spec(w.shape) for w in w_in],
        out_specs=out_specs,
        out_shape=out_shapes,
        scratch_shapes=[pltpu.VMEM((2 * s, cw_dim), F32), pltpu.VMEM((s, rcols), F32)],
        compiler_params=pltpu.CompilerParams(
            dimension_semantics=("arbitrary", "arbitrary"), vmem_limit_bytes=VMEM_LIMIT),
        name="front",
    )(x, conv0, shift0, *w_in)


def _back_kernel(x_ref, yconv_ref, y_ref, r_ref, k_ref, v_ref, g_ref,
                 rk_ref, lg_ref, lb_ref, wo_ref, n2_ref, f1_ref, f2_ref, nf_ref, bd_ref,
                 out_ref):
    bd = bd_ref[...]
    y = y_ref[...]
    inv_n = 1.0 / HEAD
    mean = _seg_sum(y, bd) * inv_n
    dlt = y - mean
    var = _seg_sum(dlt * dlt, bd) * inv_n
    yn = dlt * lax.rsqrt(var + GN_EPS) * lg_ref[...] + lb_ref[...]
    v = v_ref[...]
    bonus = _seg_sum(r_ref[...] * k_ref[...] * rk_ref[...], bd) * v
    y_rwkv = ((yn + bonus) * g_ref[...]).astype(BF16)
    cw_dim = yconv_ref.shape[-1]
    x = x_ref[...]
    x1 = x + _dot(yconv_ref[...], wo_ref[:cw_dim, :]) + _dot(y_rwkv, wo_ref[cw_dim:, :])
    h2 = _rmsnorm(x1, n2_ref[...]).astype(BF16)
    f = jnp.maximum(_dot(h2, f1_ref[...]), 0.0)
    x2 = x1 + _dot((f * f).astype(BF16), f2_ref[...])
    out_ref[...] = _rmsnorm(x2, nf_ref[...])


def _back(x, yconv, y, r, k, v, g, tb, wts):
    ns, tt, d = x.shape
    nblk = tt // tb
    row_spec = lambda w: pl.BlockSpec((None, tb, w), lambda b, i: (b, i, 0))
    names = ["r_k", "lnx_g", "lnx_b", "w_out", "norm2_g", "w_ff1", "w_ff2", "normf_g", "bd"]
    w_in = [wts[n] for n in names]
    acts = [x, yconv, y, r, k, v, g]
    return pl.pallas_call(
        _back_kernel,
        grid=(ns, nblk),
        in_specs=[row_spec(a.shape[-1]) for a in acts] + [_const_spec(w.shape) for w in w_in],
        out_specs=row_spec(d),
        out_shape=jax.ShapeDtypeStruct((ns, tt, d), F32),
        compiler_params=pltpu.CompilerParams(
            dimension_semantics=("arbitrary", "arbitrary"), vmem_limit_bytes=VMEM_LIMIT),
        name="back",
    )(*acts, *w_in)


def _mm(a, b):
    return jnp.dot(a.astype(BF16), b.astype(BF16), preferred_element_type=F32)


def _mm_nt(a, b):
    return lax.dot_general(a.astype(BF16), b.astype(BF16), (((1,), (1,)), ((), ())),
                           preferred_element_type=F32)


def _cumsum_rows(x):
    rows = x.shape[0]
    row = lax.broadcasted_iota(jnp.int32, x.shape, 0)
    sh = 1
    while sh < rows:
        x = x + jnp.where(row >= sh, pltpu.roll(x, sh, 0), 0.0)
        sh *= 2
    return x


def _scan_masks(c):
    lanes = GROUP * HEAD
    gc = GROUP * c
    i0 = lambda shape: lax.broadcasted_iota(jnp.int32, shape, 0)
    i1 = lambda shape: lax.broadcasted_iota(jnp.int32, shape, 1)
    col_t = i1((c, gc)) % c
    row_t = i0((c, gc))
    level = []
    s = 1
    while s < c:
        level.append((row_t // (2 * s) == col_t // (2 * s)) & (row_t % (2 * s) >= s)
                     & (col_t % (2 * s) < s))
        s *= 2
    return dict(
        level=level,
        lane_head=(i0((gc, lanes)) // c) == (i1((gc, lanes)) // HEAD),
        strict=col_t < i0((c, gc)),
        incl=col_t <= i0((c, gc)),
        bd_cc=(i0((gc, gc)) // c) == (i1((gc, gc)) // c),
        bd_ll=(i0((lanes, lanes)) // HEAD) == (i1((lanes, lanes)) // HEAD),
    )


def _chunk_precompute(load, masks):
    lw, kk, a, r, k, v = load()
    c = lw.shape[0]
    assert c == HEAD, "side-by-side forms share lane offsets only when CHUNK == HEAD"

    def bd_rows(m):
        return jnp.where(masks["lane_head"], jnp.concatenate([m] * GROUP, axis=0), 0.0)

    def bd_mat(m):
        return jnp.where(masks["bd_cc"], jnp.concatenate([m] * GROUP, axis=0), 0.0)

    def heads_t(m):
        mt = jnp.transpose(m)
        return jnp.concatenate([mt[h * HEAD:(h + 1) * HEAD, :] for h in range(GROUP)], axis=1)

    cum = _cumsum_rows(lw)
    cum_last = cum[c - 1:c, :]
    beta = kk * a
    alpha_t = -kk * jnp.exp(cum - lw)
    r_t = r * jnp.exp(cum)
    inv = jnp.exp(-cum)
    rem = jnp.exp(cum_last - cum)
    ar = jnp.concatenate([alpha_t, r_t], axis=0)
    g_b = _mm_nt(ar, bd_rows(beta * inv))
    g_k = _mm_nt(ar, bd_rows(k * inv))
    yield
    a_ab = jnp.where(masks["strict"], g_b[:c], 0.0)
    a_rb = jnp.where(masks["incl"], g_b[c:], 0.0)
    a_ak = jnp.where(masks["strict"], g_k[:c], 0.0)
    a_rk = jnp.where(masks["incl"], g_k[c:], 0.0)

    b_t = heads_t(beta * rem)
    k_t = heads_t(k * rem)
    res_v = _mm(jnp.concatenate([a_ak, a_rk, k_t], axis=0), bd_rows(v))

    tm, cm = None, a_ab
    n_lvl = len(masks["level"])
    for lvl, corner in enumerate(masks["level"]):
        m_s = jnp.where(corner, cm, 0.0)
        rhs = bd_mat(m_s)
        if lvl == 0:
            res = _mm(cm, rhs)
            yield
            akv, rkv, nh_k = res_v[:c], res_v[c:2 * c], res_v[2 * c:]
            tm, cm = m_s, cm + res
        elif lvl < n_lvl - 1:
            res = _mm(jnp.concatenate([tm, cm], axis=0), rhs)
            yield
            tm, cm = tm + m_s + res[:c], cm + res[c:]
        else:
            res = _mm(tm, rhs)
            yield
            tm = tm + m_s + res

    wa = alpha_t + _mm(tm, bd_rows(alpha_t))
    uv = akv + _mm(tm, bd_rows(akv))
    yield
    resid = akv - uv + _mm(a_ab, bd_rows(uv))
    yield
    uv = uv + resid + _mm(tm, bd_rows(resid))
    yield
    lhs = jnp.concatenate([a_rb, b_t], axis=0)
    res_w = _mm(lhs, bd_rows(wa))
    res_u = _mm(lhs, bd_rows(uv))
    p_col = jnp.transpose(jnp.broadcast_to(jnp.exp(cum_last), (8, cum.shape[1])))[:, 0:1]
    ph = jnp.concatenate([jnp.broadcast_to(p_col[h * HEAD:(h + 1) * HEAD, :], (HEAD, c))
                          for h in range(GROUP)], axis=1)
    yield
    rq = r_t + res_w[:c]
    yv = rkv + res_u[:c]
    mh = res_w[c:]
    nh = res_u[c:] + nh_k
    return rq, yv, mh, nh, ph


def _run_staggered(gens_by_chunk, skew, on_chunk_done):
    n = len(gens_by_chunk)
    results = [[None] * len(gens) for gens in gens_by_chunk]
    live = {(ci, j) for ci, gens in enumerate(gens_by_chunk) for j in range(len(gens))}
    n_done = 0
    rnd = 0
    while live:
        for ci in range(min(n, rnd // skew + 1)):
            for j, gen in enumerate(gens_by_chunk[ci]):
                if (ci, j) in live:
                    try:
                        next(gen)
                    except StopIteration as stop:
                        results[ci][j] = stop.value
                        live.discard((ci, j))
        while n_done < n and not any(ci == n_done for ci, _ in live):
            on_chunk_done(n_done, results[n_done])
            n_done += 1
        rnd += 1


SCAN_OPERANDS = ("lw", "kk", "a", "r", "k", "v")


def _pscan_kernel(lw_ref, kk_ref, a_ref, r_ref, k_ref, v_ref, y_ref, zout_ref, z_ref):
    @pl.when(pl.program_id(1) == 0)
    def _():
        z_ref[...] = jnp.zeros_like(z_ref)

    refs = (lw_ref, kk_ref, a_ref, r_ref, k_ref, v_ref)
    lanes = GROUP * HEAD
    n_groups = kk_ref.shape[1] // lanes
    n_chunks = kk_ref.shape[0] // CHUNK
    masks = _scan_masks(CHUNK)

    def loader(ci, g):
        rows = slice(ci * CHUNK, (ci + 1) * CHUNK)
        sl = slice(g * lanes, (g + 1) * lanes)
        return lambda: tuple(ref[rows, sl] for ref in refs)

    gens = [[_chunk_precompute(loader(ci, g), masks) for g in range(n_groups)]
            for ci in range(n_chunks)]
    zs = [z_ref[g] for g in range(n_groups)]

    def apply_chunk(ci, pre):
        res = [_mm(jnp.concatenate([pre[g][0], pre[g][2]], axis=0),
                   jnp.where(masks["bd_ll"], jnp.concatenate([zs[g]] * GROUP, axis=0), 0.0))
               for g in range(n_groups)]
        for g in range(n_groups):
            _, yv, _, nh, ph = pre[g]
            y_ref[ci * CHUNK:(ci + 1) * CHUNK, g * lanes:(g + 1) * lanes] = res[g][:CHUNK] + yv
            zs[g] = ph * zs[g] + res[g][CHUNK:] + nh

    _run_staggered(gens, SCAN_SKEW, apply_chunk)
    for g in range(n_groups):
        z_ref[g] = zs[g]
        zout_ref[g] = zs[g]


def _prompt_scan(lw, kk, a, r, k, v, tb):
    nb, t, w = kk.shape
    lanes = GROUP * HEAD
    ng = w // lanes
    spec = pl.BlockSpec((None, tb, w), lambda b, i: (b, i, 0))
    return pl.pallas_call(
        _pscan_kernel,
        grid=(nb, t // tb),
        in_specs=[spec] * 6,
        out_specs=[spec, pl.BlockSpec((None, ng, HEAD, lanes), lambda b, i: (b, 0, 0, 0))],
        out_shape=[jax.ShapeDtypeStruct((nb, t, w), F32),
                   jax.ShapeDtypeStruct((nb, ng, HEAD, lanes), F32)],
        scratch_shapes=[pltpu.VMEM((ng, HEAD, lanes), F32)],
        compiler_params=pltpu.CompilerParams(
            dimension_semantics=("arbitrary", "arbitrary"), vmem_limit_bytes=VMEM_LIMIT),
        name="prompt_scan",
    )(lw, kk, a, r, k, v)


LANE_SLAB = 128


def _sscan_kernel(kk_ref, a_ref, lw_ref, k_ref, r_ref, v_ref, s_in_ref, y_ref, s_out_ref,
                  fm_ref, yt_ref):
    n_t = kk_ref.shape[0]
    heads, n_k = s_in_ref.shape[:2]
    shape = s_in_ref.shape[2:]
    i_kk, i_kka, i_w, i_k, i_r, i_v = range(6)
    for t in range(n_t):
        kk_t = jnp.transpose(kk_ref[t])
        fm_ref[i_kk, t] = kk_t
        fm_ref[i_kka, t] = kk_t * jnp.transpose(a_ref[t])
        fm_ref[i_w, t] = jnp.exp(jnp.transpose(lw_ref[t]))
        fm_ref[i_k, t] = jnp.transpose(k_ref[t])
        fm_ref[i_r, t] = jnp.transpose(r_ref[t])
        fm_ref[i_v, t] = jnp.transpose(v_ref[t])

    zeros = jnp.zeros(shape, F32)
    for hd in range(heads):
        off = hd * HEAD

        def rows8(i, t, k8, off=off):
            return fm_ref[i, t, pl.ds(pl.multiple_of(off + k8 * 8, 8), 8), :]

        def first_sk(k8, acc, hd=hd, rows8=rows8):
            kk8 = rows8(i_kk, 0, k8)
            for j in range(8):
                acc = acc + s_in_ref[hd, k8 * 8 + j] * kk8[j:j + 1, :]
            return acc

        sk = lax.fori_loop(0, n_k // 8, first_sk, zeros)
        for t in range(n_t):
            src = s_in_ref if t == 0 else s_out_ref
            vt = fm_ref[i_v, t, off:off + HEAD, :]
            last = t == n_t - 1

            def step(k8, carry, hd=hd, rows8=rows8, t=t, src=src, vt=vt, sk=sk, last=last):
                acc, sk_next = carry
                kka8 = rows8(i_kka, t, k8)
                w8 = rows8(i_w, t, k8)
                k8v = rows8(i_k, t, k8)
                r8 = rows8(i_r, t, k8)
                kkn8 = None if last else rows8(i_kk, t + 1, k8)
                for j in range(8):
                    snew = (src[hd, k8 * 8 + j] * w8[j:j + 1, :] - sk * kka8[j:j + 1, :]
                            + vt * k8v[j:j + 1, :])
                    s_out_ref[hd, k8 * 8 + j] = snew
                    acc = acc + snew * r8[j:j + 1, :]
                    if not last:
                        sk_next = sk_next + snew * kkn8[j:j + 1, :]
                return acc, sk_next

            y, sk = lax.fori_loop(0, n_k // 8, step, (zeros, zeros))
            yt_ref[t, off:off + HEAD, :] = y
    for t in range(n_t):
        y_ref[t] = jnp.transpose(yt_ref[t])


def _sample_scan(kk, a, lw, k, r, v, state):
    n_t, nb, w = kk.shape
    heads = LANE_SLAB // HEAD
    spec = pl.BlockSpec((n_t, nb, LANE_SLAB), lambda i: (0, 0, i))
    sspec = pl.BlockSpec((heads, HEAD, HEAD, nb), lambda i: (i, 0, 0, 0))
    return pl.pallas_call(
        _sscan_kernel,
        grid=(w // LANE_SLAB,),
        in_specs=[spec] * 6 + [sspec],
        out_specs=[spec, sspec],
        out_shape=[jax.ShapeDtypeStruct((n_t, nb, w), F32),
                   jax.ShapeDtypeStruct(state.shape, F32)],
        scratch_shapes=[pltpu.VMEM((6, n_t, LANE_SLAB, nb), F32),
                        pltpu.VMEM((n_t, LANE_SLAB, nb), F32)],
        compiler_params=pltpu.CompilerParams(
            dimension_semantics=("arbitrary",), vmem_limit_bytes=VMEM_LIMIT),
        name="sample_scan",
    )(kk, a, lw, k, r, v, state)


def kernel(x_prompt, x_sample, state_conv, state_shift, state_wkv, norm1_g, w_in, conv_w, mu, w0,
           w2, a0, a2, g2, k_k, k_a, r_k, lnx_g, lnx_b, w_out, norm2_g, w_ff1, w_ff2, normf_g):
    depth = w_in.shape[0]
    assert depth == 1, "single-layer step only"
    bp, seq, d = x_prompt.shape
    bs, dseq, _ = x_sample.shape
    cw_dim = conv_w.shape[-1]
    rw_dim = w0.shape[-1]
    rcols = mu.shape[-1]
    nh = rw_dim // HEAD
    row = lambda t: t[0].reshape(1, -1)
    seg = jnp.arange(GROUP * HEAD) // HEAD
    wts = {
        "norm1_g": row(norm1_g), "w_in": w_in[0].astype(BF16), "conv_w": conv_w[0],
        "mu": row(mu), "w0": row(w0), "w2": w2[0].astype(BF16), "a0": row(a0),
        "a2": a2[0].astype(BF16), "g2": g2[0].astype(BF16), "k_k": row(k_k), "k_a": row(k_a),
        "r_k": row(r_k), "lnx_g": row(lnx_g), "lnx_b": row(lnx_b),
        "w_out": w_out[0].astype(BF16), "norm2_g": row(norm2_g),
        "w_ff1": w_ff1[0].astype(BF16), "w_ff2": w_ff2[0].astype(BF16),
        "normf_g": normf_g.reshape(1, -1),
        "bd": (seg[:, None] == seg[None, :]).astype(BF16),
    }

    zc = jnp.zeros((bp, 2, cw_dim), F32)
    zs = jnp.zeros((bp, 1, rcols), F32)
    yconv, r, k, v, kk, a, lw, g, nconv_p, nshift_p = _front(x_prompt, zc, zs, 1, FRONT_ROWS, wts)
    y, z = _prompt_scan(lw, kk, a, r, k, v, SCAN_ROWS)
    y_prompt = _back(x_prompt, yconv, y, r, k, v, g, BACK_ROWS, wts)
    ng = rw_dim // (GROUP * HEAD)
    zb = z.reshape(bp, ng, HEAD, GROUP, HEAD)
    wkv_p = jnp.transpose(zb, (0, 1, 3, 4, 2)).reshape(bp, nh, HEAD, HEAD)

    xs = jnp.transpose(x_sample, (1, 0, 2)).reshape(1, dseq * bs, d)
    c0 = jnp.transpose(state_conv[0], (1, 0, 2)).reshape(1, 2 * bs, cw_dim)
    s0 = state_shift[0].reshape(1, bs, rcols)
    tbs = 4 * bs
    yconv, r, k, v, kk, a, lw, g, nconv_s, nshift_s = _front(xs, c0, s0, bs, tbs, wts)
    tm = lambda t: t.reshape(dseq, bs, rw_dim)
    st = jnp.transpose(state_wkv[0], (1, 3, 2, 0))
    y_tm, st_new = _sample_scan(tm(kk), tm(a), tm(lw), tm(k), tm(r), tm(v), st)
    y = y_tm.reshape(1, dseq * bs, rw_dim)
    ys = _back(xs, yconv, y, r, k, v, g, tbs, wts)
    y_sample = jnp.transpose(ys.reshape(dseq, bs, d), (1, 0, 2))
    wkv_s = jnp.transpose(st_new, (3, 0, 2, 1))
    nconv_s = jnp.transpose(nconv_s.reshape(2, bs, cw_dim), (1, 0, 2))

    return (y_prompt, y_sample,
            nconv_p[None], nshift_p.reshape(1, bp, rcols), wkv_p[None],
            nconv_s[None], nshift_s.reshape(1, bs, rcols), wkv_s[None])
```

```python
import functools

import jax
import jax.numpy as jnp
from jax import lax
from jax.experimental import pallas as pl
from jax.experimental.pallas import tpu as pltpu

F32 = jnp.float32
BF16 = jnp.bfloat16

HEAD = 64
SUBLANES = 8
RMS_EPS = 1e-6
GN_EPS = 64e-5
NORM_EPS = 1e-12
VMEM_LIMIT = 56 * 1024 * 1024
CHUNK = 64
GROUP = 4
SCAN_SKEW = 1
FRONT_ROWS = 512
SCAN_ROWS = 512
BACK_ROWS = 512
SAMPLE_STEPS = 4


def _dot(a, b):
    return jnp.dot(a, b, preferred_element_type=F32)


def _sigmoid(x):
    return 0.5 * jnp.tanh(0.5 * x) + 0.5


def _rmsnorm(x, g):
    return x * lax.rsqrt(jnp.mean(x * x, axis=-1, keepdims=True) + RMS_EPS) * g


def _seg_sum(x, bd):
    lanes = bd.shape[0]
    xb = x.astype(BF16)
    return jnp.concatenate([_dot(xb[:, i:i + lanes], bd) for i in range(0, x.shape[1], lanes)],
                           axis=1)


def _shift_rows(x, k, head):
    rows = x.shape[0]
    if k == rows:
        return head
    if k % SUBLANES == 0:
        return jnp.concatenate([head, x[:rows - k]], axis=0)
    rolled = pltpu.roll(x, k, 0)
    row = lax.broadcasted_iota(jnp.int32, x.shape, 0)
    out = rolled
    for j in range(k):
        out = jnp.where(row == j, head[j:j + 1, :], out)
    return out


FRONT_WEIGHTS = ("norm1_g", "w_in", "conv_w", "mu", "w0", "w2", "a0", "a2", "g2", "k_k", "k_a",
                 "bd")


def _front_block(x, cu, cp, s, w, store):
    rows = x.shape[0]
    cw_dim = cu.shape[1]
    rw_dim = w["w0"].shape[1]
    rcols = cp.shape[1]
    win = w["w_in"]
    h = _rmsnorm(x, w["norm1_g"][...]).astype(BF16)

    def proj(c0, c1):
        return _dot(h, win[:, c0:c1])

    def rwkv_piece(c0, c1):
        return proj(3 * cw_dim + c0, 3 * cw_dim + c1)

    def mixed(p, c0, c1):
        prev = _shift_rows(p, s, cp[:, c0:c1])
        return p + (prev - p) * w["mu"][:, c0:c1], p[rows - s:]

    p_lora = rwkv_piece(3 * rw_dim, rcols)
    p_k = rwkv_piece(rw_dim, 2 * rw_dim)
    p_r = rwkv_piece(0, rw_dim)
    p_v = rwkv_piece(2 * rw_dim, 3 * rw_dim)
    lora, last_l = mixed(p_lora, 3 * rw_dim, rcols)
    d_w = w["w2"].shape[0]
    d_a = w["a2"].shape[0]
    pw = lora[:, :d_w]
    pa = lora[:, d_w:d_w + d_a]
    pg = lora[:, d_w + d_a:]
    wl = w["w0"][...] + _dot(jnp.tanh(pw).astype(BF16), w["w2"][...])
    z = -wl
    softplus = jnp.maximum(z, 0.0) + jnp.log(1.0 + jnp.exp(-jnp.abs(z)))
    store("lw", -jnp.exp(-softplus - 0.5))
    a = _sigmoid(w["a0"][...] + _dot(pa.astype(BF16), w["a2"][...]))
    store("a", a)
    store("g", _dot(_sigmoid(pg).astype(BF16), w["g2"][...]))

    c_c = proj(cw_dim, 2 * cw_dim)
    k, last_k = mixed(p_k, rw_dim, 2 * rw_dim)
    kk = k * w["k_k"][...]
    store("kk", kk * lax.rsqrt(jnp.maximum(_seg_sum(kk * kk, w["bd"][...]), NORM_EPS * NORM_EPS)))
    store("k", k * (1.0 + (a - 1.0) * w["k_a"][...]))

    c_x = proj(2 * cw_dim, 3 * cw_dim)
    r, last_r = mixed(p_r, 0, rw_dim)
    store("r", r)
    c_b = proj(0, cw_dim)
    v, last_v = mixed(p_v, 2 * rw_dim, 3 * rw_dim)
    store("v", v)
    store("new_cp", jnp.concatenate([last_r, last_k, last_v, last_l], axis=1))

    u = c_c * c_x
    u1 = _shift_rows(u, s, cu[s:2 * s])
    u2 = _shift_rows(u, 2 * s, cu)
    cw = w["conv_w"][...]
    conv = cw[0:1] * u2 + cw[1:2] * u1 + cw[2:3] * u
    store("new_cu", u[rows - 2 * s:])
    store("yconv", (c_b * conv).astype(BF16))


FRONT_OUTPUTS = ("yconv", "r", "k", "v", "kk", "a", "lw", "g", "new_cu", "new_cp")


def _front_kernel(s, x_ref, conv0_ref, shift0_ref, *refs):
    nw = len(FRONT_WEIGHTS)
    w = dict(zip(FRONT_WEIGHTS, refs[:nw]))
    outs = dict(zip(FRONT_OUTPUTS, refs[nw:nw + len(FRONT_OUTPUTS)]))
    cu_ref, cp_ref = refs[nw + len(FRONT_OUTPUTS):]

    @pl.when(pl.program_id(1) == 0)
    def _():
        cu_ref[...] = conv0_ref[...]
        cp_ref[...] = shift0_ref[...]

    carry = {}

    def store(name, value):
        outs[name][...] = value
        carry[name] = value

    _front_block(x_ref[...], cu_ref[...], cp_ref[...], s, w, store)
    cu_ref[...] = carry["new_cu"]
    cp_ref[...] = carry["new_cp"]


def _const_spec(shape):
    nd = len(shape)
    return pl.BlockSpec(shape, lambda *_: (0,) * nd, pipeline_mode=pl.Buffered(1))


def _front(x, conv0, shift0, s, tb, wts):
    ns, tt, d = x.shape
    cw_dim = conv0.shape[-1]
    rw_dim = wts["w0"].shape[-1]
    rcols = shift0.shape[-1]
    nblk = tt // tb
    row_spec = lambda w: pl.BlockSpec((None, tb, w), lambda b, i: (b, i, 0))
    st_spec = lambda r, w: pl.BlockSpec((None, r, w), lambda b, i: (b, 0, 0))
    w_in = [wts[n] for n in FRONT_WEIGHTS]
    out_shapes = ([jax.ShapeDtypeStruct((ns, tt, cw_dim), BF16)]
                  + [jax.ShapeDtypeStruct((ns, tt, rw_dim), F32)] * 7
                  + [jax.ShapeDtypeStruct((ns, 2 * s, cw_dim), F32),
                     jax.ShapeDtypeStruct((ns, s, rcols), F32)])
    out_specs = ([row_spec(cw_dim)] + [row_spec(rw_dim)] * 7
                 + [st_spec(2 * s, cw_dim), st_spec(s, rcols)])
    return pl.pallas_call(
        functools.partial(_front_kernel, s),
        grid=(ns, nblk),
        in_specs=[row_spec(d), st_spec(2 * s, cw_dim), st_spec(s, rcols)]
        + [_const_spec(w.shape) for w in w_in],
        out_specs=out_specs,
        out_shape=out_shapes,
        scratch_shapes=[pltpu.VMEM((2 * s, cw_dim), F32), pltpu.VMEM((s, rcols), F32)],
        compiler_params=pltpu.CompilerParams(
            dimension_semantics=("arbitrary", "arbitrary"), vmem_limit_bytes=VMEM_LIMIT),
        name="front",
    )(x, conv0, shift0, *w_in)


def _back_kernel(x_ref, yconv_ref, y_ref, r_ref, k_ref, v_ref, g_ref,
                 rk_ref, lg_ref, lb_ref, wo_ref, n2_ref, f1_ref, f2_ref, nf_ref, bd_ref,
                 out_ref):
    bd = bd_ref[...]
    y = y_ref[...]
    inv_n = 1.0 / HEAD
    mean = _seg_sum(y, bd) * inv_n
    dlt = y - mean
    var = _seg_sum(dlt * dlt, bd) * inv_n
    yn = dlt * lax.rsqrt(var + GN_EPS) * lg_ref[...] + lb_ref[...]
    v = v_ref[...]
    bonus = _seg_sum(r_ref[...] * k_ref[...] * rk_ref[...], bd) * v
    y_rwkv = ((yn + bonus) * g_ref[...]).astype(BF16)
    cw_dim = yconv_ref.shape[-1]
    x = x_ref[...]
    x1 = x + _dot(yconv_ref[...], wo_ref[:cw_dim, :]) + _dot(y_rwkv, wo_ref[cw_dim:, :])
    h2 = _rmsnorm(x1, n2_ref[...]).astype(BF16)
    f = jnp.maximum(_dot(h2, f1_ref[...]), 0.0)
    x2 = x1 + _dot((f * f).astype(BF16), f2_ref[...])
    out_ref[...] = _rmsnorm(x2, nf_ref[...])


def _back(x, yconv, y, r, k, v, g, tb, wts):
    ns, tt, d = x.shape
    nblk = tt // tb
    row_spec = lambda w: pl.BlockSpec((None, tb, w), lambda b, i: (b, i, 0))
    names = ["r_k", "lnx_g", "lnx_b", "w_out", "norm2_g", "w_ff1", "w_ff2", "normf_g", "bd"]
    w_in = [wts[n] for n in names]
    acts = [x, yconv, y, r, k, v, g]
    return pl.pallas_call(
        _back_kernel,
        grid=(ns, nblk),
        in_specs=[row_spec(a.shape[-1]) for a in acts] + [_const_spec(w.shape) for w in w_in],
        out_specs=row_spec(d),
        out_shape=jax.ShapeDtypeStruct((ns, tt, d), F32),
        compiler_params=pltpu.CompilerParams(
            dimension_semantics=("arbitrary", "arbitrary"), vmem_limit_bytes=VMEM_LIMIT),
        name="back",
    )(*acts, *w_in)


def _mm(a, b):
    return jnp.dot(a.astype(BF16), b.astype(BF16), preferred_element_type=F32)


def _mm_nt(a, b):
    return lax.dot_general(a.astype(BF16), b.astype(BF16), (((1,), (1,)), ((), ())),
                           preferred_element_type=F32)


def _cumsum_rows(x):
    rows = x.shape[0]
    row = lax.broadcasted_iota(jnp.int32, x.shape, 0)
    sh = 1
    while sh < rows:
        x = x + jnp.where(row >= sh, pltpu.roll(x, sh, 0), 0.0)
        sh *= 2
    return x


def _scan_masks(c):
    lanes = GROUP * HEAD
    gc = GROUP * c
    i0 = lambda shape: lax.broadcasted_iota(jnp.int32, shape, 0)
    i1 = lambda shape: lax.broadcasted_iota(jnp.int32, shape, 1)
    col_t = i1((c, gc)) % c
    row_t = i0((c, gc))
    level = []
    s = 1
    while s < c:
        level.append((row_t // (2 * s) == col_t // (2 * s)) & (row_t % (2 * s) >= s)
                     & (col_t % (2 * s) < s))
        s *= 2
    return dict(
        level=level,
        lane_head=(i0((gc, lanes)) // c) == (i1((gc, lanes)) // HEAD),
        strict=col_t < i0((c, gc)),
        incl=col_t <= i0((c, gc)),
        bd_cc=(i0((gc, gc)) // c) == (i1((gc, gc)) // c),
        bd_ll=(i0((lanes, lanes)) // HEAD) == (i1((lanes, lanes)) // HEAD),
    )


def _chunk_precompute(load, masks):
    lw, kk, a, r, k, v = load()
    c = lw.shape[0]
    assert c == HEAD, "side-by-side forms share lane offsets only when CHUNK == HEAD"

    def bd_rows(m):
        return jnp.where(masks["lane_head"], jnp.concatenate([m] * GROUP, axis=0), 0.0)

    def bd_mat(m):
        return jnp.where(masks["bd_cc"], jnp.concatenate([m] * GROUP, axis=0), 0.0)

    def heads_t(m):
        mt = jnp.transpose(m)
        return jnp.concatenate([mt[h * HEAD:(h + 1) * HEAD, :] for h in range(GROUP)], axis=1)

    cum = _cumsum_rows(lw)
    cum_last = cum[c - 1:c, :]
    beta = kk * a
    alpha_t = -kk * jnp.exp(cum - lw)
    r_t = r * jnp.exp(cum)
    inv = jnp.exp(-cum)
    rem = jnp.exp(cum_last - cum)
    ar = jnp.concatenate([alpha_t, r_t], axis=0)
    g_b = _mm_nt(ar, bd_rows(beta * inv))
    g_k = _mm_nt(ar, bd_rows(k * inv))
    yield
    a_ab = jnp.where(masks["strict"], g_b[:c], 0.0)
    a_rb = jnp.where(masks["incl"], g_b[c:], 0.0)
    a_ak = jnp.where(masks["strict"], g_k[:c], 0.0)
    a_rk = jnp.where(masks["incl"], g_k[c:], 0.0)

    b_t = heads_t(beta * rem)
    k_t = heads_t(k * rem)
    res_v = _mm(jnp.concatenate([a_ak, a_rk, k_t], axis=0), bd_rows(v))

    tm, cm = None, a_ab
    n_lvl = len(masks["level"])
    for lvl, corner in enumerate(masks["level"]):
        m_s = jnp.where(corner, cm, 0.0)
        rhs = bd_mat(m_s)
        if lvl == 0:
            res = _mm(cm, rhs)
            yield
            akv, rkv, nh_k = res_v[:c], res_v[c:2 * c], res_v[2 * c:]
            tm, cm = m_s, cm + res
        elif lvl < n_lvl - 1:
            res = _mm(jnp.concatenate([tm, cm], axis=0), rhs)
            yield
            tm, cm = tm + m_s + res[:c], cm + res[c:]
        else:
            res = _mm(tm, rhs)
            yield
            tm = tm + m_s + res

    wa = alpha_t + _mm(tm, bd_rows(alpha_t))
    uv = akv + _mm(tm, bd_rows(akv))
    yield
    resid = akv - uv + _mm(a_ab, bd_rows(uv))
    yield
    uv = uv + resid + _mm(tm, bd_rows(resid))
    yield
    lhs = jnp.concatenate([a_rb, b_t], axis=0)
    res_w = _mm(lhs, bd_rows(wa))
    res_u = _mm(lhs, bd_rows(uv))
    p_col = jnp.transpose(jnp.broadcast_to(jnp.exp(cum_last), (SUBLANES, cum.shape[1])))[:, 0:1]
    ph = jnp.concatenate([jnp.broadcast_to(p_col[h * HEAD:(h + 1) * HEAD, :], (HEAD, c))
                          for h in range(GROUP)], axis=1)
    yield
    rq = r_t + res_w[:c]
    yv = rkv + res_u[:c]
    mh = res_w[c:]
    nh = res_u[c:] + nh_k
    return rq, yv, mh, nh, ph


def _run_staggered(gens_by_chunk, skew, on_chunk_done):
    n = len(gens_by_chunk)
    results = [[None] * len(gens) for gens in gens_by_chunk]
    live = {(ci, j) for ci, gens in enumerate(gens_by_chunk) for j in range(len(gens))}
    n_done = 0
    rnd = 0
    while live:
        for ci in range(min(n, rnd // skew + 1)):
            for j, gen in enumerate(gens_by_chunk[ci]):
                if (ci, j) in live:
                    try:
                        next(gen)
                    except StopIteration as stop:
                        results[ci][j] = stop.value
                        live.discard((ci, j))
        while n_done < n and not any(ci == n_done for ci, _ in live):
            on_chunk_done(n_done, results[n_done])
            n_done += 1
        rnd += 1


SCAN_OPERANDS = ("lw", "kk", "a", "r", "k", "v")


def _pscan_kernel(lw_ref, kk_ref, a_ref, r_ref, k_ref, v_ref, y_ref, zout_ref, z_ref):
    @pl.when(pl.program_id(1) == 0)
    def _():
        z_ref[...] = jnp.zeros_like(z_ref)

    refs = (lw_ref, kk_ref, a_ref, r_ref, k_ref, v_ref)
    lanes = GROUP * HEAD
    n_groups = kk_ref.shape[1] // lanes
    n_chunks = kk_ref.shape[0] // CHUNK
    masks = _scan_masks(CHUNK)

    def loader(ci, g):
        rows = slice(ci * CHUNK, (ci + 1) * CHUNK)
        sl = slice(g * lanes, (g + 1) * lanes)
        return lambda: tuple(ref[rows, sl] for ref in refs)

    gens = [[_chunk_precompute(loader(ci, g), masks) for g in range(n_groups)]
            for ci in range(n_chunks)]
    zs = [z_ref[g] for g in range(n_groups)]

    def apply_chunk(ci, pre):
        res = [_mm(jnp.concatenate([pre[g][0], pre[g][2]], axis=0),
                   jnp.where(masks["bd_ll"], jnp.concatenate([zs[g]] * GROUP, axis=0), 0.0))
               for g in range(n_groups)]
        for g in range(n_groups):
            _, yv, _, nh, ph = pre[g]
            y_ref[ci * CHUNK:(ci + 1) * CHUNK, g * lanes:(g + 1) * lanes] = res[g][:CHUNK] + yv
            zs[g] = ph * zs[g] + res[g][CHUNK:] + nh

    _run_staggered(gens, SCAN_SKEW, apply_chunk)
    for g in range(n_groups):
        z_ref[g] = zs[g]
        zout_ref[g] = zs[g]


def _prompt_scan(lw, kk, a, r, k, v, tb):
    nb, t, w = kk.shape
    lanes = GROUP * HEAD
    ng = w // lanes
    spec = pl.BlockSpec((None, tb, w), lambda b, i: (b, i, 0))
    return pl.pallas_call(
        _pscan_kernel,
        grid=(nb, t // tb),
        in_specs=[spec] * 6,
        out_specs=[spec, pl.BlockSpec((None, ng, HEAD, lanes), lambda b, i: (b, 0, 0, 0))],
        out_shape=[jax.ShapeDtypeStruct((nb, t, w), F32),
                   jax.ShapeDtypeStruct((nb, ng, HEAD, lanes), F32)],
        scratch_shapes=[pltpu.VMEM((ng, HEAD, lanes), F32)],
        compiler_params=pltpu.CompilerParams(
            dimension_semantics=("arbitrary", "arbitrary"), vmem_limit_bytes=VMEM_LIMIT),
        name="prompt_scan",
    )(lw, kk, a, r, k, v)


LANE_SLAB = 128


def _sscan_kernel(kk_ref, a_ref, lw_ref, k_ref, r_ref, v_ref, s_in_ref, y_ref, s_out_ref,
                  fm_ref, yt_ref):
    n_t = kk_ref.shape[0]
    heads, n_k = s_in_ref.shape[:2]
    shape = s_in_ref.shape[2:]
    i_kk, i_kka, i_w, i_k, i_r, i_v = range(6)
    for t in range(n_t):
        kk_t = jnp.transpose(kk_ref[t])
        fm_ref[i_kk, t] = kk_t
        fm_ref[i_kka, t] = kk_t * jnp.transpose(a_ref[t])
        fm_ref[i_w, t] = jnp.exp(jnp.transpose(lw_ref[t]))
        fm_ref[i_k, t] = jnp.transpose(k_ref[t])
        fm_ref[i_r, t] = jnp.transpose(r_ref[t])
        fm_ref[i_v, t] = jnp.transpose(v_ref[t])

    zeros = jnp.zeros(shape, F32)
    for hd in range(heads):
        off = hd * HEAD

        def rows8(i, t, k8, off=off):
            start = pl.multiple_of(off + k8 * SUBLANES, SUBLANES)
            return fm_ref[i, t, pl.ds(start, SUBLANES), :]

        def first_sk(k8, acc, hd=hd, rows8=rows8):
            kk8 = rows8(i_kk, 0, k8)
            for j in range(SUBLANES):
                acc = acc + s_in_ref[hd, k8 * SUBLANES + j] * kk8[j:j + 1, :]
            return acc

        sk = lax.fori_loop(0, n_k // SUBLANES, first_sk, zeros)
        for t in range(n_t):
            src = s_in_ref if t == 0 else s_out_ref
            vt = fm_ref[i_v, t, off:off + HEAD, :]
            last = t == n_t - 1

            def step(k8, carry, hd=hd, rows8=rows8, t=t, src=src, vt=vt, sk=sk, last=last):
                acc, sk_next = carry
                kka8 = rows8(i_kka, t, k8)
                w8 = rows8(i_w, t, k8)
                k8v = rows8(i_k, t, k8)
                r8 = rows8(i_r, t, k8)
                kkn8 = None if last else rows8(i_kk, t + 1, k8)
                for j in range(SUBLANES):
                    snew = (src[hd, k8 * SUBLANES + j] * w8[j:j + 1, :] - sk * kka8[j:j + 1, :]
                            + vt * k8v[j:j + 1, :])
                    s_out_ref[hd, k8 * SUBLANES + j] = snew
                    acc = acc + snew * r8[j:j + 1, :]
                    if not last:
                        sk_next = sk_next + snew * kkn8[j:j + 1, :]
                return acc, sk_next

            y, sk = lax.fori_loop(0, n_k // SUBLANES, step, (zeros, zeros))
            yt_ref[t, off:off + HEAD, :] = y
    for t in range(n_t):
        y_ref[t] = jnp.transpose(yt_ref[t])


def _sample_scan(kk, a, lw, k, r, v, state):
    n_t, nb, w = kk.shape
    heads = LANE_SLAB // HEAD
    spec = pl.BlockSpec((n_t, nb, LANE_SLAB), lambda i: (0, 0, i))
    sspec = pl.BlockSpec((heads, HEAD, HEAD, nb), lambda i: (i, 0, 0, 0))
    return pl.pallas_call(
        _sscan_kernel,
        grid=(w // LANE_SLAB,),
        in_specs=[spec] * 6 + [sspec],
        out_specs=[spec, sspec],
        out_shape=[jax.ShapeDtypeStruct((n_t, nb, w), F32),
                   jax.ShapeDtypeStruct(state.shape, F32)],
        scratch_shapes=[pltpu.VMEM((6, n_t, LANE_SLAB, nb), F32),
                        pltpu.VMEM((n_t, LANE_SLAB, nb), F32)],
        compiler_params=pltpu.CompilerParams(
            dimension_semantics=("arbitrary",), vmem_limit_bytes=VMEM_LIMIT),
        name="sample_scan",
    )(kk, a, lw, k, r, v, state)


def kernel(x_prompt, x_sample, state_conv, state_shift, state_wkv, norm1_g, w_in, conv_w, mu, w0,
           w2, a0, a2, g2, k_k, k_a, r_k, lnx_g, lnx_b, w_out, norm2_g, w_ff1, w_ff2, normf_g):
    depth = w_in.shape[0]
    assert depth == 1, "single-layer step only"
    bp, seq, d = x_prompt.shape
    bs, dseq, _ = x_sample.shape
    cw_dim = conv_w.shape[-1]
    rw_dim = w0.shape[-1]
    rcols = mu.shape[-1]
    nh = rw_dim // HEAD
    row = lambda t: t[0].reshape(1, -1)
    seg = jnp.arange(GROUP * HEAD) // HEAD
    wts = {
        "norm1_g": row(norm1_g), "w_in": w_in[0].astype(BF16), "conv_w": conv_w[0],
        "mu": row(mu), "w0": row(w0), "w2": w2[0].astype(BF16), "a0": row(a0),
        "a2": a2[0].astype(BF16), "g2": g2[0].astype(BF16), "k_k": row(k_k), "k_a": row(k_a),
        "r_k": row(r_k), "lnx_g": row(lnx_g), "lnx_b": row(lnx_b),
        "w_out": w_out[0].astype(BF16), "norm2_g": row(norm2_g),
        "w_ff1": w_ff1[0].astype(BF16), "w_ff2": w_ff2[0].astype(BF16),
        "normf_g": normf_g.reshape(1, -1),
        "bd": (seg[:, None] == seg[None, :]).astype(BF16),
    }

    zc = jnp.zeros((bp, 2, cw_dim), F32)
    zs = jnp.zeros((bp, 1, rcols), F32)
    yconv, r, k, v, kk, a, lw, g, nconv_p, nshift_p = _front(x_prompt, zc, zs, 1, FRONT_ROWS, wts)
    y, z = _prompt_scan(lw, kk, a, r, k, v, SCAN_ROWS)
    y_prompt = _back(x_prompt, yconv, y, r, k, v, g, BACK_ROWS, wts)
    ng = rw_dim // (GROUP * HEAD)
    zb = z.reshape(bp, ng, HEAD, GROUP, HEAD)
    wkv_p = jnp.transpose(zb, (0, 1, 3, 4, 2)).reshape(bp, nh, HEAD, HEAD)

    xs = jnp.transpose(x_sample, (1, 0, 2)).reshape(1, dseq * bs, d)
    c0 = jnp.transpose(state_conv[0], (1, 0, 2)).reshape(1, 2 * bs, cw_dim)
    s0 = state_shift[0].reshape(1, bs, rcols)
    tbs = SAMPLE_STEPS * bs
    yconv, r, k, v, kk, a, lw, g, nconv_s, nshift_s = _front(xs, c0, s0, bs, tbs, wts)
    tm = lambda t: t.reshape(dseq, bs, rw_dim)
    st = jnp.transpose(state_wkv[0], (1, 3, 2, 0))
    y_tm, st_new = _sample_scan(tm(kk), tm(a), tm(lw), tm(k), tm(r), tm(v), st)
    y = y_tm.reshape(1, dseq * bs, rw_dim)
    ys = _back(xs, yconv, y, r, k, v, g, tbs, wts)
    y_sample = jnp.transpose(ys.reshape(dseq, bs, d), (1, 0, 2))
    wkv_s = jnp.transpose(st_new, (3, 0, 2, 1))
    nconv_s = jnp.transpose(nconv_s.reshape(2, bs, cw_dim), (1, 0, 2))

    return (y_prompt, y_sample,
            nconv_p[None], nshift_p.reshape(1, bp, rcols), wkv_p[None],
            nconv_s[None], nshift_s.reshape(1, bs, rcols), wkv_s[None])
```

```python
import functools

import jax
import jax.numpy as jnp
from jax import lax
from jax.experimental import pallas as pl
from jax.experimental.pallas import tpu as pltpu

F32 = jnp.float32
BF16 = jnp.bfloat16

HEAD = 64
SUBLANES = 8
RMS_EPS = 1e-6
GN_EPS = 64e-5
NORM_EPS = 1e-12
VMEM_LIMIT = 56 * 1024 * 1024
CHUNK = 64
GROUP = 4
SCAN_SKEW = 1
FRONT_ROWS = 512
SCAN_ROWS = 1024
BACK_ROWS = 512
SAMPLE_STEPS = 4


def _dot(a, b):
    return jnp.dot(a, b, preferred_element_type=F32)


def _sigmoid(x):
    return 0.5 * jnp.tanh(0.5 * x) + 0.5


def _rmsnorm(x, g):
    return x * lax.rsqrt(jnp.mean(x * x, axis=-1, keepdims=True) + RMS_EPS) * g


def _seg_sum(x, bd):
    lanes = bd.shape[0]
    xb = x.astype(BF16)
    return jnp.concatenate([_dot(xb[:, i:i + lanes], bd) for i in range(0, x.shape[1], lanes)],
                           axis=1)


def _shift_rows(x, k, head):
    rows = x.shape[0]
    if k == rows:
        return head
    if k % SUBLANES == 0:
        return jnp.concatenate([head, x[:rows - k]], axis=0)
    rolled = pltpu.roll(x, k, 0)
    row = lax.broadcasted_iota(jnp.int32, x.shape, 0)
    out = rolled
    for j in range(k):
        out = jnp.where(row == j, head[j:j + 1, :], out)
    return out


FRONT_WEIGHTS = ("norm1_g", "w_in", "conv_w", "mu", "w0", "w2", "a0", "a2", "g2", "k_k", "k_a",
                 "bd")


def _front_block(x, cu, cp, s, w, store):
    rows = x.shape[0]
    cw_dim = cu.shape[1]
    rw_dim = w["w0"].shape[1]
    rcols = cp.shape[1]
    win = w["w_in"]
    h = _rmsnorm(x, w["norm1_g"][...]).astype(BF16)

    def proj(c0, c1):
        return _dot(h, win[:, c0:c1])

    def rwkv_piece(c0, c1):
        return proj(3 * cw_dim + c0, 3 * cw_dim + c1)

    def mixed(p, c0, c1):
        prev = _shift_rows(p, s, cp[:, c0:c1])
        return p + (prev - p) * w["mu"][:, c0:c1], p[rows - s:]

    p_lora = rwkv_piece(3 * rw_dim, rcols)
    p_k = rwkv_piece(rw_dim, 2 * rw_dim)
    p_r = rwkv_piece(0, rw_dim)
    p_v = rwkv_piece(2 * rw_dim, 3 * rw_dim)
    c_c = proj(cw_dim, 2 * cw_dim)
    lora, last_l = mixed(p_lora, 3 * rw_dim, rcols)
    d_w = w["w2"].shape[0]
    d_a = w["a2"].shape[0]
    pw = lora[:, :d_w]
    pa = lora[:, d_w:d_w + d_a]
    pg = lora[:, d_w + d_a:]
    wl = w["w0"][...] + _dot(jnp.tanh(pw).astype(BF16), w["w2"][...])
    z = -wl
    softplus = jnp.maximum(z, 0.0) + jnp.log(1.0 + jnp.exp(-jnp.abs(z)))
    store("lw", -jnp.exp(-softplus - 0.5))
    a = _sigmoid(w["a0"][...] + _dot(pa.astype(BF16), w["a2"][...]))
    store("a", a)
    store("g", _dot(_sigmoid(pg).astype(BF16), w["g2"][...]))

    c_x = proj(2 * cw_dim, 3 * cw_dim)
    k, last_k = mixed(p_k, rw_dim, 2 * rw_dim)
    kk = k * w["k_k"][...]
    store("kk", kk * lax.rsqrt(jnp.maximum(_seg_sum(kk * kk, w["bd"][...]), NORM_EPS * NORM_EPS)))
    store("k", k * (1.0 + (a - 1.0) * w["k_a"][...]))

    c_b = proj(0, cw_dim)
    r, last_r = mixed(p_r, 0, rw_dim)
    store("r", r)
    v, last_v = mixed(p_v, 2 * rw_dim, 3 * rw_dim)
    store("v", v)
    store("new_cp", jnp.concatenate([last_r, last_k, last_v, last_l], axis=1))

    u = c_c * c_x
    u1 = _shift_rows(u, s, cu[s:2 * s])
    u2 = _shift_rows(u, 2 * s, cu)
    cw = w["conv_w"][...]
    conv = cw[0:1] * u2 + cw[1:2] * u1 + cw[2:3] * u
    store("new_cu", u[rows - 2 * s:])
    store("yconv", (c_b * conv).astype(BF16))


FRONT_OUTPUTS = ("yconv", "r", "k", "v", "kk", "a", "lw", "g", "new_cu", "new_cp")


def _front_kernel(s, x_ref, conv0_ref, shift0_ref, *refs):
    nw = len(FRONT_WEIGHTS)
    w = dict(zip(FRONT_WEIGHTS, refs[:nw]))
    outs = dict(zip(FRONT_OUTPUTS, refs[nw:nw + len(FRONT_OUTPUTS)]))
    cu_ref, cp_ref = refs[nw + len(FRONT_OUTPUTS):]

    @pl.when(pl.program_id(1) == 0)
    def _():
        cu_ref[...] = conv0_ref[...]
        cp_ref[...] = shift0_ref[...]

    carry = {}

    def store(name, value):
        outs[name][...] = value
        carry[name] = value

    _front_block(x_ref[...], cu_ref[...], cp_ref[...], s, w, store)
    cu_ref[...] = carry["new_cu"]
    cp_ref[...] = carry["new_cp"]


def _const_spec(shape):
    nd = len(shape)
    return pl.BlockSpec(shape, lambda *_: (0,) * nd, pipeline_mode=pl.Buffered(1))


def _front(x, conv0, shift0, s, tb, wts):
    ns, tt, d = x.shape
    cw_dim = conv0.shape[-1]
    rw_dim = wts["w0"].shape[-1]
    rcols = shift0.shape[-1]
    nblk = tt // tb
    row_spec = lambda w: pl.BlockSpec((None, tb, w), lambda b, i: (b, i, 0))
    st_spec = lambda r, w: pl.BlockSpec((None, r, w), lambda b, i: (b, 0, 0))
    w_in = [wts[n] for n in FRONT_WEIGHTS]
    out_shapes = ([jax.ShapeDtypeStruct((ns, tt, cw_dim), BF16)]
                  + [jax.ShapeDtypeStruct((ns, tt, rw_dim), F32)] * 7
                  + [jax.ShapeDtypeStruct((ns, 2 * s, cw_dim), F32),
                     jax.ShapeDtypeStruct((ns, s, rcols), F32)])
    out_specs = ([row_spec(cw_dim)] + [row_spec(rw_dim)] * 7
                 + [st_spec(2 * s, cw_dim), st_spec(s, rcols)])
    return pl.pallas_call(
        functools.partial(_front_kernel, s),
        grid=(ns, nblk),
        in_specs=[row_spec(d), st_spec(2 * s, cw_dim), st_spec(s, rcols)]
        + [_const_spec(w.shape) for w in w_in],
        out_specs=out_specs,
        out_shape=out_shapes,
        scratch_shapes=[pltpu.VMEM((2 * s, cw_dim), F32), pltpu.VMEM((s, rcols), F32)],
        compiler_params=pltpu.CompilerParams(
            dimension_semantics=("arbitrary", "arbitrary"), vmem_limit_bytes=VMEM_LIMIT),
        name="front",
    )(x, conv0, shift0, *w_in)


def _back_kernel(x_ref, yconv_ref, y_ref, r_ref, k_ref, v_ref, g_ref,
                 rk_ref, lg_ref, lb_ref, wo_ref, n2_ref, f1_ref, f2_ref, nf_ref, bd_ref,
                 out_ref):
    bd = bd_ref[...]
    y = y_ref[...]
    inv_n = 1.0 / HEAD
    mean = _seg_sum(y, bd) * inv_n
    dlt = y - mean
    var = _seg_sum(dlt * dlt, bd) * inv_n
    yn = dlt * lax.rsqrt(var + GN_EPS) * lg_ref[...] + lb_ref[...]
    v = v_ref[...]
    bonus = _seg_sum(r_ref[...] * k_ref[...] * rk_ref[...], bd) * v
    y_rwkv = ((yn + bonus) * g_ref[...]).astype(BF16)
    cw_dim = yconv_ref.shape[-1]
    x = x_ref[...]
    x1 = x + _dot(yconv_ref[...], wo_ref[:cw_dim, :]) + _dot(y_rwkv, wo_ref[cw_dim:, :])
    h2 = _rmsnorm(x1, n2_ref[...]).astype(BF16)
    f = jnp.maximum(_dot(h2, f1_ref[...]), 0.0)
    x2 = x1 + _dot((f * f).astype(BF16), f2_ref[...])
    out_ref[...] = _rmsnorm(x2, nf_ref[...])


def _back(x, yconv, y, r, k, v, g, tb, wts):
    ns, tt, d = x.shape
    nblk = tt // tb
    row_spec = lambda w: pl.BlockSpec((None, tb, w), lambda b, i: (b, i, 0))
    names = ["r_k", "lnx_g", "lnx_b", "w_out", "norm2_g", "w_ff1", "w_ff2", "normf_g", "bd"]
    w_in = [wts[n] for n in names]
    acts = [x, yconv, y, r, k, v, g]
    return pl.pallas_call(
        _back_kernel,
        grid=(ns, nblk),
        in_specs=[row_spec(a.shape[-1]) for a in acts] + [_const_spec(w.shape) for w in w_in],
        out_specs=row_spec(d),
        out_shape=jax.ShapeDtypeStruct((ns, tt, d), F32),
        compiler_params=pltpu.CompilerParams(
            dimension_semantics=("arbitrary", "arbitrary"), vmem_limit_bytes=VMEM_LIMIT),
        name="back",
    )(*acts, *w_in)


def _mm(a, b):
    return jnp.dot(a.astype(BF16), b.astype(BF16), preferred_element_type=F32)


def _mm_nt(a, b):
    return lax.dot_general(a.astype(BF16), b.astype(BF16), (((1,), (1,)), ((), ())),
                           preferred_element_type=F32)


def _cumsum_rows(x):
    rows = x.shape[0]
    row = lax.broadcasted_iota(jnp.int32, x.shape, 0)
    sh = 1
    while sh < rows:
        x = x + jnp.where(row >= sh, pltpu.roll(x, sh, 0), 0.0)
        sh *= 2
    return x


def _scan_masks(c):
    lanes = GROUP * HEAD
    gc = GROUP * c
    i0 = lambda shape: lax.broadcasted_iota(jnp.int32, shape, 0)
    i1 = lambda shape: lax.broadcasted_iota(jnp.int32, shape, 1)
    col_t = i1((c, gc)) % c
    row_t = i0((c, gc))
    level = []
    s = 1
    while s < c:
        level.append((row_t // (2 * s) == col_t // (2 * s)) & (row_t % (2 * s) >= s)
                     & (col_t % (2 * s) < s))
        s *= 2
    return dict(
        level=level,
        lane_head=(i0((gc, lanes)) // c) == (i1((gc, lanes)) // HEAD),
        strict=col_t < i0((c, gc)),
        incl=col_t <= i0((c, gc)),
        bd_cc=(i0((gc, gc)) // c) == (i1((gc, gc)) // c),
        bd_ll=(i0((lanes, lanes)) // HEAD) == (i1((lanes, lanes)) // HEAD),
    )


def _chunk_precompute(load, masks):
    lw, kk, a, r, k, v = load()
    c = lw.shape[0]
    assert c == HEAD, "side-by-side forms share lane offsets only when CHUNK == HEAD"

    def bd_rows(m):
        return jnp.where(masks["lane_head"], jnp.concatenate([m] * GROUP, axis=0), 0.0)

    def bd_mat(m):
        return jnp.where(masks["bd_cc"], jnp.concatenate([m] * GROUP, axis=0), 0.0)

    def heads_t(m):
        mt = jnp.transpose(m)
        return jnp.concatenate([mt[h * HEAD:(h + 1) * HEAD, :] for h in range(GROUP)], axis=1)

    cum = _cumsum_rows(lw)
    cum_last = cum[c - 1:c, :]
    beta = kk * a
    alpha_t = -kk * jnp.exp(cum - lw)
    r_t = r * jnp.exp(cum)
    inv = jnp.exp(-cum)
    rem = jnp.exp(cum_last - cum)
    ar = jnp.concatenate([alpha_t, r_t], axis=0)
    g_b = _mm_nt(ar, bd_rows(beta * inv))
    g_k = _mm_nt(ar, bd_rows(k * inv))
    yield
    a_ab = jnp.where(masks["strict"], g_b[:c], 0.0)
    a_rb = jnp.where(masks["incl"], g_b[c:], 0.0)
    a_ak = jnp.where(masks["strict"], g_k[:c], 0.0)
    a_rk = jnp.where(masks["incl"], g_k[c:], 0.0)

    b_t = heads_t(beta * rem)
    k_t = heads_t(k * rem)
    res_v = _mm(jnp.concatenate([a_ak, a_rk, k_t], axis=0), bd_rows(v))

    tm, cm = None, a_ab
    n_lvl = len(masks["level"])
    for lvl, corner in enumerate(masks["level"]):
        m_s = jnp.where(corner, cm, 0.0)
        rhs = bd_mat(m_s)
        if lvl == 0:
            res = _mm(cm, rhs)
            yield
            akv, rkv, nh_k = res_v[:c], res_v[c:2 * c], res_v[2 * c:]
            tm, cm = m_s, cm + res
        elif lvl < n_lvl - 1:
            res = _mm(jnp.concatenate([tm, cm], axis=0), rhs)
            yield
            tm, cm = tm + m_s + res[:c], cm + res[c:]
        else:
            res = _mm(tm, rhs)
            yield
            tm = tm + m_s + res

    wa = alpha_t + _mm(tm, bd_rows(alpha_t))
    uv = akv + _mm(tm, bd_rows(akv))
    yield
    resid = akv - uv + _mm(a_ab, bd_rows(uv))
    yield
    uv = uv + resid + _mm(tm, bd_rows(resid))
    yield
    lhs = jnp.concatenate([a_rb, b_t], axis=0)
    res_w = _mm(lhs, bd_rows(wa))
    res_u = _mm(lhs, bd_rows(uv))
    p_col = jnp.transpose(jnp.broadcast_to(jnp.exp(cum_last), (SUBLANES, cum.shape[1])))[:, 0:1]
    ph = jnp.concatenate([jnp.broadcast_to(p_col[h * HEAD:(h + 1) * HEAD, :], (HEAD, c))
                          for h in range(GROUP)], axis=1)
    yield
    rq = r_t + res_w[:c]
    yv = rkv + res_u[:c]
    mh = res_w[c:]
    nh = res_u[c:] + nh_k
    return rq, yv, mh, nh, ph


def _run_staggered(gens_by_chunk, skew, on_chunk_done):
    n = len(gens_by_chunk)
    results = [[None] * len(gens) for gens in gens_by_chunk]
    live = {(ci, j) for ci, gens in enumerate(gens_by_chunk) for j in range(len(gens))}
    n_done = 0
    rnd = 0
    while live:
        for ci in range(min(n, rnd // skew + 1)):
            for j, gen in enumerate(gens_by_chunk[ci]):
                if (ci, j) in live:
                    try:
                        next(gen)
                    except StopIteration as stop:
                        results[ci][j] = stop.value
                        live.discard((ci, j))
        while n_done < n and not any(ci == n_done for ci, _ in live):
            on_chunk_done(n_done, results[n_done])
            n_done += 1
        rnd += 1


SCAN_OPERANDS = ("lw", "kk", "a", "r", "k", "v")


def _pscan_kernel(lw_ref, kk_ref, a_ref, r_ref, k_ref, v_ref, y_ref, zout_ref, z_ref):
    @pl.when(pl.program_id(1) == 0)
    def _():
        z_ref[...] = jnp.zeros_like(z_ref)

    refs = (lw_ref, kk_ref, a_ref, r_ref, k_ref, v_ref)
    lanes = GROUP * HEAD
    n_groups = kk_ref.shape[1] // lanes
    n_chunks = kk_ref.shape[0] // CHUNK
    masks = _scan_masks(CHUNK)

    def loader(ci, g):
        rows = slice(ci * CHUNK, (ci + 1) * CHUNK)
        sl = slice(g * lanes, (g + 1) * lanes)
        return lambda: tuple(ref[rows, sl] for ref in refs)

    gens = [[_chunk_precompute(loader(ci, g), masks) for g in range(n_groups)]
            for ci in range(n_chunks)]
    zs = [z_ref[g] for g in range(n_groups)]

    def apply_chunk(ci, pre):
        res = [_mm(jnp.concatenate([pre[g][0], pre[g][2]], axis=0),
                   jnp.where(masks["bd_ll"], jnp.concatenate([zs[g]] * GROUP, axis=0), 0.0))
               for g in range(n_groups)]
        for g in range(n_groups):
            _, yv, _, nh, ph = pre[g]
            y_ref[ci * CHUNK:(ci + 1) * CHUNK, g * lanes:(g + 1) * lanes] = res[g][:CHUNK] + yv
            zs[g] = ph * zs[g] + res[g][CHUNK:] + nh

    _run_staggered(gens, SCAN_SKEW, apply_chunk)
    for g in range(n_groups):
        z_ref[g] = zs[g]
        zout_ref[g] = zs[g]


def _prompt_scan(lw, kk, a, r, k, v, tb):
    nb, t, w = kk.shape
    lanes = GROUP * HEAD
    ng = w // lanes
    spec = pl.BlockSpec((None, tb, w), lambda b, i: (b, i, 0))
    return pl.pallas_call(
        _pscan_kernel,
        grid=(nb, t // tb),
        in_specs=[spec] * 6,
        out_specs=[spec, pl.BlockSpec((None, ng, HEAD, lanes), lambda b, i: (b, 0, 0, 0))],
        out_shape=[jax.ShapeDtypeStruct((nb, t, w), F32),
                   jax.ShapeDtypeStruct((nb, ng, HEAD, lanes), F32)],
        scratch_shapes=[pltpu.VMEM((ng, HEAD, lanes), F32)],
        compiler_params=pltpu.CompilerParams(
            dimension_semantics=("arbitrary", "arbitrary"), vmem_limit_bytes=VMEM_LIMIT),
        name="prompt_scan",
    )(lw, kk, a, r, k, v)


LANE_SLAB = 128


def _sscan_kernel(kk_ref, a_ref, lw_ref, k_ref, r_ref, v_ref, s_in_ref, y_ref, s_out_ref,
                  fm_ref, yt_ref):
    n_t = kk_ref.shape[0]
    heads, n_k = s_in_ref.shape[:2]
    shape = s_in_ref.shape[2:]
    i_kk, i_kka, i_w, i_k, i_r, i_v = range(6)
    for t in range(n_t):
        kk_t = jnp.transpose(kk_ref[t])
        fm_ref[i_kk, t] = kk_t
        fm_ref[i_kka, t] = kk_t * jnp.transpose(a_ref[t])
        fm_ref[i_w, t] = jnp.exp(jnp.transpose(lw_ref[t]))
        fm_ref[i_k, t] = jnp.transpose(k_ref[t])
        fm_ref[i_r, t] = jnp.transpose(r_ref[t])
        fm_ref[i_v, t] = jnp.transpose(v_ref[t])

    zeros = jnp.zeros(shape, F32)
    for hd in range(heads):
        off = hd * HEAD

        def rows8(i, t, k8, off=off):
            start = pl.multiple_of(off + k8 * SUBLANES, SUBLANES)
            return fm_ref[i, t, pl.ds(start, SUBLANES), :]

        def first_sk(k8, acc, hd=hd, rows8=rows8):
            kk8 = rows8(i_kk, 0, k8)
            for j in range(SUBLANES):
                acc = acc + s_in_ref[hd, k8 * SUBLANES + j] * kk8[j:j + 1, :]
            return acc

        sk = lax.fori_loop(0, n_k // SUBLANES, first_sk, zeros)
        for t in range(n_t):
            src = s_in_ref if t == 0 else s_out_ref
            vt = fm_ref[i_v, t, off:off + HEAD, :]
            last = t == n_t - 1

            def step(k8, carry, hd=hd, rows8=rows8, t=t, src=src, vt=vt, sk=sk, last=last):
                acc, sk_next = carry
                kka8 = rows8(i_kka, t, k8)
                w8 = rows8(i_w, t, k8)
                k8v = rows8(i_k, t, k8)
                r8 = rows8(i_r, t, k8)
                kkn8 = None if last else rows8(i_kk, t + 1, k8)
                for j in range(SUBLANES):
                    snew = (src[hd, k8 * SUBLANES + j] * w8[j:j + 1, :] - sk * kka8[j:j + 1, :]
                            + vt * k8v[j:j + 1, :])
                    s_out_ref[hd, k8 * SUBLANES + j] = snew
                    acc = acc + snew * r8[j:j + 1, :]
                    if not last:
                        sk_next = sk_next + snew * kkn8[j:j + 1, :]
                return acc, sk_next

            y, sk = lax.fori_loop(0, n_k // SUBLANES, step, (zeros, zeros))
            yt_ref[t, off:off + HEAD, :] = y
    for t in range(n_t):
        y_ref[t] = jnp.transpose(yt_ref[t])


def _sample_scan(kk, a, lw, k, r, v, state):
    n_t, nb, w = kk.shape
    heads = LANE_SLAB // HEAD
    spec = pl.BlockSpec((n_t, nb, LANE_SLAB), lambda i: (0, 0, i))
    sspec = pl.BlockSpec((heads, HEAD, HEAD, nb), lambda i: (i, 0, 0, 0))
    return pl.pallas_call(
        _sscan_kernel,
        grid=(w // LANE_SLAB,),
        in_specs=[spec] * 6 + [sspec],
        out_specs=[spec, sspec],
        out_shape=[jax.ShapeDtypeStruct((n_t, nb, w), F32),
                   jax.ShapeDtypeStruct(state.shape, F32)],
        scratch_shapes=[pltpu.VMEM((6, n_t, LANE_SLAB, nb), F32),
                        pltpu.VMEM((n_t, LANE_SLAB, nb), F32)],
        compiler_params=pltpu.CompilerParams(
            dimension_semantics=("arbitrary",), vmem_limit_bytes=VMEM_LIMIT),
        name="sample_scan",
    )(kk, a, lw, k, r, v, state)


def kernel(x_prompt, x_sample, state_conv, state_shift, state_wkv, norm1_g, w_in, conv_w, mu, w0,
           w2, a0, a2, g2, k_k, k_a, r_k, lnx_g, lnx_b, w_out, norm2_g, w_ff1, w_ff2, normf_g):
    depth = w_in.shape[0]
    assert depth == 1, "single-layer step only"
    bp, seq, d = x_prompt.shape
    bs, dseq, _ = x_sample.shape
    cw_dim = conv_w.shape[-1]
    rw_dim = w0.shape[-1]
    rcols = mu.shape[-1]
    nh = rw_dim // HEAD
    row = lambda t: t[0].reshape(1, -1)
    seg = jnp.arange(GROUP * HEAD) // HEAD
    wts = {
        "norm1_g": row(norm1_g), "w_in": w_in[0].astype(BF16), "conv_w": conv_w[0],
        "mu": row(mu), "w0": row(w0), "w2": w2[0].astype(BF16), "a0": row(a0),
        "a2": a2[0].astype(BF16), "g2": g2[0].astype(BF16), "k_k": row(k_k), "k_a": row(k_a),
        "r_k": row(r_k), "lnx_g": row(lnx_g), "lnx_b": row(lnx_b),
        "w_out": w_out[0].astype(BF16), "norm2_g": row(norm2_g),
        "w_ff1": w_ff1[0].astype(BF16), "w_ff2": w_ff2[0].astype(BF16),
        "normf_g": normf_g.reshape(1, -1),
        "bd": (seg[:, None] == seg[None, :]).astype(BF16),
    }

    zc = jnp.zeros((bp, 2, cw_dim), F32)
    zs = jnp.zeros((bp, 1, rcols), F32)
    yconv, r, k, v, kk, a, lw, g, nconv_p, nshift_p = _front(x_prompt, zc, zs, 1, FRONT_ROWS, wts)
    y, z = _prompt_scan(lw, kk, a, r, k, v, SCAN_ROWS)
    y_prompt = _back(x_prompt, yconv, y, r, k, v, g, BACK_ROWS, wts)
    ng = rw_dim // (GROUP * HEAD)
    zb = z.reshape(bp, ng, HEAD, GROUP, HEAD)
    wkv_p = jnp.transpose(zb, (0, 1, 3, 4, 2)).reshape(bp, nh, HEAD, HEAD)

    xs = jnp.transpose(x_sample, (1, 0, 2)).reshape(1, dseq * bs, d)
    c0 = jnp.transpose(state_conv[0], (1, 0, 2)).reshape(1, 2 * bs, cw_dim)
    s0 = state_shift[0].reshape(1, bs, rcols)
    tbs = SAMPLE_STEPS * bs
    yconv, r, k, v, kk, a, lw, g, nconv_s, nshift_s = _front(xs, c0, s0, bs, tbs, wts)
    tm = lambda t: t.reshape(dseq, bs, rw_dim)
    st = jnp.transpose(state_wkv[0], (1, 3, 2, 0))
    y_tm, st_new = _sample_scan(tm(kk), tm(a), tm(lw), tm(k), tm(r), tm(v), st)
    y = y_tm.reshape(1, dseq * bs, rw_dim)
    ys = _back(xs, yconv, y, r, k, v, g, tbs, wts)
    y_sample = jnp.transpose(ys.reshape(dseq, bs, d), (1, 0, 2))
    wkv_s = jnp.transpose(st_new, (3, 0, 2, 1))
    nconv_s = jnp.transpose(nconv_s.reshape(2, bs, cw_dim), (1, 0, 2))

    return (y_prompt, y_sample,
            nconv_p[None], nshift_p.reshape(1, bp, rcols), wkv_p[None],
            nconv_s[None], nshift_s.reshape(1, bs, rcols), wkv_s[None])
```

```python
import functools

import jax
import jax.numpy as jnp
from jax import lax
from jax.experimental import pallas as pl
from jax.experimental.pallas import tpu as pltpu

F32 = jnp.float32
BF16 = jnp.bfloat16

HEAD = 64
SUBLANES = 8
RMS_EPS = 1e-6
GN_EPS = 64e-5
NORM_EPS = 1e-12
VMEM_LIMIT = 56 * 1024 * 1024
CHUNK = 64
GROUP = 4
SCAN_SKEW = 1
FRONT_ROWS = 512
SCAN_ROWS = 1024
BACK_ROWS = 512
SAMPLE_STEPS = 4


def _dot(a, b):
    return jnp.dot(a, b, preferred_element_type=F32)


def _sigmoid(x):
    return 0.5 * jnp.tanh(0.5 * x) + 0.5


def _rmsnorm(x, g):
    return x * lax.rsqrt(jnp.mean(x * x, axis=-1, keepdims=True) + RMS_EPS) * g


def _seg_sum(x, bd):
    lanes = bd.shape[0]
    xb = x.astype(BF16)
    return jnp.concatenate([_dot(xb[:, i:i + lanes], bd) for i in range(0, x.shape[1], lanes)],
                           axis=1)


def _shift_rows(x, k, head):
    rows = x.shape[0]
    if k == rows:
        return head
    if k % SUBLANES == 0:
        return jnp.concatenate([head, x[:rows - k]], axis=0)
    rolled = pltpu.roll(x, k, 0)
    row = lax.broadcasted_iota(jnp.int32, x.shape, 0)
    out = rolled
    for j in range(k):
        out = jnp.where(row == j, head[j:j + 1, :], out)
    return out


FRONT_WEIGHTS = ("norm1_g", "w_in", "conv_w", "mu", "w0", "w2", "a0", "a2", "g2", "k_k", "k_a",
                 "bd", "w_out_conv")


def _front_block(x, cu, cp, s, w, store):
    rows = x.shape[0]
    cw_dim = cu.shape[1]
    rw_dim = w["w0"].shape[1]
    rcols = cp.shape[1]
    win = w["w_in"]
    h = _rmsnorm(x, w["norm1_g"][...]).astype(BF16)

    def proj(c0, c1):
        return _dot(h, win[:, c0:c1])

    def rwkv_piece(c0, c1):
        return proj(3 * cw_dim + c0, 3 * cw_dim + c1)

    def mixed(p, c0, c1):
        prev = _shift_rows(p, s, cp[:, c0:c1])
        return p + (prev - p) * w["mu"][:, c0:c1], p[rows - s:]

    p_lora = rwkv_piece(3 * rw_dim, rcols)
    p_k = rwkv_piece(rw_dim, 2 * rw_dim)
    p_r = rwkv_piece(0, rw_dim)
    p_v = rwkv_piece(2 * rw_dim, 3 * rw_dim)
    c_c = proj(cw_dim, 2 * cw_dim)
    lora, last_l = mixed(p_lora, 3 * rw_dim, rcols)
    d_w = w["w2"].shape[0]
    d_a = w["a2"].shape[0]
    pw = lora[:, :d_w]
    pa = lora[:, d_w:d_w + d_a]
    pg = lora[:, d_w + d_a:]
    wl = w["w0"][...] + _dot(jnp.tanh(pw).astype(BF16), w["w2"][...])
    z = -wl
    softplus = jnp.maximum(z, 0.0) + jnp.log(1.0 + jnp.exp(-jnp.abs(z)))
    store("lw", -jnp.exp(-softplus - 0.5))
    a = _sigmoid(w["a0"][...] + _dot(pa.astype(BF16), w["a2"][...]))
    store("a", a)
    store("g", _dot(_sigmoid(pg).astype(BF16), w["g2"][...]))

    c_x = proj(2 * cw_dim, 3 * cw_dim)
    k, last_k = mixed(p_k, rw_dim, 2 * rw_dim)
    kk = k * w["k_k"][...]
    store("kk", kk * lax.rsqrt(jnp.maximum(_seg_sum(kk * kk, w["bd"][...]), NORM_EPS * NORM_EPS)))
    store("k", k * (1.0 + (a - 1.0) * w["k_a"][...]))

    c_b = proj(0, cw_dim)
    u = c_c * c_x
    u1 = _shift_rows(u, s, cu[s:2 * s])
    u2 = _shift_rows(u, 2 * s, cu)
    cw = w["conv_w"][...]
    conv = cw[0:1] * u2 + cw[1:2] * u1 + cw[2:3] * u
    store("new_cu", u[rows - 2 * s:])
    out_conv = _dot((c_b * conv).astype(BF16), w["w_out_conv"][...])

    r, last_r = mixed(p_r, 0, rw_dim)
    store("r", r)
    v, last_v = mixed(p_v, 2 * rw_dim, 3 * rw_dim)
    store("v", v)
    store("new_cp", jnp.concatenate([last_r, last_k, last_v, last_l], axis=1))
    store("xc", x + out_conv)


FRONT_OUTPUTS = ("xc", "r", "k", "v", "kk", "a", "lw", "g", "new_cu", "new_cp")


def _front_kernel(s, x_ref, conv0_ref, shift0_ref, *refs):
    nw = len(FRONT_WEIGHTS)
    w = dict(zip(FRONT_WEIGHTS, refs[:nw]))
    outs = dict(zip(FRONT_OUTPUTS, refs[nw:nw + len(FRONT_OUTPUTS)]))
    cu_ref, cp_ref = refs[nw + len(FRONT_OUTPUTS):]

    @pl.when(pl.program_id(1) == 0)
    def _():
        cu_ref[...] = conv0_ref[...]
        cp_ref[...] = shift0_ref[...]

    carry = {}

    def store(name, value):
        outs[name][...] = value
        carry[name] = value

    _front_block(x_ref[...], cu_ref[...], cp_ref[...], s, w, store)
    cu_ref[...] = carry["new_cu"]
    cp_ref[...] = carry["new_cp"]


def _const_spec(shape):
    nd = len(shape)
    return pl.BlockSpec(shape, lambda *_: (0,) * nd, pipeline_mode=pl.Buffered(1))


def _front(x, conv0, shift0, s, tb, wts):
    ns, tt, d = x.shape
    cw_dim = conv0.shape[-1]
    rw_dim = wts["w0"].shape[-1]
    rcols = shift0.shape[-1]
    nblk = tt // tb
    row_spec = lambda w: pl.BlockSpec((None, tb, w), lambda b, i: (b, i, 0))
    st_spec = lambda r, w: pl.BlockSpec((None, r, w), lambda b, i: (b, 0, 0))
    w_in = [wts[n] for n in FRONT_WEIGHTS]
    out_shapes = ([jax.ShapeDtypeStruct((ns, tt, d), F32)]
                  + [jax.ShapeDtypeStruct((ns, tt, rw_dim), F32)] * 7
                  + [jax.ShapeDtypeStruct((ns, 2 * s, cw_dim), F32),
                     jax.ShapeDtypeStruct((ns, s, rcols), F32)])
    out_specs = ([row_spec(d)] + [row_spec(rw_dim)] * 7
                 + [st_spec(2 * s, cw_dim), st_spec(s, rcols)])
    return pl.pallas_call(
        functools.partial(_front_kernel, s),
        grid=(ns, nblk),
        in_specs=[row_spec(d), st_spec(2 * s, cw_dim), st_spec(s, rcols)]
        + [_const_spec(w.shape) for w in w_in],
        out_specs=out_specs,
        out_shape=out_shapes,
        scratch_shapes=[pltpu.VMEM((2 * s, cw_dim), F32), pltpu.VMEM((s, rcols), F32)],
        compiler_params=pltpu.CompilerParams(
            dimension_semantics=("arbitrary", "arbitrary"), vmem_limit_bytes=VMEM_LIMIT),
        name="front",
    )(x, conv0, shift0, *w_in)


def _back_kernel(xc_ref, y_ref, r_ref, k_ref, v_ref, g_ref,
                 rk_ref, lg_ref, lb_ref, wo_ref, n2_ref, f1_ref, f2_ref, nf_ref, bd_ref,
                 out_ref):
    bd = bd_ref[...]
    y = y_ref[...]
    inv_n = 1.0 / HEAD
    mean = _seg_sum(y, bd) * inv_n
    dlt = y - mean
    var = _seg_sum(dlt * dlt, bd) * inv_n
    yn = dlt * lax.rsqrt(var + GN_EPS) * lg_ref[...] + lb_ref[...]
    v = v_ref[...]
    bonus = _seg_sum(r_ref[...] * k_ref[...] * rk_ref[...], bd) * v
    y_rwkv = ((yn + bonus) * g_ref[...]).astype(BF16)
    x1 = xc_ref[...] + _dot(y_rwkv, wo_ref[...])
    h2 = _rmsnorm(x1, n2_ref[...]).astype(BF16)
    f = jnp.maximum(_dot(h2, f1_ref[...]), 0.0)
    x2 = x1 + _dot((f * f).astype(BF16), f2_ref[...])
    out_ref[...] = _rmsnorm(x2, nf_ref[...])


def _back(xc, y, r, k, v, g, tb, wts):
    ns, tt, d = xc.shape
    nblk = tt // tb
    row_spec = lambda w: pl.BlockSpec((None, tb, w), lambda b, i: (b, i, 0))
    names = ["r_k", "lnx_g", "lnx_b", "w_out_rwkv", "norm2_g", "w_ff1", "w_ff2", "normf_g", "bd"]
    w_in = [wts[n] for n in names]
    acts = [xc, y, r, k, v, g]
    return pl.pallas_call(
        _back_kernel,
        grid=(ns, nblk),
        in_specs=[row_spec(a.shape[-1]) for a in acts] + [_const_spec(w.shape) for w in w_in],
        out_specs=row_spec(d),
        out_shape=jax.ShapeDtypeStruct((ns, tt, d), F32),
        compiler_params=pltpu.CompilerParams(
            dimension_semantics=("arbitrary", "arbitrary"), vmem_limit_bytes=VMEM_LIMIT),
        name="back",
    )(*acts, *w_in)


def _mm(a, b):
    return jnp.dot(a.astype(BF16), b.astype(BF16), preferred_element_type=F32)


def _mm_nt(a, b):
    return lax.dot_general(a.astype(BF16), b.astype(BF16), (((1,), (1,)), ((), ())),
                           preferred_element_type=F32)


def _cumsum_rows(x):
    rows = x.shape[0]
    row = lax.broadcasted_iota(jnp.int32, x.shape, 0)
    sh = 1
    while sh < rows:
        x = x + jnp.where(row >= sh, pltpu.roll(x, sh, 0), 0.0)
        sh *= 2
    return x


def _scan_masks(c):
    lanes = GROUP * HEAD
    gc = GROUP * c
    i0 = lambda shape: lax.broadcasted_iota(jnp.int32, shape, 0)
    i1 = lambda shape: lax.broadcasted_iota(jnp.int32, shape, 1)
    col_t = i1((c, gc)) % c
    row_t = i0((c, gc))
    level = []
    s = 1
    while s < c:
        level.append((row_t // (2 * s) == col_t // (2 * s)) & (row_t % (2 * s) >= s)
                     & (col_t % (2 * s) < s))
        s *= 2
    return dict(
        level=level,
        lane_head=(i0((gc, lanes)) // c) == (i1((gc, lanes)) // HEAD),
        strict=col_t < i0((c, gc)),
        incl=col_t <= i0((c, gc)),
        bd_cc=(i0((gc, gc)) // c) == (i1((gc, gc)) // c),
        bd_ll=(i0((lanes, lanes)) // HEAD) == (i1((lanes, lanes)) // HEAD),
    )


def _chunk_precompute(load, masks):
    lw, kk, a, r, k, v = load()
    c = lw.shape[0]
    assert c == HEAD, "side-by-side forms share lane offsets only when CHUNK == HEAD"

    def bd_rows(m):
        return jnp.where(masks["lane_head"], jnp.concatenate([m] * GROUP, axis=0), 0.0)

    def bd_mat(m):
        return jnp.where(masks["bd_cc"], jnp.concatenate([m] * GROUP, axis=0), 0.0)

    def heads_t(m):
        mt = jnp.transpose(m)
        return jnp.concatenate([mt[h * HEAD:(h + 1) * HEAD, :] for h in range(GROUP)], axis=1)

    cum = _cumsum_rows(lw)
    cum_last = cum[c - 1:c, :]
    beta = kk * a
    alpha_t = -kk * jnp.exp(cum - lw)
    r_t = r * jnp.exp(cum)
    inv = jnp.exp(-cum)
    rem = jnp.exp(cum_last - cum)
    ar = jnp.concatenate([alpha_t, r_t], axis=0)
    g_b = _mm_nt(ar, bd_rows(beta * inv))
    g_k = _mm_nt(ar, bd_rows(k * inv))
    yield
    a_ab = jnp.where(masks["strict"], g_b[:c], 0.0)
    a_rb = jnp.where(masks["incl"], g_b[c:], 0.0)
    a_ak = jnp.where(masks["strict"], g_k[:c], 0.0)
    a_rk = jnp.where(masks["incl"], g_k[c:], 0.0)

    b_t = heads_t(beta * rem)
    k_t = heads_t(k * rem)
    res_v = _mm(jnp.concatenate([a_ak, a_rk, k_t], axis=0), bd_rows(v))

    tm, cm = None, a_ab
    n_lvl = len(masks["level"])
    for lvl, corner in enumerate(masks["level"]):
        m_s = jnp.where(corner, cm, 0.0)
        rhs = bd_mat(m_s)
        if lvl == 0:
            res = _mm(cm, rhs)
            yield
            akv, rkv, nh_k = res_v[:c], res_v[c:2 * c], res_v[2 * c:]
            tm, cm = m_s, cm + res
        elif lvl < n_lvl - 1:
            res = _mm(jnp.concatenate([tm, cm], axis=0), rhs)
            yield
            tm, cm = tm + m_s + res[:c], cm + res[c:]
        else:
            res = _mm(tm, rhs)
            yield
            tm = tm + m_s + res

    wa = alpha_t + _mm(tm, bd_rows(alpha_t))
    uv = akv + _mm(tm, bd_rows(akv))
    yield
    resid = akv - uv + _mm(a_ab, bd_rows(uv))
    yield
    uv = uv + resid + _mm(tm, bd_rows(resid))
    yield
    lhs = jnp.concatenate([a_rb, b_t], axis=0)
    res_w = _mm(lhs, bd_rows(wa))
    res_u = _mm(lhs, bd_rows(uv))
    p_col = jnp.transpose(jnp.broadcast_to(jnp.exp(cum_last), (SUBLANES, cum.shape[1])))[:, 0:1]
    ph = jnp.concatenate([jnp.broadcast_to(p_col[h * HEAD:(h + 1) * HEAD, :], (HEAD, c))
                          for h in range(GROUP)], axis=1)
    yield
    rq = r_t + res_w[:c]
    yv = rkv + res_u[:c]
    mh = res_w[c:]
    nh = res_u[c:] + nh_k
    return rq, yv, mh, nh, ph


def _run_staggered(gens_by_chunk, skew, on_chunk_done):
    n = len(gens_by_chunk)
    results = [[None] * len(gens) for gens in gens_by_chunk]
    live = {(ci, j) for ci, gens in enumerate(gens_by_chunk) for j in range(len(gens))}
    n_done = 0
    rnd = 0
    while live:
        for ci in range(min(n, rnd // skew + 1)):
            for j, gen in enumerate(gens_by_chunk[ci]):
                if (ci, j) in live:
                    try:
                        next(gen)
                    except StopIteration as stop:
                        results[ci][j] = stop.value
                        live.discard((ci, j))
        while n_done < n and not any(ci == n_done for ci, _ in live):
            on_chunk_done(n_done, results[n_done])
            n_done += 1
        rnd += 1


SCAN_OPERANDS = ("lw", "kk", "a", "r", "k", "v")


def _pscan_kernel(lw_ref, kk_ref, a_ref, r_ref, k_ref, v_ref, y_ref, zout_ref, z_ref):
    @pl.when(pl.program_id(1) == 0)
    def _():
        z_ref[...] = jnp.zeros_like(z_ref)

    refs = (lw_ref, kk_ref, a_ref, r_ref, k_ref, v_ref)
    lanes = GROUP * HEAD
    n_groups = kk_ref.shape[1] // lanes
    n_chunks = kk_ref.shape[0] // CHUNK
    masks = _scan_masks(CHUNK)

    def loader(ci, g):
        rows = slice(ci * CHUNK, (ci + 1) * CHUNK)
        sl = slice(g * lanes, (g + 1) * lanes)
        return lambda: tuple(ref[rows, sl] for ref in refs)

    gens = [[_chunk_precompute(loader(ci, g), masks) for g in range(n_groups)]
            for ci in range(n_chunks)]
    zs = [z_ref[g] for g in range(n_groups)]

    def apply_chunk(ci, pre):
        res = [_mm(jnp.concatenate([pre[g][0], pre[g][2]], axis=0),
                   jnp.where(masks["bd_ll"], jnp.concatenate([zs[g]] * GROUP, axis=0), 0.0))
               for g in range(n_groups)]
        for g in range(n_groups):
            _, yv, _, nh, ph = pre[g]
            y_ref[ci * CHUNK:(ci + 1) * CHUNK, g * lanes:(g + 1) * lanes] = res[g][:CHUNK] + yv
            zs[g] = ph * zs[g] + res[g][CHUNK:] + nh

    _run_staggered(gens, SCAN_SKEW, apply_chunk)
    for g in range(n_groups):
        z_ref[g] = zs[g]
        zout_ref[g] = zs[g]


def _prompt_scan(lw, kk, a, r, k, v, tb):
    nb, t, w = kk.shape
    lanes = GROUP * HEAD
    ng = w // lanes
    spec = pl.BlockSpec((None, tb, w), lambda b, i: (b, i, 0))
    return pl.pallas_call(
        _pscan_kernel,
        grid=(nb, t // tb),
        in_specs=[spec] * 6,
        out_specs=[spec, pl.BlockSpec((None, ng, HEAD, lanes), lambda b, i: (b, 0, 0, 0))],
        out_shape=[jax.ShapeDtypeStruct((nb, t, w), F32),
                   jax.ShapeDtypeStruct((nb, ng, HEAD, lanes), F32)],
        scratch_shapes=[pltpu.VMEM((ng, HEAD, lanes), F32)],
        compiler_params=pltpu.CompilerParams(
            dimension_semantics=("arbitrary", "arbitrary"), vmem_limit_bytes=VMEM_LIMIT),
        name="prompt_scan",
    )(lw, kk, a, r, k, v)


LANE_SLAB = 128


def _sscan_kernel(kk_ref, a_ref, lw_ref, k_ref, r_ref, v_ref, s_in_ref, y_ref, s_out_ref,
                  fm_ref, yt_ref):
    n_t = kk_ref.shape[0]
    heads, n_k = s_in_ref.shape[:2]
    shape = s_in_ref.shape[2:]
    i_kk, i_kka, i_w, i_k, i_r, i_v = range(6)
    for t in range(n_t):
        kk_t = jnp.transpose(kk_ref[t])
        fm_ref[i_kk, t] = kk_t
        fm_ref[i_kka, t] = kk_t * jnp.transpose(a_ref[t])
        fm_ref[i_w, t] = jnp.exp(jnp.transpose(lw_ref[t]))
        fm_ref[i_k, t] = jnp.transpose(k_ref[t])
        fm_ref[i_r, t] = jnp.transpose(r_ref[t])
        fm_ref[i_v, t] = jnp.transpose(v_ref[t])

    zeros = jnp.zeros(shape, F32)
    for hd in range(heads):
        off = hd * HEAD

        def rows8(i, t, k8, off=off):
            start = pl.multiple_of(off + k8 * SUBLANES, SUBLANES)
            return fm_ref[i, t, pl.ds(start, SUBLANES), :]

        def first_sk(k8, acc, hd=hd, rows8=rows8):
            kk8 = rows8(i_kk, 0, k8)
            for j in range(SUBLANES):
                acc = acc + s_in_ref[hd, k8 * SUBLANES + j] * kk8[j:j + 1, :]
            return acc

        sk = lax.fori_loop(0, n_k // SUBLANES, first_sk, zeros)
        for t in range(n_t):
            src = s_in_ref if t == 0 else s_out_ref
            vt = fm_ref[i_v, t, off:off + HEAD, :]
            last = t == n_t - 1

            def step(k8, carry, hd=hd, rows8=rows8, t=t, src=src, vt=vt, sk=sk, last=last):
                acc, sk_next = carry
                kka8 = rows8(i_kka, t, k8)
                w8 = rows8(i_w, t, k8)
                k8v = rows8(i_k, t, k8)
                r8 = rows8(i_r, t, k8)
                kkn8 = None if last else rows8(i_kk, t + 1, k8)
                for j in range(SUBLANES):
                    snew = (src[hd, k8 * SUBLANES + j] * w8[j:j + 1, :] - sk * kka8[j:j + 1, :]
                            + vt * k8v[j:j + 1, :])
                    s_out_ref[hd, k8 * SUBLANES + j] = snew
                    acc = acc + snew * r8[j:j + 1, :]
                    if not last:
                        sk_next = sk_next + snew * kkn8[j:j + 1, :]
                return acc, sk_next

            y, sk = lax.fori_loop(0, n_k // SUBLANES, step, (zeros, zeros))
            yt_ref[t, off:off + HEAD, :] = y
    for t in range(n_t):
        y_ref[t] = jnp.transpose(yt_ref[t])


def _sample_scan(kk, a, lw, k, r, v, state):
    n_t, nb, w = kk.shape
    heads = LANE_SLAB // HEAD
    spec = pl.BlockSpec((n_t, nb, LANE_SLAB), lambda i: (0, 0, i))
    sspec = pl.BlockSpec((heads, HEAD, HEAD, nb), lambda i: (i, 0, 0, 0))
    return pl.pallas_call(
        _sscan_kernel,
        grid=(w // LANE_SLAB,),
        in_specs=[spec] * 6 + [sspec],
        out_specs=[spec, sspec],
        out_shape=[jax.ShapeDtypeStruct((n_t, nb, w), F32),
                   jax.ShapeDtypeStruct(state.shape, F32)],
        scratch_shapes=[pltpu.VMEM((6, n_t, LANE_SLAB, nb), F32),
                        pltpu.VMEM((n_t, LANE_SLAB, nb), F32)],
        compiler_params=pltpu.CompilerParams(
            dimension_semantics=("arbitrary",), vmem_limit_bytes=VMEM_LIMIT),
        name="sample_scan",
    )(kk, a, lw, k, r, v, state)


def kernel(x_prompt, x_sample, state_conv, state_shift, state_wkv, norm1_g, w_in, conv_w, mu, w0,
           w2, a0, a2, g2, k_k, k_a, r_k, lnx_g, lnx_b, w_out, norm2_g, w_ff1, w_ff2, normf_g):
    depth = w_in.shape[0]
    assert depth == 1, "single-layer step only"
    bp, seq, d = x_prompt.shape
    bs, dseq, _ = x_sample.shape
    cw_dim = conv_w.shape[-1]
    rw_dim = w0.shape[-1]
    rcols = mu.shape[-1]
    nh = rw_dim // HEAD
    row = lambda t: t[0].reshape(1, -1)
    seg = jnp.arange(GROUP * HEAD) // HEAD
    wts = {
        "norm1_g": row(norm1_g), "w_in": w_in[0].astype(BF16), "conv_w": conv_w[0],
        "mu": row(mu), "w0": row(w0), "w2": w2[0].astype(BF16), "a0": row(a0),
        "a2": a2[0].astype(BF16), "g2": g2[0].astype(BF16), "k_k": row(k_k), "k_a": row(k_a),
        "r_k": row(r_k), "lnx_g": row(lnx_g), "lnx_b": row(lnx_b),
        "w_out_conv": w_out[0, :cw_dim].astype(BF16), "w_out_rwkv": w_out[0, cw_dim:].astype(BF16),
        "norm2_g": row(norm2_g),
        "w_ff1": w_ff1[0].astype(BF16), "w_ff2": w_ff2[0].astype(BF16),
        "normf_g": normf_g.reshape(1, -1),
        "bd": (seg[:, None] == seg[None, :]).astype(BF16),
    }

    zc = jnp.zeros((bp, 2, cw_dim), F32)
    zs = jnp.zeros((bp, 1, rcols), F32)
    xc, r, k, v, kk, a, lw, g, nconv_p, nshift_p = _front(x_prompt, zc, zs, 1, FRONT_ROWS, wts)
    y, z = _prompt_scan(lw, kk, a, r, k, v, SCAN_ROWS)
    y_prompt = _back(xc, y, r, k, v, g, BACK_ROWS, wts)
    ng = rw_dim // (GROUP * HEAD)
    zb = z.reshape(bp, ng, HEAD, GROUP, HEAD)
    wkv_p = jnp.transpose(zb, (0, 1, 3, 4, 2)).reshape(bp, nh, HEAD, HEAD)

    xs = jnp.transpose(x_sample, (1, 0, 2)).reshape(1, dseq * bs, d)
    c0 = jnp.transpose(state_conv[0], (1, 0, 2)).reshape(1, 2 * bs, cw_dim)
    s0 = state_shift[0].reshape(1, bs, rcols)
    tbs = SAMPLE_STEPS * bs
    xc, r, k, v, kk, a, lw, g, nconv_s, nshift_s = _front(xs, c0, s0, bs, tbs, wts)
    tm = lambda t: t.reshape(dseq, bs, rw_dim)
    st = jnp.transpose(state_wkv[0], (1, 3, 2, 0))
    y_tm, st_new = _sample_scan(tm(kk), tm(a), tm(lw), tm(k), tm(r), tm(v), st)
    y = y_tm.reshape(1, dseq * bs, rw_dim)
    ys = _back(xc, y, r, k, v, g, tbs, wts)
    y_sample = jnp.transpose(ys.reshape(dseq, bs, d), (1, 0, 2))
    wkv_s = jnp.transpose(st_new, (3, 0, 2, 1))
    nconv_s = jnp.transpose(nconv_s.reshape(2, bs, cw_dim), (1, 0, 2))

    return (y_prompt, y_sample,
            nconv_p[None], nshift_p.reshape(1, bp, rcols), wkv_p[None],
            nconv_s[None], nshift_s.reshape(1, bs, rcols), wkv_s[None])
```
